```python
import math
import jax
import jax.numpy as jnp
from jax import lax
import numpy as np

D_MODEL = 1024
BATCH = 32
SEQ = 256
DEPTH = 4
DEC_BATCH = 8
DEC_SEQ = 4096
PAST_LEN = 512

GRID_W = 64
GLA_HEADS = 4
GLA_KEY = D_MODEL // 2
GLA_VAL = D_MODEL
GLA_DK = GLA_KEY // GLA_HEADS
GLA_DV = GLA_VAL // GLA_HEADS
GATE_RANK = 16
GLA_TAU = 16.0
GLA_CHUNK = 64
S5_WIDTH = D_MODEL
S5_GROUP = 16
S5_GROUPS = S5_WIDTH // S5_GROUP
S5_STATE = 64
S5_CHUNK = 128
LRU_WIDTH = 3 * D_MODEL // 2
LRU_BLOCK = 128
LRU_BLOCKS = LRU_WIDTH // LRU_BLOCK
LRU_CONV = 4
LRU_C = 8.0
MLP_HIDDEN = 4 * D_MODEL
ALPHA = (2.0 * DEPTH) ** 0.25
BETA = (8.0 * DEPTH) ** -0.25
IN_WIDTHS = (GLA_KEY, GLA_KEY, GLA_VAL, GLA_VAL, 2 * GATE_RANK, S5_WIDTH, LRU_WIDTH, LRU_WIDTH, 3 * D_MODEL)
IN_COLS = sum(IN_WIDTHS)

kernel_name = "hybrid_gla_s5_rglru_flow_step"

F32 = jnp.float32


def _flip(z):
    return jnp.flip(z, axis=1)


def _in_splits():
    return tuple(int(i) for i in np.cumsum(IN_WIDTHS)[:-1])


def layer_norm(x, g, b, eps=1e-5):
    xf = x.astype(F32)
    mu = jnp.mean(xf, axis=-1, keepdims=True)
    var = jnp.mean(jnp.square(xf - mu), axis=-1, keepdims=True)
    y = (xf - mu) * lax.rsqrt(var + eps)
    return (y * g.astype(F32) + b.astype(F32)).astype(x.dtype)


def grid_pos_embed(n_tokens, dim):
    rows = n_tokens // GRID_W
    quarter = dim // 4
    omega = 1.0 / (10000.0 ** (jnp.arange(quarter, dtype=F32) / quarter))
    r = jnp.arange(rows, dtype=F32)[:, None, None] * omega
    cl = jnp.arange(GRID_W, dtype=F32)[None, :, None] * omega
    shape = (rows, GRID_W, quarter)
    emb = jnp.concatenate([jnp.broadcast_to(jnp.sin(r), shape), jnp.broadcast_to(jnp.cos(r), shape),
                           jnp.broadcast_to(jnp.sin(cl), shape), jnp.broadcast_to(jnp.cos(cl), shape)], axis=-1)
    return emb.reshape(rows * GRID_W, dim)


def gla_scan(q, k, v, log_a, s0):
    bsz, T, H, DK = q.shape
    DV = v.shape[-1]
    n = T // GLA_CHUNK

    def chunks(z):
        return z.reshape(bsz, n, GLA_CHUNK, H, z.shape[-1])

    q, k, v, log_a = chunks(q), chunks(k), chunks(v), chunks(log_a)
    cum = jnp.cumsum(log_a, axis=2)
    last = cum[:, :, -1:]
    q_dec = q * jnp.exp(cum)
    k_inv = k * jnp.exp(-cum)
    k_end = k * jnp.exp(last - cum)
    mask = jnp.tril(jnp.ones((GLA_CHUNK, GLA_CHUNK), dtype=bool))
    scores = jnp.where(mask, jnp.einsum("bnthk,bnshk->bnhts", q_dec, k_inv), 0.0)
    o_intra = jnp.einsum("bnhts,bnshv->bnthv", scores, v)
    upd = jnp.einsum("bnshk,bnshv->nbhkv", k_end, v)
    decay = jnp.exp(jnp.moveaxis(last[:, :, 0], 1, 0))

    def step(S, inp):
        d, u = inp
        return d[..., None] * S + u, S

    s_fin, s_start = lax.scan(step, s0, (decay, upd))
    o_inter = jnp.einsum("bnthk,nbhkv->bnthv", q_dec, s_start)
    return (o_intra + o_inter).reshape(bsz, T, H, DV), s_fin


def _complex_combine(e1, e2):
    a1r, a1i, b1r, b1i = e1
    a2r, a2i, b2r, b2i = e2
    return (a2r * a1r - a2i * a1i, a2r * a1i + a2i * a1r,
            a2r * b1r - a2i * b1i + b2r, a2r * b1i + a2i * b1r + b2i)


def _real_combine(e1, e2):
    return (e1[0] * e2[0], e2[0] * e1[1] + e2[1])


def s5_scan(u, lam_re, lam_im, log_step, b_re, b_im, c_re, c_im, s_re0, s_im0):
    bsz, T, G, I = u.shape
    n = T // S5_CHUNK
    step = jnp.exp(log_step)[:, None]
    mag = jnp.exp(lam_re * step)
    ang = lam_im * step
    ab_re, ab_im = mag * jnp.cos(ang), mag * jnp.sin(ang)
    den = jnp.square(lam_re) + jnp.square(lam_im)
    nr = ab_re - 1.0
    f_re = (nr * lam_re + ab_im * lam_im) / den
    f_im = (ab_im * lam_re - nr * lam_im) / den
    bb_re = f_re[..., None] * b_re - f_im[..., None] * b_im
    bb_im = f_re[..., None] * b_im + f_im[..., None] * b_re
    u_chunks = jnp.moveaxis(u.reshape(bsz, n, S5_CHUNK, G, I), 1, 0)

    def chunk_step(carry, u_c):
        s_re, s_im = carry
        x_re = jnp.einsum("bcgi,gpi->bcgp", u_c, bb_re)
        x_im = jnp.einsum("bcgi,gpi->bcgp", u_c, bb_im)
        x_re = x_re.at[:, 0].add(ab_re * s_re - ab_im * s_im)
        x_im = x_im.at[:, 0].add(ab_re * s_im + ab_im * s_re)
        a_re = jnp.broadcast_to(ab_re, x_re.shape)
        a_im = jnp.broadcast_to(ab_im, x_im.shape)
        _, _, h_re, h_im = lax.associative_scan(_complex_combine, (a_re, a_im, x_re, x_im), axis=1)
        y = jnp.einsum("bcgp,gip->bcgi", h_re, c_re) - jnp.einsum("bcgp,gip->bcgi", h_im, c_im)
        return (h_re[:, -1], h_im[:, -1]), y

    (s_re, s_im), ys = lax.scan(chunk_step, (s_re0, s_im0), u_chunks)
    return jnp.moveaxis(ys, 0, 1).reshape(bsz, T, G, I), s_re, s_im


def rglru_scan(x, gate_a, gate_i, lam, h0):
    log_a = -LRU_C * jax.nn.softplus(-lam) * jax.nn.sigmoid(gate_a)
    a = jnp.exp(log_a)
    b = jnp.sqrt(-jnp.expm1(2.0 * log_a)) * (jax.nn.sigmoid(gate_i) * x)
    b = b.at[:, 0].add(a[:, 0] * h0)
    _, h = lax.associative_scan(_real_combine, (a, b), axis=1)
    return h, h[:, -1]


def centred_depthwise_conv(x, w, bias):
    T = x.shape[1]
    left = LRU_CONV // 2
    right = LRU_CONV - 1 - left
    xp = jnp.pad(x, ((0, 0), (left, right), (0, 0)))
    return bias + sum(w[j] * xp[:, j:j + T] for j in range(LRU_CONV))


def mixer(h, lw, init):
    gla0, s5re0, s5im0, lru0 = [s.astype(F32) for s in init]
    bsz, T, _ = h.shape
    dt = h.dtype
    proj = h @ lw["w_in"]
    q, k, v, r, glr, u, lx, ly, gates = jnp.split(proj, _in_splits(), axis=-1)

    qh = q.astype(F32).reshape(bsz, T, GLA_HEADS, GLA_DK) * (GLA_DK ** -0.5)
    kh = k.astype(F32).reshape(bsz, T, GLA_HEADS, GLA_DK)
    vh = v.astype(F32).reshape(bsz, T, GLA_HEADS, GLA_DV)
    logits = jnp.einsum("bter,erk->btek", glr.astype(F32).reshape(bsz, T, 2, GATE_RANK),
                        lw["gla_w_gate"].astype(F32)) + lw["gla_b_gate"].astype(F32)
    log_a = (jax.nn.log_sigmoid(logits) / GLA_TAU).reshape(bsz, T, 2, GLA_HEADS, GLA_DK)
    o_f, sg_f = gla_scan(qh, kh, vh, log_a[:, :, 0], gla0[:, 0])
    o_b, sg_b = gla_scan(_flip(qh), _flip(kh), _flip(vh), _flip(log_a[:, :, 1]), gla0[:, 1])
    o = o_f + _flip(o_b)
    mu = jnp.mean(o, axis=-1, keepdims=True)
    var = jnp.mean(jnp.square(o - mu), axis=-1, keepdims=True)
    on = ((o - mu) * lax.rsqrt(var + 1e-5)).reshape(bsz, T, GLA_VAL)
    y_gla = (on * lw["gla_norm_g"].astype(F32) * jax.nn.silu(r.astype(F32))).astype(dt)

    uf = u.astype(F32)
    ug = uf.reshape(bsz, T, S5_GROUPS, S5_GROUP)
    s5p = [lw[nm].astype(F32) for nm in ("s5_b_re", "s5_b_im", "s5_c_re", "s5_c_im")]
    lam_re = lw["s5_lam_re"].astype(F32)
    lam_im = lw["s5_lam_im"].astype(F32)
    log_step = lw["s5_log_step"].astype(F32)
    ys_f, sr_f, si_f = s5_scan(ug, lam_re[0], lam_im[0], log_step[0], *s5p, s5re0[:, 0], s5im0[:, 0])
    ys_b, sr_b, si_b = s5_scan(_flip(ug), lam_re[1], lam_im[1], log_step[1], *s5p, s5re0[:, 1], s5im0[:, 1])
    ys = (ys_f + _flip(ys_b)).reshape(bsz, T, S5_WIDTH) + lw["s5_d"].astype(F32) * uf
    ys = jax.nn.gelu(ys).astype(dt)
    y_s5 = ys * jax.nn.sigmoid(ys @ lw["s5_w_glu"])

    xc = centred_depthwise_conv(lx, lw["lru_conv_w"], lw["lru_conv_b"]).astype(F32)
    xb = xc.reshape(bsz, T, LRU_BLOCKS, LRU_BLOCK)
    ga = jnp.einsum("btnc,encd->btend", xb, lw["lru_w_a"].astype(F32)).reshape(bsz, T, 2, LRU_WIDTH) \
        + lw["lru_b_a"].astype(F32)
    gi = jnp.einsum("btnc,encd->btend", xb, lw["lru_w_i"].astype(F32)).reshape(bsz, T, 2, LRU_WIDTH) \
        + lw["lru_b_i"].astype(F32)
    lam = lw["lru_lam"].astype(F32)
    hf, lf = rglru_scan(xc, ga[:, :, 0], gi[:, :, 0], lam[0], lru0[:, 0])
    hb, lb = rglru_scan(_flip(xc), _flip(ga[:, :, 1]), _flip(gi[:, :, 1]), lam[1], lru0[:, 1])
    y_lru = ((hf + _flip(hb)) * jax.nn.gelu(ly.astype(F32))).astype(dt)

    g_gla, g_s5, g_lru = jnp.split(jax.nn.sigmoid(gates), 3, axis=-1)
    merged = (g_gla * (y_gla @ lw["w_br_gla"]) + g_s5 * (y_s5 @ lw["w_br_s5"])
              + g_lru * (y_lru @ lw["w_br_lru"]))
    out = merged @ lw["w_out"]
    finals = (jnp.stack([sg_f, sg_b], axis=1), jnp.stack([sr_f, sr_b], axis=1),
              jnp.stack([si_f, si_b], axis=1), jnp.stack([lf, lb], axis=1))
    return out, finals


def trunk_layer(x, mod, lw, init):
    shift1, scale1, gate1, shift2, scale2, gate2 = jnp.split(mod[:, None, :], 6, axis=-1)
    mix, finals = mixer(x * (1 + scale1) + shift1, lw, init)
    x = layer_norm(ALPHA * x + gate1 * mix, lw["ln1_g"], lw["ln1_b"])
    hid = jnp.square(jax.nn.relu((x * (1 + scale2) + shift2) @ lw["w_mlp_in"]))
    x = layer_norm(ALPHA * x + gate2 * (hid @ lw["w_mlp_out"]), lw["ln2_g"], lw["ln2_b"])
    return x, finals


def setup_inputs(seed: int = 0) -> dict:
    key = jax.random.key(seed)
    ks = iter(jax.random.split(key, 64))
    L, D = DEPTH, D_MODEL

    def nrm(shape, scale):
        return scale * jax.random.normal(next(ks), shape, F32)

    def unif(shape, lo, hi):
        return jax.random.uniform(next(ks), shape, F32, lo, hi)

    lru_a8 = unif((L, 2, LRU_WIDTH), 0.9, 0.999)
    lru_s = lru_a8 ** (1.0 / LRU_C)
    inp = {}
    inp["x_prompt"] = nrm((BATCH, SEQ, D), 1.0)
    inp["x_sample"] = nrm((DEC_BATCH, DEC_SEQ, D), 1.0)
    inp["state_gla"] = nrm((DEC_BATCH, L, 2, GLA_HEADS, GLA_DK, GLA_DV), 1.0)
    inp["state_s5_re"] = nrm((DEC_BATCH, L, 2, S5_GROUPS, S5_STATE), 0.1)
    inp["state_s5_im"] = nrm((DEC_BATCH, L, 2, S5_GROUPS, S5_STATE), 0.1)
    inp["state_lru"] = nrm((DEC_BATCH, L, 2, LRU_WIDTH), 0.5)
    inp["c"] = nrm((DEC_BATCH, D), 1.0)
    inp["c_ctx"] = nrm((D,), 1.0)
    inp["w_ada"] = nrm((L, D, 6 * D), 0.5 * D ** -0.5)
    inp["b_ada"] = nrm((L, 6 * D), 0.02)
    inp["w_in"] = nrm((L, D, IN_COLS), D ** -0.5)
    inp["gla_w_gate"] = nrm((L, 2, GATE_RANK, GLA_KEY), GATE_RANK ** -0.5)
    inp["gla_b_gate"] = nrm((L, 2, GLA_KEY), 0.1)
    inp["gla_norm_g"] = 1.0 + nrm((L, GLA_VAL), 0.02)
    inp["s5_lam_re"] = -0.5 + nrm((L, 2, S5_GROUPS, S5_STATE), 0.01)
    inp["s5_lam_im"] = jnp.pi * jnp.arange(S5_STATE, dtype=F32) + nrm((L, 2, S5_GROUPS, S5_STATE), 0.01)
    inp["s5_log_step"] = unif((L, 2, S5_GROUPS), math.log(0.001), math.log(0.1))
    inp["s5_b_re"] = nrm((L, S5_GROUPS, S5_STATE, S5_GROUP), (2 * S5_GROUP) ** -0.5)
    inp["s5_b_im"] = nrm((L, S5_GROUPS, S5_STATE, S5_GROUP), (2 * S5_GROUP) ** -0.5)
    inp["s5_c_re"] = nrm((L, S5_GROUPS, S5_GROUP, S5_STATE), S5_STATE ** -0.5)
    inp["s5_c_im"] = nrm((L, S5_GROUPS, S5_GROUP, S5_STATE), S5_STATE ** -0.5)
    inp["s5_d"] = nrm((L, S5_WIDTH), 1.0)
    inp["s5_w_glu"] = nrm((L, S5_WIDTH, S5_WIDTH), S5_WIDTH ** -0.5)
    inp["lru_conv_w"] = nrm((L, LRU_CONV, LRU_WIDTH), LRU_CONV ** -0.5)
    inp["lru_conv_b"] = nrm((L, LRU_WIDTH), 0.02)
    inp["lru_w_a"] = nrm((L, 2, LRU_BLOCKS, LRU_BLOCK, LRU_BLOCK), LRU_BLOCK ** -0.5)
    inp["lru_b_a"] = nrm((L, 2, LRU_WIDTH), 0.1)
    inp["lru_w_i"] = nrm((L, 2, LRU_BLOCKS, LRU_BLOCK, LRU_BLOCK), LRU_BLOCK ** -0.5)
    inp["lru_b_i"] = nrm((L, 2, LRU_WIDTH), 0.1)
    inp["lru_lam"] = jnp.log(lru_s) - jnp.log1p(-lru_s)
    inp["w_br_gla"] = nrm((L, GLA_VAL, D), GLA_VAL ** -0.5)
    inp["w_br_s5"] = nrm((L, S5_WIDTH, D), S5_WIDTH ** -0.5)
    inp["w_br_lru"] = nrm((L, LRU_WIDTH, D), LRU_WIDTH ** -0.5)
    inp["w_out"] = nrm((L, D, D), BETA * D ** -0.5)
    inp["ln1_g"] = 1.0 + nrm((L, D), 0.02)
    inp["ln1_b"] = nrm((L, D), 0.02)
    inp["ln2_g"] = 1.0 + nrm((L, D), 0.02)
    inp["ln2_b"] = nrm((L, D), 0.02)
    inp["w_mlp_in"] = nrm((L, D, MLP_HIDDEN), D ** -0.5)
    inp["w_mlp_out"] = nrm((L, MLP_HIDDEN, D), BETA * MLP_HIDDEN ** -0.5)
    return inp


def reference(x_prompt, x_sample, state_gla, state_s5_re, state_s5_im, state_lru, c, c_ctx,
              w_ada, b_ada, w_in, gla_w_gate, gla_b_gate, gla_norm_g,
              s5_lam_re, s5_lam_im, s5_log_step, s5_b_re, s5_b_im, s5_c_re, s5_c_im, s5_d, s5_w_glu,
              lru_conv_w, lru_conv_b, lru_w_a, lru_b_a, lru_w_i, lru_b_i, lru_lam,
              w_br_gla, w_br_s5, w_br_lru, w_out, ln1_g, ln1_b, ln2_g, ln2_b, w_mlp_in, w_mlp_out):
    bp = x_prompt.shape[0]
    xp = x_prompt
    xs = x_sample + grid_pos_embed(x_sample.shape[1], D_MODEL).astype(x_sample.dtype)
    silu_ctx = jax.nn.silu(c_ctx)[None]
    silu_c = jax.nn.silu(c)
    zero_init = (jnp.zeros((bp, 2, GLA_HEADS, GLA_DK, GLA_DV), F32),
                 jnp.zeros((bp, 2, S5_GROUPS, S5_STATE), F32),
                 jnp.zeros((bp, 2, S5_GROUPS, S5_STATE), F32),
                 jnp.zeros((bp, 2, LRU_WIDTH), F32))
    fin_gla, fin_s5_re, fin_s5_im, fin_lru = [], [], [], []
    for l in range(DEPTH):
        lw = {"w_in": w_in[l], "gla_w_gate": gla_w_gate[l], "gla_b_gate": gla_b_gate[l],
              "gla_norm_g": gla_norm_g[l], "s5_lam_re": s5_lam_re[l], "s5_lam_im": s5_lam_im[l],
              "s5_log_step": s5_log_step[l], "s5_b_re": s5_b_re[l], "s5_b_im": s5_b_im[l],
              "s5_c_re": s5_c_re[l], "s5_c_im": s5_c_im[l], "s5_d": s5_d[l], "s5_w_glu": s5_w_glu[l],
              "lru_conv_w": lru_conv_w[l], "lru_conv_b": lru_conv_b[l], "lru_w_a": lru_w_a[l],
              "lru_b_a": lru_b_a[l], "lru_w_i": lru_w_i[l], "lru_b_i": lru_b_i[l], "lru_lam": lru_lam[l],
              "w_br_gla": w_br_gla[l], "w_br_s5": w_br_s5[l], "w_br_lru": w_br_lru[l], "w_out": w_out[l],
              "ln1_g": ln1_g[l], "ln1_b": ln1_b[l], "ln2_g": ln2_g[l], "ln2_b": ln2_b[l],
              "w_mlp_in": w_mlp_in[l], "w_mlp_out": w_mlp_out[l]}
        mod_ctx = silu_ctx @ w_ada[l] + b_ada[l]
        xp, fin = trunk_layer(xp, mod_ctx, lw, zero_init)
        fin_gla.append(fin[0])
        fin_s5_re.append(fin[1])
        fin_s5_im.append(fin[2])
        fin_lru.append(fin[3])
        mod_lat = silu_c @ w_ada[l] + b_ada[l]
        cache_l = (state_gla[:, l], state_s5_re[:, l], state_s5_im[:, l], state_lru[:, l])
        xs, _ = trunk_layer(xs, mod_lat, lw, cache_l)
    sdt = x_prompt.dtype
    new_state_gla = jnp.stack(fin_gla, axis=1).astype(sdt)
    new_state_s5_re = jnp.stack(fin_s5_re, axis=1).astype(sdt)
    new_state_s5_im = jnp.stack(fin_s5_im, axis=1).astype(sdt)
    new_state_lru = jnp.stack(fin_lru, axis=1).astype(sdt)
    return (xp, xs, new_state_gla, new_state_s5_re, new_state_s5_im, new_state_lru)
```

```python
import functools
import math

import jax
import jax.numpy as jnp
from jax import lax
from jax.experimental import pallas as pl
from jax.experimental.pallas import tpu as pltpu

F32 = jnp.float32
BF16 = jnp.bfloat16
HIGHEST = lax.Precision.HIGHEST

LANES = 128
SUBLANES = 8
VMEM_LIMIT = 56 * 1024 * 1024

GRID_W = 64
GLA_HEADS = 4
GATE_RANK = 16
GLA_TAU = 16.0
GLA_CHUNK = 64
S5_GROUP = 16
S5_STATE = 64
S5_SUB = 16
LRU_BLOCK = 128
LRU_CONV = 4
LRU_C = 8.0
LN_EPS = 1e-5


def _cparams(*sem):
    return pltpu.CompilerParams(dimension_semantics=sem, vmem_limit_bytes=VMEM_LIMIT)


def _dot(a, b):
    return jnp.dot(a.astype(BF16), b.astype(BF16), preferred_element_type=F32)


def _dot_nt(a, b):
    return lax.dot_general(a.astype(BF16), b.astype(BF16), (((1,), (1,)), ((), ())),
                           preferred_element_type=F32)


def _dot_tn(a, b):
    return lax.dot_general(a.astype(BF16), b.astype(BF16), (((0,), (0,)), ((), ())),
                           preferred_element_type=F32)


def _dot_f32(a, b):
    return jnp.dot(a, b, precision=HIGHEST, preferred_element_type=F32)


def _layer_norm(z, g, b):
    mu = jnp.mean(z, axis=-1, keepdims=True)
    zc = z - mu
    var = jnp.mean(zc * zc, axis=-1, keepdims=True)
    return zc * lax.rsqrt(var + LN_EPS) * g + b


def _sigmoid(x):
    return jax.nn.sigmoid(x)


def _silu(x):
    return x * _sigmoid(x)


def _ada_kernel(cc_ref, w_ref, b_ref, o_ref):
    s = _silu(cc_ref[...])
    o_ref[0] = _dot_f32(s, w_ref[0]) + b_ref[0]


def _ada_mod(cc, w_ada, b_ada):
    L, D, D6 = w_ada.shape
    R = cc.shape[0]
    tn = D6 // 4
    return pl.pallas_call(
        _ada_kernel,
        grid=(L, D6 // tn),
        in_specs=[pl.BlockSpec((R, D), lambda l, j: (0, 0)),
                  pl.BlockSpec((1, D, tn), lambda l, j: (l, 0, j)),
                  pl.BlockSpec((1, 1, tn), lambda l, j: (l, 0, j))],
        out_specs=pl.BlockSpec((1, R, tn), lambda l, j: (l, 0, j)),
        out_shape=jax.ShapeDtypeStruct((L, R, D6), F32),
        compiler_params=_cparams("parallel", "parallel"),
        name="ada_mod",
    )(cc, w_ada, b_ada.reshape(L, 1, D6))


def _addpos_kernel(x_ref, p_ref, o_ref):
    o_ref[0] = x_ref[0] + p_ref[...]


def _add_pos(x, pos):
    B, T, D = x.shape
    tt = min(T, 512)
    return pl.pallas_call(
        _addpos_kernel,
        grid=(T // tt, B),
        in_specs=[pl.BlockSpec((1, tt, D), lambda t, b: (b, t, 0)),
                  pl.BlockSpec((tt, D), lambda t, b: (t, 0))],
        out_specs=pl.BlockSpec((1, tt, D), lambda t, b: (b, t, 0)),
        out_shape=jax.ShapeDtypeStruct((B, T, D), F32),
        compiler_params=_cparams("parallel", "parallel"),
        name="add_pos",
    )(x, pos)


def _grid_pos_table(n_tokens, dim):
    rows = n_tokens // GRID_W
    quarter = dim // 4
    omega = 1.0 / (10000.0 ** (jnp.arange(quarter, dtype=F32) / quarter))
    r = jnp.arange(rows, dtype=F32)[:, None, None] * omega
    cl = jnp.arange(GRID_W, dtype=F32)[None, :, None] * omega
    shape = (rows, GRID_W, quarter)
    emb = jnp.concatenate([jnp.broadcast_to(jnp.sin(r), shape), jnp.broadcast_to(jnp.cos(r), shape),
                           jnp.broadcast_to(jnp.sin(cl), shape), jnp.broadcast_to(jnp.cos(cl), shape)], axis=-1)
    return emb.reshape(rows * GRID_W, dim)


def _inproj_kernel(x_ref, mod_ref, w_ref, o_ref, h_ref, *, D):
    @pl.when(pl.program_id(1) == 0)
    def _():
        shift = mod_ref[0, :, 0:D]
        scale = mod_ref[0, :, D:2 * D]
        h_ref[...] = (x_ref[...] * (1.0 + scale) + shift).astype(BF16)

    o_ref[...] = jnp.dot(h_ref[...], w_ref[...], preferred_element_type=F32)


def _row_tile(N, T, per_batch_mod, want):
    tm = min(want, T if per_batch_mod else N)
    while N % tm or (per_batch_mod and T % tm):
        tm //= 2
    return tm


def _in_proj(x2, mod, w_pack, T):
    N, D = x2.shape
    NC = w_pack.shape[1]
    Bm = mod.shape[0]
    tm = _row_tile(N, T, Bm > 1, 1024)
    tn = NC // 9 if NC % (9 * LANES) == 0 else NC
    per_b = T // tm
    mod_map = (lambda i, j: (i // per_b, 0, 0)) if Bm > 1 else (lambda i, j: (0, 0, 0))
    return pl.pallas_call(
        functools.partial(_inproj_kernel, D=D),
        grid=(N // tm, NC // tn),
        in_specs=[pl.BlockSpec((tm, D), lambda i, j: (i, 0)),
                  pl.BlockSpec((1, 1, 6 * D), mod_map),
                  pl.BlockSpec((D, tn), lambda i, j: (0, j))],
        out_specs=pl.BlockSpec((tm, tn), lambda i, j: (i, j)),
        out_shape=jax.ShapeDtypeStruct((N, NC), F32),
        scratch_shapes=[pltpu.VMEM((tm, D), BF16)],
        compiler_params=_cparams("parallel", "arbitrary"),
        name="in_proj",
    )(x2, mod, w_pack)


def _gla_kernel(*refs, T, DK, DV, has_init):
    if has_init:
        (q_ref, k_ref, v_ref, glr_ref, wg_ref, bg_ref, s0_ref,
         o_ref, sfin_ref, qb_ref, keb_ref, db_ref, sf_ref, sb_ref) = refs
    else:
        (q_ref, k_ref, v_ref, glr_ref, wg_ref, bg_ref,
         o_ref, sfin_ref, qb_ref, keb_ref, db_ref, sf_ref, sb_ref) = refs
    C = GLA_CHUNK
    n = T // C
    qscale = DK ** -0.5
    row = lax.broadcasted_iota(jnp.int32, (C, C), 0)
    col = lax.broadcasted_iota(jnp.int32, (C, C), 1)
    lower = row >= col
    upper = row <= col
    ltri = lower.astype(F32)
    utri = upper.astype(F32)
    wg = wg_ref[0]
    bg = bg_ref[0]

    if has_init:
        sf_ref[...] = s0_ref[0, 0, 0].T
        sb_ref[...] = s0_ref[0, 1, 0].T
    else:
        sf_ref[...] = jnp.zeros((DV, DK), F32)
        sb_ref[...] = jnp.zeros((DV, DK), F32)

    def fwd_body(i, carry):
        r0 = pl.multiple_of(i * C, C)
        rows = pl.ds(r0, C)
        qc = q_ref[rows, :] * qscale
        kc = k_ref[rows, :]
        vc = v_ref[rows, :]
        logits = _dot(glr_ref[rows, :], wg) + bg
        la = (jnp.minimum(logits, 0.0) - jnp.log1p(jnp.exp(-jnp.abs(logits)))) * (1.0 / GLA_TAU)
        cum_f = _dot_f32(ltri, la[:, :DK])
        cum_b = _dot_f32(utri, la[:, DK:])
        last_f = cum_f[C - 1:C, :]
        last_b = cum_b[0:1, :]
        q_f = qc * jnp.exp(cum_f)
        k_f = kc * jnp.exp(-cum_f)
        ke_f = kc * jnp.exp(last_f - cum_f)
        q_b = qc * jnp.exp(cum_b)
        k_b = kc * jnp.exp(-cum_b)
        ke_b = kc * jnp.exp(last_b - cum_b)
        sc = jnp.where(lower, _dot_nt(q_f, k_f), 0.0) + jnp.where(upper, _dot_nt(q_b, k_b), 0.0)
        s_f = sf_ref[...]
        o_ref[rows, :] = _dot(sc, vc) + _dot_nt(q_f, s_f)
        sf_ref[...] = s_f * jnp.exp(last_f) + _dot_tn(vc, ke_f)
        qb_ref[rows, :] = q_b
        keb_ref[rows, :] = ke_b
        db_ref[pl.ds(i, 1), :] = jnp.exp(last_b)
        return carry

    lax.fori_loop(0, n, fwd_body, 0)

    def bwd_body(j, carry):
        i = n - 1 - j
        r0 = pl.multiple_of(i * C, C)
        rows = pl.ds(r0, C)
        s_b = sb_ref[...]
        o_ref[rows, :] = o_ref[rows, :] + _dot_nt(qb_ref[rows, :], s_b)
        sb_ref[...] = s_b * db_ref[pl.ds(i, 1), :] + _dot_tn(v_ref[rows, :], keb_ref[rows, :])
        return carry

    lax.fori_loop(0, n, bwd_body, 0)

    sfin_ref[0, 0, 0] = sf_ref[...].T
    sfin_ref[0, 1, 0] = sb_ref[...].T


def _gla_core(P, wg, bg, s0, B, T, D):
    H = GLA_HEADS
    DK = D // 2 // H
    DV = D // H
    N = B * T
    NC = P.shape[1]
    has_init = s0 is not None
    kcol = (D // 2) // DK
    vcol = D // DV
    glrcol = (NC - LANES) // LANES
    in_specs = [pl.BlockSpec((T, DK), lambda b, h: (b, h)),
                pl.BlockSpec((T, DK), lambda b, h: (b, kcol + h)),
                pl.BlockSpec((T, DV), lambda b, h: (b, vcol + h)),
                pl.BlockSpec((T, LANES), lambda b, h: (b, glrcol)),
                pl.BlockSpec((1, LANES, 2 * DK), lambda b, h: (h, 0, 0)),
                pl.BlockSpec((1, 1, 2 * DK), lambda b, h: (h, 0, 0))]
    args = [P, P, P, P, wg, bg]
    if has_init:
        in_specs.append(pl.BlockSpec((1, 2, 1, DK, DV), lambda b, h: (b, 0, h, 0, 0)))
        args.append(s0)
    o, sfin = pl.pallas_call(
        functools.partial(_gla_kernel, T=T, DK=DK, DV=DV, has_init=has_init),
        grid=(B, H),
        in_specs=in_specs,
        out_specs=[pl.BlockSpec((T, DV), lambda b, h: (b, h)),
                   pl.BlockSpec((1, 2, 1, DK, DV), lambda b, h: (b, 0, h, 0, 0))],
        out_shape=[jax.ShapeDtypeStruct((N, D), F32),
                   jax.ShapeDtypeStruct((B, 2, H, DK, DV), F32)],
        scratch_shapes=[pltpu.VMEM((T, DK), F32), pltpu.VMEM((T, DK), F32),
                        pltpu.VMEM((T // GLA_CHUNK, DK), F32),
                        pltpu.VMEM((DV, DK), F32), pltpu.VMEM((DV, DK), F32)],
        compiler_params=_cparams("parallel", "parallel"),
        name="gla_core",
    )(*args)
    return o, sfin


def _s5prep_kernel(lr_ref, li_ref, ls_ref, btr_ref, bti_ref, cr_ref, ci_ref,
                   kt_ref, wre_ref, wim_ref, cpre_ref, cpimn_ref, pre_ref, pim_ref):
    I = S5_GROUP
    lr = lr_ref[0, 0]
    li = li_ref[0, 0]
    step = jnp.exp(ls_ref[0, 0])
    mag = jnp.exp(lr * step)
    ang = li * step
    a_re = mag * jnp.cos(ang)
    a_im = mag * jnp.sin(ang)
    den = lr * lr + li * li
    nr = a_re - 1.0
    f_re = (nr * lr + a_im * li) / den
    f_im = (a_im * lr - nr * li) / den
    btr = btr_ref[0]
    bti = bti_ref[0]
    bb_re = f_re[:, None, :] * btr - f_im[:, None, :] * bti
    bb_im = f_re[:, None, :] * bti + f_im[:, None, :] * btr
    c_re = cr_ref[0]
    c_im = ci_ref[0]
    p_re = jnp.ones_like(lr)
    p_im = jnp.zeros_like(lr)
    batched_nt = (((2,), (2,)), ((0,), (0,)))
    for e in range(S5_SUB + 1):
        pr = p_re[:, None, :]
        pi = p_im[:, None, :]
        cp_re = c_re * pr - c_im * pi
        cp_im = c_re * pi + c_im * pr
        cpre_ref[0, 0, :, e * I:(e + 1) * I, :] = cp_re
        cpimn_ref[0, 0, :, e * I:(e + 1) * I, :] = -cp_im
        if e < S5_SUB:
            wre_ref[0, 0, :, e * I:(e + 1) * I, :] = pr * bb_re - pi * bb_im
            wim_ref[0, 0, :, e * I:(e + 1) * I, :] = pr * bb_im + pi * bb_re
            kt_ref[0, 0, :, e * I:(e + 1) * I, :] = (
                lax.dot_general(cp_re, bb_re, batched_nt, precision=HIGHEST, preferred_element_type=F32)
                - lax.dot_general(cp_im, bb_im, batched_nt, precision=HIGHEST, preferred_element_type=F32))
            p_re, p_im = p_re * a_re - p_im * a_im, p_re * a_im + p_im * a_re
    pre_ref[0, 0] = p_re
    pim_ref[0, 0] = p_im


def _s5_operators(lam_re, lam_im, log_step, b_re, b_im, c_re, c_im):
    L, _, G, Pn = lam_re.shape
    I = S5_GROUP
    S = S5_SUB
    ls = jnp.broadcast_to(log_step[..., None], lam_re.shape)
    btr = jnp.swapaxes(b_re, -1, -2)
    bti = jnp.swapaxes(b_im, -1, -2)
    Gb = SUBLANES
    lam_spec = pl.BlockSpec((1, 1, Gb, Pn), lambda l, d, g: (l, d, g, 0))
    gip_spec = pl.BlockSpec((1, Gb, I, Pn), lambda l, d, g: (l, g, 0, 0))

    def out_spec(rows, last):
        return pl.BlockSpec((1, 1, Gb, rows, last), lambda l, d, g: (l, d, g, 0, 0))

    kt, wre, wim, cpre, cpimn, pre, pim = pl.pallas_call(
        _s5prep_kernel,
        grid=(L, 2, G // Gb),
        in_specs=[lam_spec, lam_spec, lam_spec, gip_spec, gip_spec, gip_spec, gip_spec],
        out_specs=[out_spec(S * I, I), out_spec(S * I, Pn), out_spec(S * I, Pn),
                   out_spec((S + 1) * I, Pn), out_spec((S + 1) * I, Pn), lam_spec, lam_spec],
        out_shape=[jax.ShapeDtypeStruct((L, 2, G, S * I, I), F32),
                   jax.ShapeDtypeStruct((L, 2, G, S * I, Pn), F32),
                   jax.ShapeDtypeStruct((L, 2, G, S * I, Pn), F32),
                   jax.ShapeDtypeStruct((L, 2, G, (S + 1) * I, Pn), F32),
                   jax.ShapeDtypeStruct((L, 2, G, (S + 1) * I, Pn), F32),
                   jax.ShapeDtypeStruct((L, 2, G, Pn), F32),
                   jax.ShapeDtypeStruct((L, 2, G, Pn), F32)],
        compiler_params=_cparams("parallel", "parallel", "parallel"),
        name="s5_prep",
    )(lam_re, lam_im, ls, btr, bti, c_re, c_im)

    kt = kt.reshape(L, 2, G, S, I, I)
    s_idx = jnp.arange(S)[:, None]
    t_idx = jnp.arange(S)[None, :]

    def toeplitz(k, lag, valid):
        m = k[:, :, jnp.clip(lag, 0, S - 1)]
        m = jnp.where(valid[None, None, :, :, None, None], m, 0.0)
        return m.transpose(0, 1, 2, 5, 3, 4).reshape(L, G, S * I, S * I)

    mf = toeplitz(kt[:, 0], t_idx - s_idx, t_idx >= s_idx)
    mb = toeplitz(kt[:, 1], s_idx - t_idx, s_idx >= t_idx)

    def by_pos(w, reverse):
        w = w.reshape(L, G, S, I, Pn)
        if reverse:
            w = w[:, :, ::-1]
        return w.reshape(L, G, S * I, Pn)

    wall = jnp.concatenate([by_pos(wre[:, 0], True), by_pos(wre[:, 1], False),
                            by_pos(wim[:, 0], True), by_pos(wim[:, 1], False)], axis=-1)

    def readout(cp):
        cp = cp.reshape(L, 2, G, S + 1, I, Pn)
        f = cp[:, 0, :, 1:]
        b = cp[:, 1, :, 1:][:, :, ::-1]
        f = f.reshape(L, G, S * I, Pn).swapaxes(-1, -2)
        b = b.reshape(L, G, S * I, Pn).swapaxes(-1, -2)
        return jnp.concatenate([f, b], axis=2)

    vre = readout(cpre)
    vim = readout(cpimn)
    are = jnp.concatenate([pre[:, 0], pre[:, 1]], axis=-1)[:, :, None, :]
    aim = jnp.concatenate([pim[:, 0], pim[:, 1]], axis=-1)[:, :, None, :]
    return mf, mb, wall, vre, vim, are, aim


def _s5_kernel(*refs, R, B, has_init):
    if has_init:
        (x_ref, mf_ref, mb_ref, w_ref, vre_ref, vim_ref, are_ref, aim_ref, s0re_ref, s0im_ref,
         y_ref, fre_ref, fim_ref, zre_ref, zim_ref, hre_ref, him_ref) = refs
    else:
        (x_ref, mf_ref, mb_ref, w_ref, vre_ref, vim_ref, are_ref, aim_ref,
         y_ref, fre_ref, fim_ref, zre_ref, zim_ref, hre_ref, him_ref) = refs
    Pn = S5_STATE
    x = x_ref[0].astype(BF16)
    m = (mf_ref[0] + mb_ref[0]).astype(BF16)
    y_ref[0] = jnp.dot(x, m, preferred_element_type=F32)
    z = jnp.dot(x, w_ref[0].astype(BF16), preferred_element_type=F32)
    zre_ref[...] = z[:, :2 * Pn]
    zim_ref[...] = z[:, 2 * Pn:]
    a_re = are_ref[0]
    a_im = aim_ref[0]
    is_fwd = lax.broadcasted_iota(jnp.int32, (B, 2 * Pn), 1) < Pn
    if has_init:
        h0 = (s0re_ref[0], s0im_ref[0])
    else:
        h0 = (jnp.zeros((B, 2 * Pn), F32), jnp.zeros((B, 2 * Pn), F32))

    def fwd_step(r, h):
        hr, hi = h
        rows = pl.ds(pl.multiple_of(r * B, B), B)
        hre_ref[rows, :] = hr
        him_ref[rows, :] = hi
        return (hr * a_re - hi * a_im + zre_ref[rows, :], hr * a_im + hi * a_re + zim_ref[rows, :])

    hf = lax.fori_loop(0, R, fwd_step, h0)

    def bwd_step(j, h):
        hr, hi = h
        rows = pl.ds(pl.multiple_of((R - 1 - j) * B, B), B)
        hre_ref[rows, :] = jnp.where(is_fwd, hre_ref[rows, :], hr)
        him_ref[rows, :] = jnp.where(is_fwd, him_ref[rows, :], hi)
        return (hr * a_re - hi * a_im + zre_ref[rows, :], hr * a_im + hi * a_re + zim_ref[rows, :])

    hb = lax.fori_loop(0, R, bwd_step, h0)
    fre_ref[0] = jnp.where(is_fwd, hf[0], hb[0])
    fim_ref[0] = jnp.where(is_fwd, hf[1], hb[1])
    y_ref[0] = y_ref[0] + _dot(hre_ref[...], vre_ref[0]) + _dot(him_ref[...], vim_ref[0])


def _s5_core(xg, ops, s0re, s0im, R, B):
    G, RB, K = xg.shape
    mf, mb, wall, vre, vim, are, aim = ops
    P2 = 2 * S5_STATE
    has_init = s0re is not None
    gspec = lambda shape: pl.BlockSpec((1,) + shape, lambda g: (g, 0, 0))
    in_specs = [gspec((RB, K)), gspec((K, K)), gspec((K, K)), gspec((K, K)),
                gspec((P2, K)), gspec((P2, K)), gspec((1, P2)), gspec((1, P2))]
    args = [xg, mf, mb, wall, vre, vim, are, aim]
    if has_init:
        in_specs += [gspec((B, P2)), gspec((B, P2))]
        args += [s0re, s0im]
    return pl.pallas_call(
        functools.partial(_s5_kernel, R=R, B=B, has_init=has_init),
        grid=(G,),
        in_specs=in_specs,
        out_specs=[gspec((RB, K)), gspec((B, P2)), gspec((B, P2))],
        out_shape=[jax.ShapeDtypeStruct((G, RB, K), F32),
                   jax.ShapeDtypeStruct((G, B, P2), F32),
                   jax.ShapeDtypeStruct((G, B, P2), F32)],
        scratch_shapes=[pltpu.VMEM((RB, P2), F32) for _ in range(4)],
        compiler_params=_cparams("parallel"),
        name="s5_core",
    )(*args)


def _lru_kernel(*refs, Tt, CB, reverse, has_init):
    refs = list(refs)
    lx_ref, prev_ref, next_ref, cw_ref, cb_ref, wg_ref, bgate_ref, lam_ref = refs[:8]
    pos = 8
    h0_ref = None
    if has_init:
        h0_ref = refs[pos]
        pos += 1
    if reverse:
        hf_ref, ly_ref = refs[pos:pos + 2]
        pos += 2
    out_ref, fin_ref, a_scr, b_scr, h_scr, carry_ref = refs[pos:pos + 6]
    NBk = CB // LRU_BLOCK
    k = pl.program_id(2)
    nk = pl.num_programs(2)
    first_tile = k == 0

    @pl.when(first_tile)
    def _():
        if has_init:
            carry_ref[...] = h0_ref[...]
        else:
            carry_ref[...] = jnp.zeros((SUBLANES, CB), F32)

    tpos = (nk - 1 - k) if reverse else k
    has_prev = tpos > 0
    has_next = tpos < nk - 1
    cw = cw_ref[0]
    cbias = cb_ref[0]
    bgate = bgate_ref[0]
    lam = lam_ref[0]
    cfac = -LRU_C * (jnp.maximum(-lam, 0.0) + jnp.log1p(jnp.exp(-jnp.abs(lam))))

    for b in range(SUBLANES):
        cur = lx_ref[b]
        pv = jnp.where(has_prev, prev_ref[b], 0.0)
        nx = jnp.where(has_next, next_ref[b], 0.0)
        ext = jnp.concatenate([pv, cur, nx], axis=0)
        xc = cbias + sum(cw[j:j + 1, :] * ext[SUBLANES - 2 + j: SUBLANES - 2 + j + Tt, :]
                         for j in range(LRU_CONV))
        for c in range(NBk):
            sl = slice(c * LRU_BLOCK, (c + 1) * LRU_BLOCK)
            xcc = xc[:, sl]
            gates = _dot(xcc, wg_ref[0, c]) + jnp.concatenate(
                [bgate[:, sl], bgate[:, CB + c * LRU_BLOCK: CB + (c + 1) * LRU_BLOCK]], axis=1)
            log_a = cfac[:, sl] * _sigmoid(gates[:, :LRU_BLOCK])
            a = jnp.exp(log_a)
            bt = jnp.sqrt(-jnp.tanh(log_a) * (1.0 + a * a)) * (_sigmoid(gates[:, LRU_BLOCK:]) * xcc)
            a_scr[c, pl.ds(b, Tt, stride=SUBLANES), :] = a
            b_scr[c, pl.ds(b, Tt, stride=SUBLANES), :] = bt

    def step(s, h):
        t = (Tt - 1 - s) if reverse else s
        rows = pl.ds(pl.multiple_of(t * SUBLANES, SUBLANES), SUBLANES)
        new = []
        for c in range(NBk):
            hc = a_scr[c, rows, :] * h[c] + b_scr[c, rows, :]
            h_scr[c, rows, :] = hc
            new.append(hc)
        return tuple(new)

    h_init = tuple(carry_ref[:, c * LRU_BLOCK:(c + 1) * LRU_BLOCK] for c in range(NBk))
    h_last = lax.fori_loop(0, Tt, step, h_init)
    for c in range(NBk):
        carry_ref[:, c * LRU_BLOCK:(c + 1) * LRU_BLOCK] = h_last[c]
        fin_ref[:, c * LRU_BLOCK:(c + 1) * LRU_BLOCK] = h_last[c]

    for b in range(SUBLANES):
        hb = jnp.concatenate([h_scr[c, pl.ds(b, Tt, stride=SUBLANES), :] for c in range(NBk)], axis=1)
        if reverse:
            out_ref[b] = (hf_ref[b] + hb) * jax.nn.gelu(ly_ref[b])
        else:
            out_ref[b] = hb


def _lru_sweep(P3, lw, h0, hf, reverse, B, T, D):
    W = 3 * D // 2
    CB = 2 * LRU_BLOCK
    Tt = min(T, 256)
    nk = T // Tt
    nb8 = Tt // SUBLANES
    lxcol = (D // 2 + D // 2 + D + D + D) // CB
    lycol = lxcol + W // CB
    d = 1 if reverse else 0
    has_init = h0 is not None

    def tmap(k):
        return (nk - 1 - k) if reverse else k

    in_specs = [
        pl.BlockSpec((SUBLANES, Tt, CB), lambda g, j, k: (g, tmap(k), lxcol + j)),
        pl.BlockSpec((SUBLANES, SUBLANES, CB),
                     lambda g, j, k: (g, jnp.maximum(tmap(k) * nb8 - 1, 0), lxcol + j)),
        pl.BlockSpec((SUBLANES, SUBLANES, CB),
                     lambda g, j, k: (g, jnp.minimum((tmap(k) + 1) * nb8, T // SUBLANES - 1), lxcol + j)),
        pl.BlockSpec((1, LRU_CONV, CB), lambda g, j, k: (j, 0, 0)),
        pl.BlockSpec((1, 1, CB), lambda g, j, k: (j, 0, 0)),
        pl.BlockSpec((1, CB // LRU_BLOCK, LRU_BLOCK, 2 * LRU_BLOCK), lambda g, j, k: (j, 0, 0, 0)),
        pl.BlockSpec((1, 1, 2 * CB), lambda g, j, k: (j, 0, 0)),
        pl.BlockSpec((1, 1, CB), lambda g, j, k: (j, 0, 0)),
    ]
    args = [P3, P3, P3, lw["conv_w"], lw["conv_b"], lw["wg"][d], lw["bg"][d], lw["lam"][d]]
    if has_init:
        in_specs.append(pl.BlockSpec((SUBLANES, CB), lambda g, j, k: (g, j)))
        args.append(h0[:, d])
    if reverse:
        in_specs.append(pl.BlockSpec((SUBLANES, Tt, CB), lambda g, j, k: (g, tmap(k), j)))
        in_specs.append(pl.BlockSpec((SUBLANES, Tt, CB), lambda g, j, k: (g, tmap(k), lycol + j)))
        args += [hf, P3]
    out, fin = pl.pallas_call(
        functools.partial(_lru_kernel, Tt=Tt, CB=CB, reverse=reverse, has_init=has_init),
        grid=(B // SUBLANES, W // CB, nk),
        in_specs=in_specs,
        out_specs=[pl.BlockSpec((SUBLANES, Tt, CB), lambda g, j, k: (g, tmap(k), j)),
                   pl.BlockSpec((SUBLANES, CB), lambda g, j, k: (g, j))],
        out_shape=[jax.ShapeDtypeStruct((B, T, W), F32), jax.ShapeDtypeStruct((B, W), F32)],
        scratch_shapes=[pltpu.VMEM((CB // LRU_BLOCK, Tt * SUBLANES, LRU_BLOCK), F32) for _ in range(3)]
        + [pltpu.VMEM((SUBLANES, CB), F32)],
        compiler_params=_cparams("parallel", "parallel", "arbitrary"),
        name="lru_bwd" if reverse else "lru_fwd",
    )(*args)
    return out, fin


def _merge_kernel(x_ref, mod_ref, o_ref, r_ref, ys_ref, u_ref, yl_ref, gg_ref, gs_ref, gl_ref,
                  gn_ref, sd_ref, wglu_ref, wbg_ref, wbs_ref, wbl_ref, wo_ref, lg_ref, lb_ref,
                  out_ref, *, D, alpha):
    DV = D // GLA_HEADS
    gn = gn_ref[...]
    parts = []
    for h in range(GLA_HEADS):
        sl = slice(h * DV, (h + 1) * DV)
        o = o_ref[:, sl]
        mu = jnp.mean(o, axis=-1, keepdims=True)
        oc = o - mu
        var = jnp.mean(oc * oc, axis=-1, keepdims=True)
        parts.append(oc * lax.rsqrt(var + LN_EPS) * gn[:, sl] * _silu(r_ref[:, sl]))
    y_gla = jnp.concatenate(parts, axis=1)
    ys = jax.nn.gelu(ys_ref[...] + sd_ref[...] * u_ref[...])
    y_s5 = ys * _sigmoid(_dot(ys, wglu_ref[...]))
    merged = (_sigmoid(gg_ref[...]) * _dot(y_gla, wbg_ref[...])
              + _sigmoid(gs_ref[...]) * _dot(y_s5, wbs_ref[...])
              + _sigmoid(gl_ref[...]) * _dot(yl_ref[...], wbl_ref[...]))
    mix = _dot(merged, wo_ref[...])
    gate1 = mod_ref[0, :, 2 * D:3 * D]
    out_ref[...] = _layer_norm(alpha * x_ref[...] + gate1 * mix, lg_ref[...], lb_ref[...])


def _merge(x2, mod, o_gla, P, ys, ylru, lw, T, alpha):
    N, D = x2.shape
    W = 3 * D // 2
    Bm = mod.shape[0]
    tm = _row_tile(N, T, Bm > 1, 256)
    per_b = T // tm
    mod_map = (lambda i: (i // per_b, 0, 0)) if Bm > 1 else (lambda i: (0, 0, 0))
    rcol = 2
    ucol = 3
    gcol = (4 * D + 2 * W) // D
    row =lambda width, cb=0: pl.BlockSpec((tm, width), lambda i: (i, cb))
    full = lambda a: pl.BlockSpec(a.shape, lambda i: (0,) * a.ndim)
    weights = [lw["gnorm"], lw["s5_d"], lw["w_glu"], lw["w_br_gla"], lw["w_br_s5"], lw["w_br_lru"],
               lw["w_out"], lw["ln1_g"], lw["ln1_b"]]
    return pl.pallas_call(
        functools.partial(_merge_kernel, D=D, alpha=alpha),
        grid=(N // tm,),
        in_specs=[row(D), pl.BlockSpec((1, 1, 6 * D), mod_map), row(D), row(D, rcol), row(D), row(D, ucol),
                  row(W), row(D, gcol), row(D, gcol + 1), row(D, gcol + 2)] + [full(w) for w in weights],
        out_specs=row(D),
        out_shape=jax.ShapeDtypeStruct((N, D), F32),
        compiler_params=_cparams("parallel"),
        name="merge",
    )(x2, mod, o_gla, P, ys, P, ylru, P, P, P, *weights)


def _mlp_kernel(x_ref, mod_ref, w1_ref, w2_ref, lg_ref, lb_ref, out_ref, h_ref, acc_ref, *, D, alpha):
    j = pl.program_id(1)

    @pl.when(j == 0)
    def _():
        shift = mod_ref[0, :, 3 * D:4 * D]
        scale = mod_ref[0, :, 4 * D:5 * D]
        h_ref[...] = (x_ref[...] * (1.0 + scale) + shift).astype(BF16)
        acc_ref[...] = jnp.zeros_like(acc_ref)

    hid = jnp.dot(h_ref[...], w1_ref[...], preferred_element_type=F32)
    hid = jnp.square(jnp.maximum(hid, 0.0))
    acc_ref[...] += _dot(hid, w2_ref[...])

    @pl.when(j == pl.num_programs(1) - 1)
    def _():
        gate2 = mod_ref[0, :, 5 * D:6 * D]
        out_ref[...] = _layer_norm(alpha * x_ref[...] + gate2 * acc_ref[...], lg_ref[...], lb_ref[...])


def _mlp(x2, mod, lw, T, alpha):
    N, D = x2.shape
    HID = lw["w_mlp_in"].shape[1]
    Bm = mod.shape[0]
    tm = _row_tile(N, T, Bm > 1, 512)
    th = min(HID, 1024)
    per_b = T // tm
    mod_map = (lambda i, j: (i // per_b, 0, 0)) if Bm > 1 else (lambda i, j: (0, 0, 0))
    return pl.pallas_call(
        functools.partial(_mlp_kernel, D=D, alpha=alpha),
        grid=(N // tm, HID // th),
        in_specs=[pl.BlockSpec((tm, D), lambda i, j: (i, 0)),
                  pl.BlockSpec((1, 1, 6 * D), mod_map),
                  pl.BlockSpec((D, th), lambda i, j: (0, j)),
                  pl.BlockSpec((th, D), lambda i, j: (j, 0)),
                  pl.BlockSpec((1, D), lambda i, j: (0, 0)),
                  pl.BlockSpec((1, D), lambda i, j: (0, 0))],
        out_specs=pl.BlockSpec((tm, D), lambda i, j: (i, 0)),
        out_shape=jax.ShapeDtypeStruct((N, D), F32),
        scratch_shapes=[pltpu.VMEM((tm, D), BF16), pltpu.VMEM((tm, D), F32)],
        compiler_params=_cparams("parallel", "arbitrary"),
        name="mlp",
    )(x2, mod, lw["w_mlp_in"], lw["w_mlp_out"], lw["ln2_g"], lw["ln2_b"])


def _pack_layer_weights(l, D, w_in, gla_w_gate, gla_b_gate, gla_norm_g, s5_d, s5_w_glu,
                        lru_conv_w, lru_conv_b, lru_w_a, lru_b_a, lru_w_i, lru_b_i, lru_lam,
                        w_br_gla, w_br_s5, w_br_lru, w_out, ln1_g, ln1_b, ln2_g, ln2_b,
                        w_mlp_in, w_mlp_out):
    H = GLA_HEADS
    KEY = D // 2
    DK = KEY // H
    W = 3 * D // 2
    CB = 2 * LRU_BLOCK
    NB = W // LRU_BLOCK
    widths = (KEY, KEY, D, D, 2 * GATE_RANK, D, W, W, 3 * D)
    offs = [0]
    for wd in widths:
        offs.append(offs[-1] + wd)
    wl = w_in[l]
    piece = lambda i: wl[:, offs[i]:offs[i + 1]]
    pad = jnp.zeros((D, LANES - 2 * GATE_RANK), wl.dtype)
    w_pack = jnp.concatenate([piece(0), piece(1), piece(2), piece(3), piece(5), piece(6), piece(7),
                              piece(8), piece(4), pad], axis=1).astype(BF16)
    wgate = gla_w_gate[l]
    wg = jnp.zeros((H, LANES, 2 * DK), F32)
    wg = wg.at[:, 0:GATE_RANK, 0:DK].set(wgate[0].reshape(GATE_RANK, H, DK).transpose(1, 0, 2))
    wg = wg.at[:, GATE_RANK:2 * GATE_RANK, DK:].set(wgate[1].reshape(GATE_RANK, H, DK).transpose(1, 0, 2))
    bgate = gla_b_gate[l].reshape(2, H, DK).transpose(1, 0, 2).reshape(H, 1, 2 * DK)
    lru_wg = jnp.concatenate([lru_w_a[l], lru_w_i[l]], axis=-1)
    lru_wg = lru_wg.reshape(2, W // CB, CB // LRU_BLOCK, LRU_BLOCK, 2 * LRU_BLOCK).astype(BF16)
    lru_bg = jnp.concatenate([lru_b_a[l].reshape(2, W // CB, 1, CB), lru_b_i[l].reshape(2, W // CB, 1, CB)],
                             axis=-1)
    return {
        "w_pack": w_pack, "gla_wg": wg.astype(BF16), "gla_bg": bgate,
        "lru": {"conv_w": lru_conv_w[l].reshape(LRU_CONV, W // CB, CB).transpose(1, 0, 2),
                "conv_b": lru_conv_b[l].reshape(W // CB, 1, CB),
                "wg": lru_wg, "bg": lru_bg, "lam": lru_lam[l].reshape(2, W // CB, 1, CB)},
        "gnorm": gla_norm_g[l].reshape(1, D), "s5_d": s5_d[l].reshape(1, D),
        "w_glu": s5_w_glu[l].astype(BF16), "w_br_gla": w_br_gla[l].astype(BF16),
        "w_br_s5": w_br_s5[l].astype(BF16), "w_br_lru": w_br_lru[l].astype(BF16),
        "w_out": w_out[l].astype(BF16),
        "ln1_g": ln1_g[l].reshape(1, D), "ln1_b": ln1_b[l].reshape(1, D),
        "ln2_g": ln2_g[l].reshape(1, D), "ln2_b": ln2_b[l].reshape(1, D),
        "w_mlp_in": w_mlp_in[l].astype(BF16), "w_mlp_out": w_mlp_out[l].astype(BF16),
    }


def _trunk_layer(x2, mod, lw, s5ops, init, B, T, D, alpha):
    N = B * T
    G = D // S5_GROUP
    S = S5_SUB
    R = T // S
    Pn = S5_STATE
    P = _in_proj(x2, mod, lw["w_pack"], T)
    gla0 = s5re0 = s5im0 = lru0 = None
    if init is not None:
        gla0, s5re0, s5im0, lru0 = init
        s5re0 = s5re0.transpose(2, 0, 1, 3).reshape(G, B, 2 * Pn)
        s5im0 = s5im0.transpose(2, 0, 1, 3).reshape(G, B, 2 * Pn)
    o_gla, gla_fin = _gla_core(P, lw["gla_wg"], lw["gla_bg"], gla0, B, T, D)
    ucol = 3 * D
    u = P[:, ucol:ucol + D]
    xg = u.reshape(B, R, S, G, S5_GROUP).transpose(3, 1, 0, 2, 4).reshape(G, R * B, S * S5_GROUP)
    yg, fre, fim = _s5_core(xg, s5ops, s5re0, s5im0, R, B)
    ys = yg.reshape(G, R, B, S, S5_GROUP).transpose(2, 1, 3, 0, 4).reshape(N, D)
    s5re_fin = fre.reshape(G, B, 2, Pn).transpose(1, 2, 0, 3)
    s5im_fin = fim.reshape(G, B, 2, Pn).transpose(1, 2, 0, 3)
    P3 = P.reshape(B, T, P.shape[1])
    hf, lf = _lru_sweep(P3, lw["lru"], lru0, None, False, B, T, D)
    ylru, lb = _lru_sweep(P3, lw["lru"], lru0, hf, True, B, T, D)
    lru_fin = jnp.stack([lf, lb], axis=1)
    x1 = _merge(x2, mod, o_gla, P, ys, ylru.reshape(N, -1), lw, T, alpha)
    x3 = _mlp(x1, mod, lw, T, alpha)
    return x3, (gla_fin, s5re_fin, s5im_fin, lru_fin)


def kernel(x_prompt, x_sample, state_gla, state_s5_re, state_s5_im, state_lru, c, c_ctx, w_ada, b_ada, w_in, gla_w_gate, gla_b_gate, gla_norm_g, s5_lam_re, s5_lam_im, s5_log_step, s5_b_re, s5_b_im, s5_c_re, s5_c_im, s5_d, s5_w_glu, lru_conv_w, lru_conv_b, lru_w_a, lru_b_a, lru_w_i, lru_b_i, lru_lam, w_br_gla, w_br_s5, w_br_lru, w_out, ln1_g, ln1_b, ln2_g, ln2_b, w_mlp_in, w_mlp_out):
    Bp, Tp, D = x_prompt.shape
    Bs, Ts, _ = x_sample.shape
    L = w_in.shape[0]
    alpha = (2.0 * L) ** 0.25

    n_rows = -(-(Bs + 1) // SUBLANES) * SUBLANES
    cc = jnp.concatenate([c, c_ctx[None], jnp.zeros((n_rows - Bs - 1, D), F32)], axis=0)
    mod = _ada_mod(cc, w_ada, b_ada)
    s5ops_all = _s5_operators(s5_lam_re, s5_lam_im, s5_log_step, s5_b_re, s5_b_im, s5_c_re, s5_c_im)

    xp = x_prompt.reshape(Bp * Tp, D)
    xs = _add_pos(x_sample, _grid_pos_table(Ts, D)).reshape(Bs * Ts, D)
    fins = []
    for l in range(L):
        lw = _pack_layer_weights(l, D, w_in, gla_w_gate, gla_b_gate, gla_norm_g, s5_d, s5_w_glu,
                                 lru_conv_w, lru_conv_b, lru_w_a, lru_b_a, lru_w_i, lru_b_i, lru_lam,
                                 w_br_gla, w_br_s5, w_br_lru, w_out, ln1_g, ln1_b, ln2_g, ln2_b,
                                 w_mlp_in, w_mlp_out)
        s5ops = tuple(a[l] for a in s5ops_all)
        mod_ctx = mod[l, Bs:Bs + 1].reshape(1, 1, 6 * D)
        mod_lat = mod[l, :Bs].reshape(Bs, 1, 6 * D)
        xp, fin = _trunk_layer(xp, mod_ctx, lw, s5ops, None, Bp, Tp, D, alpha)
        fins.append(fin)
        cache = (state_gla[:, l], state_s5_re[:, l], state_s5_im[:, l], state_lru[:, l])
        xs, _ = _trunk_layer(xs, mod_lat, lw, s5ops, cache, Bs, Ts, D, alpha)
    sdt = x_prompt.dtype
    new_states = tuple(jnp.stack([f[i] for f in fins], axis=1).astype(sdt) for i in range(4))
    return (xp.reshape(Bp, Tp, D), xs.reshape(Bs, Ts, D)) + new_states
```

```python
import functools
import math

import jax
import jax.numpy as jnp
from jax import lax
from jax.experimental import pallas as pl
from jax.experimental.pallas import tpu as pltpu

F32 = jnp.float32
BF16 = jnp.bfloat16
HIGHEST = lax.Precision.HIGHEST

LANES = 128
SUBLANES = 8
VMEM_LIMIT = 56 * 1024 * 1024

GRID_W = 64
GLA_HEADS = 4
GATE_RANK = 16
GLA_TAU = 16.0
GLA_CHUNK = 64
GLA_BLOCK = 256
S5_GROUP = 16
S5_STATE = 64
S5_SUB = 16
LRU_BLOCK = 128
LRU_CONV = 4
LRU_C = 8.0
LN_EPS = 1e-5
F32_TINY = 1.1754944e-38
SCAN_UNROLL = 8


def _cparams(*sem):
    return pltpu.CompilerParams(dimension_semantics=sem, vmem_limit_bytes=VMEM_LIMIT)


def _dot(a, b):
    return jnp.dot(a.astype(BF16), b.astype(BF16), preferred_element_type=F32)


def _dot_nt(a, b):
    return lax.dot_general(a.astype(BF16), b.astype(BF16), (((1,), (1,)), ((), ())),
                           preferred_element_type=F32)


def _dot_tn(a, b):
    return lax.dot_general(a.astype(BF16), b.astype(BF16), (((0,), (0,)), ((), ())),
                           preferred_element_type=F32)


def _dot_f32(a, b):
    return jnp.dot(a, b, precision=HIGHEST, preferred_element_type=F32)


def _layer_norm(z, g, b):
    mu = jnp.mean(z, axis=-1, keepdims=True)
    zc = z - mu
    var = jnp.mean(zc * zc, axis=-1, keepdims=True)
    return zc * lax.rsqrt(var + LN_EPS) * g + b


def _sigmoid(x):
    return 0.5 * jnp.tanh(0.5 * x) + 0.5


def _silu(x):
    return x * _sigmoid(x)


def _ada_kernel(cc_ref, w_ref, b_ref, o_ref):
    s = _silu(cc_ref[...])
    o_ref[0] = _dot_f32(s, w_ref[0]) + b_ref[0]


def _ada_mod(cc, w_ada, b_ada):
    L, D, D6 = w_ada.shape
    R = cc.shape[0]
    tn = D6 // 4
    return pl.pallas_call(
        _ada_kernel,
        grid=(L, D6 // tn),
        in_specs=[pl.BlockSpec((R, D), lambda l, j: (0, 0)),
                  pl.BlockSpec((1, D, tn), lambda l, j: (l, 0, j)),
                  pl.BlockSpec((1, 1, tn), lambda l, j: (l, 0, j))],
        out_specs=pl.BlockSpec((1, R, tn), lambda l, j: (l, 0, j)),
        out_shape=jax.ShapeDtypeStruct((L, R, D6), F32),
        compiler_params=_cparams("parallel", "parallel"),
        name="ada_mod",
    )(cc, w_ada, b_ada.reshape(L, 1, D6))


def _addpos_kernel(x_ref, p_ref, o_ref):
    o_ref[0] = x_ref[0] + p_ref[...]


def _add_pos(x, pos):
    B, T, D = x.shape
    tt = min(T, 512)
    return pl.pallas_call(
        _addpos_kernel,
        grid=(T // tt, B),
        in_specs=[pl.BlockSpec((1, tt, D), lambda t, b: (b, t, 0)),
                  pl.BlockSpec((tt, D), lambda t, b: (t, 0))],
        out_specs=pl.BlockSpec((1, tt, D), lambda t, b: (b, t, 0)),
        out_shape=jax.ShapeDtypeStruct((B, T, D), F32),
        compiler_params=_cparams("parallel", "parallel"),
        name="add_pos",
    )(x, pos)


def _grid_pos_table(n_tokens, dim):
    rows = n_tokens // GRID_W
    quarter = dim // 4
    omega = 1.0 / (10000.0 ** (jnp.arange(quarter, dtype=F32) / quarter))
    r = jnp.arange(rows, dtype=F32)[:, None, None] * omega
    cl = jnp.arange(GRID_W, dtype=F32)[None, :, None] * omega
    shape = (rows, GRID_W, quarter)
    emb = jnp.concatenate([jnp.broadcast_to(jnp.sin(r), shape), jnp.broadcast_to(jnp.cos(r), shape),
                           jnp.broadcast_to(jnp.sin(cl), shape), jnp.broadcast_to(jnp.cos(cl), shape)], axis=-1)
    return emb.reshape(rows * GRID_W, dim)


def _inproj_kernel(x_ref, mod_ref, w_ref, wglr_ref, o_ref, glr_ref, h_ref, *, D):
    @pl.when(pl.program_id(1) == 0)
    def _():
        shift = mod_ref[0, :, 0:D]
        scale = mod_ref[0, :, D:2 * D]
        h = (x_ref[...] * (1.0 + scale) + shift).astype(BF16)
        h_ref[...] = h
        glr_ref[...] = jnp.dot(h, wglr_ref[...], preferred_element_type=F32)

    o_ref[...] = jnp.dot(h_ref[...], w_ref[...], preferred_element_type=F32)


def _row_tile(N, T, per_batch_mod, want):
    tm = min(want, T if per_batch_mod else N)
    while N % tm or (per_batch_mod and T % tm):
        tm //= 2
    return tm


def _in_proj(x2, mod, w_pack, w_glr, T):
    N, D = x2.shape
    NC = w_pack.shape[1]
    Bm = mod.shape[0]
    tm = _row_tile(N, T, Bm > 1, 1024)
    tn = D
    per_b = T // tm
    mod_map = (lambda i, j: (i // per_b, 0, 0)) if Bm > 1 else (lambda i, j: (0, 0, 0))
    return pl.pallas_call(
        functools.partial(_inproj_kernel, D=D),
        grid=(N // tm, NC // tn),
        in_specs=[pl.BlockSpec((tm, D), lambda i, j: (i, 0)),
                  pl.BlockSpec((1, 1, 6 * D), mod_map),
                  pl.BlockSpec((D, tn), lambda i, j: (0, j)),
                  pl.BlockSpec((D, LANES), lambda i, j: (0, 0))],
        out_specs=[pl.BlockSpec((tm, tn), lambda i, j: (i, j)),
                   pl.BlockSpec((tm, LANES), lambda i, j: (i, 0))],
        out_shape=[jax.ShapeDtypeStruct((N, NC), F32), jax.ShapeDtypeStruct((N, LANES), F32)],
        scratch_shapes=[pltpu.VMEM((tm, D), BF16)],
        compiler_params=_cparams("parallel", "arbitrary"),
        name="in_proj",
    )(x2, mod, w_pack, w_glr)


def _gla_kernel(*refs, T, DK, DV, has_init):
    if has_init:
        (q_ref, k_ref, v_ref, glr_ref, wg_ref, bg_ref, s0_ref,
         o_ref, sfin_ref, qb_ref, keb_ref, db_ref, sf_ref, sb_ref) = refs
    else:
        (q_ref, k_ref, v_ref, glr_ref, wg_ref, bg_ref,
         o_ref, sfin_ref, qb_ref, keb_ref, db_ref, sf_ref, sb_ref) = refs
    C = GLA_CHUNK
    BLK = GLA_BLOCK
    NS = BLK // C
    nblk = T // BLK
    qscale = DK ** -0.5
    shift = C.bit_length() - 1
    row = lax.broadcasted_iota(jnp.int32, (BLK, BLK), 0)
    col = lax.broadcasted_iota(jnp.int32, (BLK, BLK), 1)
    same = lax.shift_right_logical(row, shift) == lax.shift_right_logical(col, shift)
    lower = jnp.logical_and(same, row >= col)
    upper = jnp.logical_and(same, row <= col)
    tri = jnp.where(lower, 1.0, 0.0).astype(BF16)
    chunk_of_row = lax.shift_right_logical(lax.broadcasted_iota(jnp.int32, (BLK, DK), 0), shift)
    wg = wg_ref[0]
    bg = bg_ref[0]

    def chunk_cols(ke):
        return jnp.concatenate([jnp.where(chunk_of_row == c, ke, 0.0) for c in range(NS)], axis=1).astype(BF16)

    def per_chunk_last(x, r):
        return jnp.concatenate([jnp.broadcast_to(x[c * C + r:c * C + r + 1, :], (C, x.shape[1]))
                                for c in range(NS)], axis=0)

    if has_init:
        sf_ref[...] = s0_ref[0, 0, 0].T
        sb_ref[...] = s0_ref[0, 1, 0].T
    else:
        sf_ref[...] = jnp.zeros((DV, DK), F32)
        sb_ref[...] = jnp.zeros((DV, DK), F32)

    def fwd_body(i, carry):
        r0 = pl.multiple_of(i * BLK, BLK)
        rows = pl.ds(r0, BLK)
        qc = q_ref[rows, :] * qscale
        kc = k_ref[rows, :]
        vc = v_ref[rows, :]
        logits = _dot(glr_ref[rows, :], wg) + bg
        la = (jnp.minimum(logits, 0.0) - jnp.log1p(jnp.exp(-jnp.abs(logits)))) * (1.0 / GLA_TAU)
        p1 = la.astype(BF16)
        r1 = la - p1.astype(F32)
        p2 = r1.astype(BF16)
        p3 = (r1 - p2.astype(F32)).astype(BF16)
        pre = (jnp.dot(tri, p1, preferred_element_type=F32) + jnp.dot(tri, p2, preferred_element_type=F32)
               + jnp.dot(tri, p3, preferred_element_type=F32))
        tot = per_chunk_last(pre, C - 1)
        cum_f = pre[:, :DK]
        last_f = tot[:, :DK]
        cum_b = tot[:, DK:] - pre[:, DK:] + la[:, DK:]
        last_b = tot[:, DK:]
        q_f = qc * jnp.exp(cum_f)
        k_f = kc * jnp.exp(-cum_f)
        ke_f = kc * jnp.exp(last_f - cum_f)
        q_b = qc * jnp.exp(cum_b)
        k_b = kc * jnp.exp(-cum_b)
        ke_b = kc * jnp.exp(last_b - cum_b)
        sc = jnp.where(lower, _dot_nt(q_f, k_f), 0.0) + jnp.where(upper, _dot_nt(q_b, k_b), 0.0)
        o_blk = _dot(sc, vc)
        u_f = jnp.dot(vc.T.astype(BF16), chunk_cols(ke_f), preferred_element_type=F32)
        d_f = jnp.exp(last_f)
        s = sf_ref[...]
        for c in range(NS):
            rc = slice(c * C, (c + 1) * C)
            o_ref[pl.ds(r0 + c * C, C), :] = o_blk[rc] + _dot_nt(q_f[rc], s)
            s = s * d_f[c * C:c * C + 1, :] + u_f[:, c * DK:(c + 1) * DK]
        sf_ref[...] = s
        qb_ref[rows, :] = q_b
        keb_ref[rows, :] = ke_b
        d_b = jnp.exp(last_b)
        db_ref[pl.ds(i, 1), :] = jnp.concatenate([d_b[c * C:c * C + 1, :] for c in range(NS)], axis=1)
        return carry

    lax.fori_loop(0, nblk, fwd_body, 0)

    def bwd_body(j, carry):
        i = nblk - 1 - j
        r0 = pl.multiple_of(i * BLK, BLK)
        rows = pl.ds(r0, BLK)
        q_b = qb_ref[rows, :]
        u_b = jnp.dot(v_ref[rows, :].T.astype(BF16), chunk_cols(keb_ref[rows, :]), preferred_element_type=F32)
        d_b = db_ref[pl.ds(i, 1), :]
        s = sb_ref[...]
        for c in reversed(range(NS)):
            rc = pl.ds(r0 + c * C, C)
            o_ref[rc, :] = o_ref[rc, :] + _dot_nt(q_b[c * C:(c + 1) * C], s)
            s = s * d_b[:, c * DK:(c + 1) * DK] + u_b[:, c * DK:(c + 1) * DK]
        sb_ref[...] = s
        return carry

    lax.fori_loop(0, nblk, bwd_body, 0)

    sfin_ref[0, 0, 0] = sf_ref[...].T
    sfin_ref[0, 1, 0] = sb_ref[...].T


def _gla_core(P, glr, wg, bg, s0, B, T, D):
    H = GLA_HEADS
    DK = D // 2 // H
    DV = D // H
    N = B * T
    has_init = s0 is not None
    kcol = (D // 2) // DK
    vcol = D // DV
    in_specs = [pl.BlockSpec((T, DK), lambda b, h: (b, h)),
                pl.BlockSpec((T, DK), lambda b, h: (b, kcol + h)),
                pl.BlockSpec((T, DV), lambda b, h: (b, vcol + h)),
                pl.BlockSpec((T, LANES), lambda b, h: (b, 0)),
                pl.BlockSpec((1, LANES, 2 * DK), lambda b, h: (h, 0, 0)),
                pl.BlockSpec((1, 1, 2 * DK), lambda b, h: (h, 0, 0))]
    args = [P, P, P, glr, wg, bg]
    if has_init:
        in_specs.append(pl.BlockSpec((1, 2, 1, DK, DV), lambda b, h: (b, 0, h, 0, 0)))
        args.append(s0)
    o, sfin = pl.pallas_call(
        functools.partial(_gla_kernel, T=T, DK=DK, DV=DV, has_init=has_init),
        grid=(B, H),
        in_specs=in_specs,
        out_specs=[pl.BlockSpec((T, DV), lambda b, h: (b, h)),
                   pl.BlockSpec((1, 2, 1, DK, DV), lambda b, h: (b, 0, h, 0, 0))],
        out_shape=[jax.ShapeDtypeStruct((N, D), F32),
                   jax.ShapeDtypeStruct((B, 2, H, DK, DV), F32)],
        scratch_shapes=[pltpu.VMEM((T, DK), F32), pltpu.VMEM((T, DK), F32),
                        pltpu.VMEM((T // GLA_BLOCK, (GLA_BLOCK // GLA_CHUNK) * DK), F32),
                        pltpu.VMEM((DV, DK), F32), pltpu.VMEM((DV, DK), F32)],
        compiler_params=_cparams("parallel", "parallel"),
        name="gla_core",
    )(*args)
    return o, sfin


def _s5prep_kernel(lr_ref, li_ref, ls_ref, btr_ref, bti_ref, cr_ref, ci_ref,
                   kt_ref, wre_ref, wim_ref, cpre_ref, cpimn_ref, pre_ref, pim_ref):
    I = S5_GROUP
    lr = lr_ref[0, 0]
    li = li_ref[0, 0]
    step = jnp.exp(ls_ref[0, 0])
    mag = jnp.exp(lr * step)
    ang = li * step
    a_re = mag * jnp.cos(ang)
    a_im = mag * jnp.sin(ang)
    den = lr * lr + li * li
    nr = a_re - 1.0
    f_re = (nr * lr + a_im * li) / den
    f_im = (a_im * lr - nr * li) / den
    btr = btr_ref[0]
    bti = bti_ref[0]
    bb_re = f_re[:, None, :] * btr - f_im[:, None, :] * bti
    bb_im = f_re[:, None, :] * bti + f_im[:, None, :] * btr
    c_re = cr_ref[0]
    c_im = ci_ref[0]
    p_re = jnp.ones_like(lr)
    p_im = jnp.zeros_like(lr)
    batched_nt = (((2,), (2,)), ((0,), (0,)))
    for e in range(S5_SUB + 1):
        pr = p_re[:, None, :]
        pi = p_im[:, None, :]
        cp_re = c_re * pr - c_im * pi
        cp_im = c_re * pi + c_im * pr
        cpre_ref[0, 0, :, e * I:(e + 1) * I, :] = cp_re
        cpimn_ref[0, 0, :, e * I:(e + 1) * I, :] = -cp_im
        if e < S5_SUB:
            wre_ref[0, 0, :, e * I:(e + 1) * I, :] = pr * bb_re - pi * bb_im
            wim_ref[0, 0, :, e * I:(e + 1) * I, :] = pr * bb_im + pi * bb_re
            kt_ref[0, 0, :, e * I:(e + 1) * I, :] = (
                lax.dot_general(cp_re, bb_re, batched_nt, precision=HIGHEST, preferred_element_type=F32)
                - lax.dot_general(cp_im, bb_im, batched_nt, precision=HIGHEST, preferred_element_type=F32))
            p_re, p_im = p_re * a_re - p_im * a_im, p_re * a_im + p_im * a_re
    pre_ref[0, 0] = p_re
    pim_ref[0, 0] = p_im


def _s5_operators(lam_re, lam_im, log_step, b_re, b_im, c_re, c_im):
    L, _, G, Pn = lam_re.shape
    I = S5_GROUP
    S = S5_SUB
    ls = jnp.broadcast_to(log_step[..., None], lam_re.shape)
    btr = jnp.swapaxes(b_re, -1, -2)
    bti = jnp.swapaxes(b_im, -1, -2)
    Gb = SUBLANES
    lam_spec = pl.BlockSpec((1, 1, Gb, Pn), lambda l, d, g: (l, d, g, 0))
    gip_spec = pl.BlockSpec((1, Gb, I, Pn), lambda l, d, g: (l, g, 0, 0))

    def out_spec(rows, last):
        return pl.BlockSpec((1, 1, Gb, rows, last), lambda l, d, g: (l, d, g, 0, 0))

    kt, wre, wim, cpre, cpimn, pre, pim = pl.pallas_call(
        _s5prep_kernel,
        grid=(L, 2, G // Gb),
        in_specs=[lam_spec, lam_spec, lam_spec, gip_spec, gip_spec, gip_spec, gip_spec],
        out_specs=[out_spec(S * I, I), out_spec(S * I, Pn), out_spec(S * I, Pn),
                   out_spec((S + 1) * I, Pn), out_spec((S + 1) * I, Pn), lam_spec, lam_spec],
        out_shape=[jax.ShapeDtypeStruct((L, 2, G, S * I, I), F32),
                   jax.ShapeDtypeStruct((L, 2, G, S * I, Pn), F32),
                   jax.ShapeDtypeStruct((L, 2, G, S * I, Pn), F32),
                   jax.ShapeDtypeStruct((L, 2, G, (S + 1) * I, Pn), F32),
                   jax.ShapeDtypeStruct((L, 2, G, (S + 1) * I, Pn), F32),
                   jax.ShapeDtypeStruct((L, 2, G, Pn), F32),
                   jax.ShapeDtypeStruct((L, 2, G, Pn), F32)],
        compiler_params=_cparams("parallel", "parallel", "parallel"),
        name="s5_prep",
    )(lam_re, lam_im, ls, btr, bti, c_re, c_im)

    kt = kt.reshape(L, 2, G, S, I, I)
    s_idx = jnp.arange(S)[:, None]
    t_idx = jnp.arange(S)[None, :]

    def toeplitz(k, lag, valid):
        m = k[:, :, jnp.clip(lag, 0, S - 1)]
        m = jnp.where(valid[None, None, :, :, None, None], m, 0.0)
        return m.transpose(0, 1, 2, 5, 3, 4).reshape(L, G, S * I, S * I)

    mf = toeplitz(kt[:, 0], t_idx - s_idx, t_idx >= s_idx)
    mb = toeplitz(kt[:, 1], s_idx - t_idx, s_idx >= t_idx)

    def by_pos(w, reverse):
        w = w.reshape(L, G, S, I, Pn)
        if reverse:
            w = w[:, :, ::-1]
        return w.reshape(L, G, S * I, Pn)

    wall = jnp.concatenate([by_pos(wre[:, 0], True), by_pos(wre[:, 1], False),
                            by_pos(wim[:, 0], True), by_pos(wim[:, 1], False)], axis=-1)

    def readout(cp):
        cp = cp.reshape(L, 2, G, S + 1, I, Pn)
        f = cp[:, 0, :, 1:]
        b = cp[:, 1, :, 1:][:, :, ::-1]
        f = f.reshape(L, G, S * I, Pn).swapaxes(-1, -2)
        b = b.reshape(L, G, S * I, Pn).swapaxes(-1, -2)
        return jnp.concatenate([f, b], axis=2)

    vre = readout(cpre)
    vim = readout(cpimn)
    are = jnp.concatenate([pre[:, 0], pre[:, 1]], axis=-1)[:, :, None, :]
    aim = jnp.concatenate([pim[:, 0], pim[:, 1]], axis=-1)[:, :, None, :]
    t = lambda a: jnp.swapaxes(a, -1, -2)
    return {"row": (mf, mb, wall, vre, vim, are, aim),
            "col": (t(mf), t(mb), t(wall), t(vre), t(vim), are, aim)}


def _s5_kernel(*refs, R, B, has_init):
    if has_init:
        (x_ref, mf_ref, mb_ref, w_ref, vre_ref, vim_ref, are_ref, aim_ref, s0re_ref, s0im_ref,
         y_ref, fre_ref, fim_ref, zre_ref, zim_ref, hre_ref, him_ref) = refs
    else:
        (x_ref, mf_ref, mb_ref, w_ref, vre_ref, vim_ref, are_ref, aim_ref,
         y_ref, fre_ref, fim_ref, zre_ref, zim_ref, hre_ref, him_ref) = refs
    Pn = S5_STATE
    x = x_ref[0].astype(BF16)
    m = (mf_ref[0] + mb_ref[0]).astype(BF16)
    y_ref[0] = jnp.dot(x, m, preferred_element_type=F32)
    z = jnp.dot(x, w_ref[0].astype(BF16), preferred_element_type=F32)
    zre_ref[...] = z[:, :2 * Pn]
    zim_ref[...] = z[:, 2 * Pn:]
    a_re = are_ref[0]
    a_im = aim_ref[0]
    is_fwd = lax.broadcasted_iota(jnp.int32, (B, 2 * Pn), 1) < Pn
    if has_init:
        h0 = (s0re_ref[0], s0im_ref[0])
    else:
        h0 = (jnp.zeros((B, 2 * Pn), F32), jnp.zeros((B, 2 * Pn), F32))

    def fwd_step(r, h):
        hr, hi = h
        rows = pl.ds(pl.multiple_of(r * B, B), B)
        hre_ref[rows, :] = hr
        him_ref[rows, :] = hi
        return (hr * a_re - hi * a_im + zre_ref[rows, :], hr * a_im + hi * a_re + zim_ref[rows, :])

    hf = lax.fori_loop(0, R, fwd_step, h0, unroll=SCAN_UNROLL)

    def bwd_step(j, h):
        hr, hi = h
        rows = pl.ds(pl.multiple_of((R - 1 - j) * B, B), B)
        hre_ref[rows, :] = jnp.where(is_fwd, hre_ref[rows, :], hr)
        him_ref[rows, :] = jnp.where(is_fwd, him_ref[rows, :], hi)
        return (hr * a_re - hi * a_im + zre_ref[rows, :], hr * a_im + hi * a_re + zim_ref[rows, :])

    hb = lax.fori_loop(0, R, bwd_step, h0, unroll=SCAN_UNROLL)
    fre_ref[0] = jnp.where(is_fwd, hf[0], hb[0])
    fim_ref[0] = jnp.where(is_fwd, hf[1], hb[1])
    y_ref[0] = y_ref[0] + _dot(hre_ref[...], vre_ref[0]) + _dot(him_ref[...], vim_ref[0])


def _s5_core(xg, ops, s0re, s0im, R, B):
    G, RB, K = xg.shape
    mf, mb, wall, vre, vim, are, aim = ops
    P2 = 2 * S5_STATE
    has_init = s0re is not None
    gspec = lambda shape: pl.BlockSpec((1,) + shape, lambda g: (g, 0, 0))
    in_specs = [gspec((RB, K)), gspec((K, K)), gspec((K, K)), gspec((K, K)),
                gspec((P2, K)), gspec((P2, K)), gspec((1, P2)), gspec((1, P2))]
    args = [xg, mf, mb, wall, vre, vim, are, aim]
    if has_init:
        in_specs += [gspec((B, P2)), gspec((B, P2))]
        args += [s0re, s0im]
    return pl.pallas_call(
        functools.partial(_s5_kernel, R=R, B=B, has_init=has_init),
        grid=(G,),
        in_specs=in_specs,
        out_specs=[gspec((RB, K)), gspec((B, P2)), gspec((B, P2))],
        out_shape=[jax.ShapeDtypeStruct((G, RB, K), F32),
                   jax.ShapeDtypeStruct((G, B, P2), F32),
                   jax.ShapeDtypeStruct((G, B, P2), F32)],
        scratch_shapes=[pltpu.VMEM((RB, P2), F32) for _ in range(4)],
        compiler_params=_cparams("parallel"),
        name="s5_core",
    )(*args)


def _s5projt_kernel(x_ref, mod_ref, wt_ref, o_ref, *, D):
    shift = mod_ref[0, :, 0:D]
    scale = mod_ref[0, :, D:2 * D]
    h = (x_ref[...] * (1.0 + scale) + shift).astype(BF16)
    ut = lax.dot_general(wt_ref[...], h, (((1,), (1,)), ((), ())), preferred_element_type=F32)
    o_ref[...] = ut.reshape(o_ref.shape).astype(o_ref.dtype)


def _s5_proj_t(x2, mod, w_ut, B, T):
    N, D = x2.shape
    S = S5_SUB
    R = T // S
    G = D // S5_GROUP
    xv = x2.reshape(B * R, S * D)
    return pl.pallas_call(
        functools.partial(_s5projt_kernel, D=D),
        grid=(B, S),
        in_specs=[pl.BlockSpec((R, D), lambda b, s: (b, s)),
                  pl.BlockSpec((1, 1, 6 * D), lambda b, s: (b, 0, 0)),
                  pl.BlockSpec((D, D), lambda b, s: (0, 0))],
        out_specs=pl.BlockSpec((G, S5_GROUP, R), lambda b, s: (0, s, b)),
        out_shape=jax.ShapeDtypeStruct((G, S * S5_GROUP, B * R), BF16),
        compiler_params=_cparams("parallel", "parallel"),
        name="s5_proj_t",
    )(xv, mod, w_ut)


def _s5t_kernel(xt_ref, mft_ref, mbt_ref, wt_ref, vret_ref, vimt_ref, are_ref, aim_ref, s0re_ref, s0im_ref,
                yt_ref, fre_ref, fim_ref, zt_ref, zre_ref, zim_ref, hre_ref, him_ref, *, R, B):
    Pn = S5_STATE
    xt = xt_ref[0]
    mt = (mft_ref[0] + mbt_ref[0]).astype(BF16)
    yt_ref[0] = jnp.dot(mt, xt, preferred_element_type=F32)
    zt_ref[...] = jnp.dot(wt_ref[0].astype(BF16), xt, preferred_element_type=F32)
    for b in range(B):
        zb = zt_ref[:, b * R:(b + 1) * R].T
        zre_ref[pl.ds(b, R, stride=B), :] = zb[:, :2 * Pn]
        zim_ref[pl.ds(b, R, stride=B), :] = zb[:, 2 * Pn:]
    a_re = are_ref[0]
    a_im = aim_ref[0]
    is_fwd = lax.broadcasted_iota(jnp.int32, (B, 2 * Pn), 1) < Pn
    h0 = (s0re_ref[0], s0im_ref[0])

    def fwd_step(r, h):
        hr, hi = h
        rows = pl.ds(pl.multiple_of(r * B, B), B)
        hre_ref[rows, :] = hr
        him_ref[rows, :] = hi
        return (hr * a_re - hi * a_im + zre_ref[rows, :], hr * a_im + hi * a_re + zim_ref[rows, :])

    hf = lax.fori_loop(0, R, fwd_step, h0, unroll=SCAN_UNROLL)

    def bwd_step(j, h):
        hr, hi = h
        rows = pl.ds(pl.multiple_of((R - 1 - j) * B, B), B)
        hre_ref[rows, :] = jnp.where(is_fwd, hre_ref[rows, :], hr)
        him_ref[rows, :] = jnp.where(is_fwd, him_ref[rows, :], hi)
        return (hr * a_re - hi * a_im + zre_ref[rows, :], hr * a_im + hi * a_re + zim_ref[rows, :])

    hb = lax.fori_loop(0, R, bwd_step, h0, unroll=SCAN_UNROLL)
    fre_ref[0] = jnp.where(is_fwd, hf[0], hb[0])
    fim_ref[0] = jnp.where(is_fwd, hf[1], hb[1])
    vret = vret_ref[0]
    vimt = vimt_ref[0]
    for b in range(B):
        hre_b = hre_ref[pl.ds(b, R, stride=B), :]
        him_b = him_ref[pl.ds(b, R, stride=B), :]
        cols = slice(b * R, (b + 1) * R)
        yt_ref[0, :, cols] = yt_ref[0, :, cols] + _dot_nt(vret, hre_b) + _dot_nt(vimt, him_b)


def _s5_core_t(xt, ops, s0re, s0im, R, B):
    G, K, BR = xt.shape
    mft, mbt, wt, vret, vimt, are, aim = ops
    P2 = 2 * S5_STATE
    gspec = lambda shape: pl.BlockSpec((1,) + shape, lambda g: (g, 0, 0))
    return pl.pallas_call(
        functools.partial(_s5t_kernel, R=R, B=B),
        grid=(G,),
        in_specs=[gspec((K, BR)), gspec((K, K)), gspec((K, K)), gspec((K, K)),
                  gspec((K, P2)), gspec((K, P2)), gspec((1, P2)), gspec((1, P2)),
                  gspec((B, P2)), gspec((B, P2))],
        out_specs=[gspec((K, BR)), gspec((B, P2)), gspec((B, P2))],
        out_shape=[jax.ShapeDtypeStruct((G, K, BR), F32),
                   jax.ShapeDtypeStruct((G, B, P2), F32),
                   jax.ShapeDtypeStruct((G, B, P2), F32)],
        scratch_shapes=[pltpu.VMEM((K, BR), F32)] + [pltpu.VMEM((BR, P2), F32) for _ in range(4)],
        compiler_params=_cparams("parallel"),
        name="s5_core_t",
    )(xt, mft, mbt, wt, vret, vimt, are, aim, s0re, s0im)


def _lru_kernel(*refs, Tt, CB, reverse, has_init):
    refs = list(refs)
    lx_ref, prev_ref, next_ref, cw_ref, cb_ref, wg_ref, bgate_ref, lam_ref = refs[:8]
    pos = 8
    h0_ref = None
    if has_init:
        h0_ref = refs[pos]
        pos += 1
    if reverse:
        hf_ref, ly_ref = refs[pos:pos + 2]
        pos += 2
    out_ref, fin_ref, x_scr, a_scr, b_scr, h_scr, carry_ref = refs[pos:pos + 7]
    NBk = CB // LRU_BLOCK
    k = pl.program_id(2)
    nk = pl.num_programs(2)
    first_tile = k == 0

    @pl.when(first_tile)
    def _():
        if has_init:
            carry_ref[...] = h0_ref[...]
        else:
            carry_ref[...] = jnp.zeros((SUBLANES, CB), F32)

    tpos = (nk - 1 - k) if reverse else k
    has_prev = tpos > 0
    has_next = tpos < nk - 1
    cw = cw_ref[0]
    cbias = cb_ref[0]
    bgate = bgate_ref[0]
    lam = lam_ref[0]
    cfac = -LRU_C * (jnp.maximum(-lam, 0.0) + jnp.log1p(jnp.exp(-jnp.abs(lam))))

    left = LRU_CONV // 2
    for b in range(SUBLANES):
        cur = lx_ref[b]
        pv = jnp.where(has_prev, prev_ref[b], 0.0)
        nx = jnp.where(has_next, next_ref[b], 0.0)
        for c in range(NBk):
            sl = slice(c * LRU_BLOCK, (c + 1) * LRU_BLOCK)
            x_scr[c, pl.ds(left * SUBLANES + b, Tt, stride=SUBLANES), :] = cur[:, sl]
            for q in range(left):
                x_scr[c, pl.ds(q * SUBLANES + b, 1), :] = pv[SUBLANES - left + q:SUBLANES - left + q + 1, sl]
            for q in range(LRU_CONV - 1 - left):
                x_scr[c, pl.ds((Tt + left + q) * SUBLANES + b, 1), :] = nx[q:q + 1, sl]

    RC = min(Tt * SUBLANES, 512)
    for c in range(NBk):
        sl = slice(c * LRU_BLOCK, (c + 1) * LRU_BLOCK)
        bias_c = jnp.concatenate([bgate[:, sl], bgate[:, CB + c * LRU_BLOCK: CB + (c + 1) * LRU_BLOCK]], axis=1)
        for r0 in range(0, Tt * SUBLANES, RC):
            xc = cbias[:, sl] + sum(cw[j:j + 1, sl] * x_scr[c, pl.ds(r0 + j * SUBLANES, RC), :]
                                    for j in range(LRU_CONV))
            gates = _dot(xc, wg_ref[0, c]) + bias_c
            log_a = cfac[:, sl] * _sigmoid(gates[:, :LRU_BLOCK])
            a = jnp.exp(log_a)
            om = -jnp.tanh(log_a) * (1.0 + a * a)
            root = om * lax.rsqrt(jnp.maximum(om, F32_TINY))
            bt = root * (_sigmoid(gates[:, LRU_BLOCK:]) * xc)
            a_scr[c, pl.ds(r0, RC), :] = a
            b_scr[c, pl.ds(r0, RC), :] = bt

    def step(s, h):
        t = (Tt - 1 - s) if reverse else s
        rows = pl.ds(pl.multiple_of(t * SUBLANES, SUBLANES), SUBLANES)
        new = []
        for c in range(NBk):
            hc = a_scr[c, rows, :] * h[c] + b_scr[c, rows, :]
            h_scr[c, rows, :] = hc
            new.append(hc)
        return tuple(new)

    h_init = tuple(carry_ref[:, c * LRU_BLOCK:(c + 1) * LRU_BLOCK] for c in range(NBk))
    h_last = lax.fori_loop(0, Tt, step, h_init, unroll=SCAN_UNROLL)
    for c in range(NBk):
        carry_ref[:, c * LRU_BLOCK:(c + 1) * LRU_BLOCK] = h_last[c]
        fin_ref[:, c * LRU_BLOCK:(c + 1) * LRU_BLOCK] = h_last[c]

    for b in range(SUBLANES):
        hb = jnp.concatenate([h_scr[c, pl.ds(b, Tt, stride=SUBLANES), :] for c in range(NBk)], axis=1)
        if reverse:
            out_ref[b] = (hf_ref[b] + hb) * jax.nn.gelu(ly_ref[b])
        else:
            out_ref[b] = hb


def _lru_sweep(P3, lw, h0, hf, reverse, B, T, D):
    W = 3 * D // 2
    CB = 2 * LRU_BLOCK
    Tt = min(T, 256)
    nk = T // Tt
    nb8 = Tt // SUBLANES
    lxcol = (D // 2 + D // 2 + D + D + D) // CB
    lycol = lxcol + W // CB
    d = 1 if reverse else 0
    has_init = h0 is not None

    def tmap(k):
        return (nk - 1 - k) if reverse else k

    in_specs = [
        pl.BlockSpec((SUBLANES, Tt, CB), lambda g, j, k: (g, tmap(k), lxcol + j)),
        pl.BlockSpec((SUBLANES, SUBLANES, CB),
                     lambda g, j, k: (g, jnp.maximum(tmap(k) * nb8 - 1, 0), lxcol + j)),
        pl.BlockSpec((SUBLANES, SUBLANES, CB),
                     lambda g, j, k: (g, jnp.minimum((tmap(k) + 1) * nb8, T // SUBLANES - 1), lxcol + j)),
        pl.BlockSpec((1, LRU_CONV, CB), lambda g, j, k: (j, 0, 0)),
        pl.BlockSpec((1, 1, CB), lambda g, j, k: (j, 0, 0)),
        pl.BlockSpec((1, CB // LRU_BLOCK, LRU_BLOCK, 2 * LRU_BLOCK), lambda g, j, k: (j, 0, 0, 0)),
        pl.BlockSpec((1, 1, 2 * CB), lambda g, j, k: (j, 0, 0)),
        pl.BlockSpec((1, 1, CB), lambda g, j, k: (j, 0, 0)),
    ]
    args = [P3, P3, P3, lw["conv_w"], lw["conv_b"], lw["wg"][d], lw["bg"][d], lw["lam"][d]]
    if has_init:
        in_specs.append(pl.BlockSpec((SUBLANES, CB), lambda g, j, k: (g, j)))
        args.append(h0[:, d])
    if reverse:
        in_specs.append(pl.BlockSpec((SUBLANES, Tt, CB), lambda g, j, k: (g, tmap(k), j)))
        in_specs.append(pl.BlockSpec((SUBLANES, Tt, CB), lambda g, j, k: (g, tmap(k), lycol + j)))
        args += [hf, P3]
    out, fin = pl.pallas_call(
        functools.partial(_lru_kernel, Tt=Tt, CB=CB, reverse=reverse, has_init=has_init),
        grid=(B // SUBLANES, W // CB, nk),
        in_specs=in_specs,
        out_specs=[pl.BlockSpec((SUBLANES, Tt, CB), lambda g, j, k: (g, tmap(k), j)),
                   pl.BlockSpec((SUBLANES, CB), lambda g, j, k: (g, j))],
        out_shape=[jax.ShapeDtypeStruct((B, T, W), F32), jax.ShapeDtypeStruct((B, W), F32)],
        scratch_shapes=[pltpu.VMEM((CB // LRU_BLOCK, (Tt + LRU_CONV) * SUBLANES, LRU_BLOCK), F32)]
        + [pltpu.VMEM((CB // LRU_BLOCK, Tt * SUBLANES, LRU_BLOCK), F32) for _ in range(3)]
        + [pltpu.VMEM((SUBLANES, CB), F32)],
        compiler_params=_cparams("parallel", "parallel", "arbitrary"),
        name="lru_bwd" if reverse else "lru_fwd",
    )(*args)
    return out, fin


def _merge_kernel(x_ref, mod_ref, o_ref, r_ref, ys_ref, u_ref, yl_ref, gg_ref, gs_ref, gl_ref,
                  gn_ref, sd_ref, wglu_ref, wbg_ref, wbs_ref, wbl_ref, wo_ref, lg_ref, lb_ref,
                  out_ref, *, D, alpha, ys_on_lanes):
    DV = D // GLA_HEADS
    if ys_on_lanes:
        ys_raw = ys_ref[...].reshape(D, ys_ref.shape[-1]).T
    else:
        ys_raw = ys_ref[...]
    gn = gn_ref[...]
    parts = []
    for h in range(GLA_HEADS):
        sl = slice(h * DV, (h + 1) * DV)
        o = o_ref[:, sl]
        mu = jnp.mean(o, axis=-1, keepdims=True)
        oc = o - mu
        var = jnp.mean(oc * oc, axis=-1, keepdims=True)
        parts.append(oc * lax.rsqrt(var + LN_EPS) * gn[:, sl] * _silu(r_ref[:, sl]))
    y_gla = jnp.concatenate(parts, axis=1)
    ys = jax.nn.gelu(ys_raw + sd_ref[...] * u_ref[...])
    y_s5 = ys * _sigmoid(_dot(ys, wglu_ref[...]))
    merged = (_sigmoid(gg_ref[...]) * _dot(y_gla, wbg_ref[...])
              + _sigmoid(gs_ref[...]) * _dot(y_s5, wbs_ref[...])
              + _sigmoid(gl_ref[...]) * _dot(yl_ref[...], wbl_ref[...]))
    mix = _dot(merged, wo_ref[...])
    gate1 = mod_ref[0, :, 2 * D:3 * D]
    out_ref[...] = _layer_norm(alpha * x_ref[...] + gate1 * mix, lg_ref[...], lb_ref[...])


def _merge(x2, mod, o_gla, P, ys, ylru, lw, T, alpha, ys_on_lanes):
    N, D = x2.shape
    W = 3 * D // 2
    NCB = P.shape[1] // D
    Bm = mod.shape[0]
    rcol, ucol, gcol = 2, 3, 7
    S = S5_SUB if ys_on_lanes else 1
    seq = T // S
    rows_total = N // S
    tm = _row_tile(rows_total, seq, Bm > 1, 256)
    per_b = seq // tm
    mod_map = (lambda i, s: (i // per_b, 0, 0)) if Bm > 1 else (lambda i, s: (0, 0, 0))
    row = lambda width, nblk=1, cb=0: pl.BlockSpec((tm, width), lambda i, s: (i, s * nblk + cb))
    full = lambda a: pl.BlockSpec(a.shape, lambda i, s: (0,) * a.ndim)
    view = lambda a: a.reshape(rows_total, S * a.shape[1])
    if ys_on_lanes:
        ys_spec = pl.BlockSpec((D // S5_GROUP, S5_GROUP, tm), lambda i, s: (0, s, i))
    else:
        ys_spec = row(D)
        ys = view(ys)
    weights = [lw["gnorm"], lw["s5_d"], lw["w_glu"], lw["w_br_gla"], lw["w_br_s5"], lw["w_br_lru"],
               lw["w_out"], lw["ln1_g"], lw["ln1_b"]]
    Pv = view(P)
    out = pl.pallas_call(
        functools.partial(_merge_kernel, D=D, alpha=alpha, ys_on_lanes=ys_on_lanes),
        grid=(rows_total // tm, S),
        in_specs=[row(D), pl.BlockSpec((1, 1, 6 * D), mod_map), row(D), row(D, NCB, rcol), ys_spec,
                  row(D, NCB, ucol), row(W), row(D, NCB, gcol), row(D, NCB, gcol + 1), row(D, NCB, gcol + 2)]
        + [full(w) for w in weights],
        out_specs=row(D),
        out_shape=jax.ShapeDtypeStruct((rows_total, S * D), F32),
        compiler_params=_cparams("parallel", "parallel"),
        name="merge",
    )(view(x2), mod, view(o_gla), Pv, ys, Pv, view(ylru), Pv, Pv, Pv, *weights)
    return out.reshape(N, D)


def _mlp_kernel(x_ref, mod_ref, w1_ref, w2_ref, lg_ref, lb_ref, out_ref, h_ref, acc_ref, *, D, alpha):
    j = pl.program_id(1)

    @pl.when(j == 0)
    def _():
        shift = mod_ref[0, :, 3 * D:4 * D]
        scale = mod_ref[0, :, 4 * D:5 * D]
        h_ref[...] = (x_ref[...] * (1.0 + scale) + shift).astype(BF16)
        acc_ref[...] = jnp.zeros_like(acc_ref)

    hid = jnp.dot(h_ref[...], w1_ref[...], preferred_element_type=F32)
    hid = jnp.square(jnp.maximum(hid, 0.0))
    acc_ref[...] += _dot(hid, w2_ref[...])

    @pl.when(j == pl.num_programs(1) - 1)
    def _():
        gate2 = mod_ref[0, :, 5 * D:6 * D]
        out_ref[...] = _layer_norm(alpha * x_ref[...] + gate2 * acc_ref[...], lg_ref[...], lb_ref[...])


def _mlp(x2, mod, lw, T, alpha):
    N, D = x2.shape
    HID = lw["w_mlp_in"].shape[1]
    Bm = mod.shape[0]
    tm = _row_tile(N, T, Bm > 1, 512)
    th = min(HID, 1024)
    per_b = T // tm
    mod_map = (lambda i, j: (i // per_b, 0, 0)) if Bm > 1 else (lambda i, j: (0, 0, 0))
    return pl.pallas_call(
        functools.partial(_mlp_kernel, D=D, alpha=alpha),
        grid=(N // tm, HID // th),
        in_specs=[pl.BlockSpec((tm, D), lambda i, j: (i, 0)),
                  pl.BlockSpec((1, 1, 6 * D), mod_map),
                  pl.BlockSpec((D, th), lambda i, j: (0, j)),
                  pl.BlockSpec((th, D), lambda i, j: (j, 0)),
                  pl.BlockSpec((1, D), lambda i, j: (0, 0)),
                  pl.BlockSpec((1, D), lambda i, j: (0, 0))],
        out_specs=pl.BlockSpec((tm, D), lambda i, j: (i, 0)),
        out_shape=jax.ShapeDtypeStruct((N, D), F32),
        scratch_shapes=[pltpu.VMEM((tm, D), BF16), pltpu.VMEM((tm, D), F32)],
        compiler_params=_cparams("parallel", "arbitrary"),
        name="mlp",
    )(x2, mod, lw["w_mlp_in"], lw["w_mlp_out"], lw["ln2_g"], lw["ln2_b"])


def _pack_layer_weights(l, D, w_in, gla_w_gate, gla_b_gate, gla_norm_g, s5_d, s5_w_glu,
                        lru_conv_w, lru_conv_b, lru_w_a, lru_b_a, lru_w_i, lru_b_i, lru_lam,
                        w_br_gla, w_br_s5, w_br_lru, w_out, ln1_g, ln1_b, ln2_g, ln2_b,
                        w_mlp_in, w_mlp_out):
    H = GLA_HEADS
    KEY = D // 2
    DK = KEY // H
    W = 3 * D // 2
    CB = 2 * LRU_BLOCK
    NB = W // LRU_BLOCK
    widths = (KEY, KEY, D, D, 2 * GATE_RANK, D, W, W, 3 * D)
    offs = [0]
    for wd in widths:
        offs.append(offs[-1] + wd)
    wl = w_in[l]
    piece = lambda i: wl[:, offs[i]:offs[i + 1]]
    w_pack = jnp.concatenate([piece(0), piece(1), piece(2), piece(3), piece(5), piece(6), piece(7),
                              piece(8)], axis=1).astype(BF16)
    w_glr = jnp.concatenate([piece(4), jnp.zeros((D, LANES - 2 * GATE_RANK), wl.dtype)], axis=1).astype(BF16)
    w_ut = piece(5).T.astype(BF16)
    wgate = gla_w_gate[l]
    wg = jnp.zeros((H, LANES, 2 * DK), F32)
    wg = wg.at[:, 0:GATE_RANK, 0:DK].set(wgate[0].reshape(GATE_RANK, H, DK).transpose(1, 0, 2))
    wg = wg.at[:, GATE_RANK:2 * GATE_RANK, DK:].set(wgate[1].reshape(GATE_RANK, H, DK).transpose(1, 0, 2))
    bgate = gla_b_gate[l].reshape(2, H, DK).transpose(1, 0, 2).reshape(H, 1, 2 * DK)
    lru_wg = jnp.concatenate([lru_w_a[l], lru_w_i[l]], axis=-1)
    lru_wg = lru_wg.reshape(2, W // CB, CB // LRU_BLOCK, LRU_BLOCK, 2 * LRU_BLOCK).astype(BF16)
    lru_bg = jnp.concatenate([lru_b_a[l].reshape(2, W // CB, 1, CB), lru_b_i[l].reshape(2, W // CB, 1, CB)],
                             axis=-1)
    return {
        "w_pack": w_pack, "w_glr": w_glr, "w_ut": w_ut, "gla_wg": wg.astype(BF16), "gla_bg": bgate,
        "lru": {"conv_w": lru_conv_w[l].reshape(LRU_CONV, W // CB, CB).transpose(1, 0, 2),
                "conv_b": lru_conv_b[l].reshape(W // CB, 1, CB),
                "wg": lru_wg, "bg": lru_bg, "lam": lru_lam[l].reshape(2, W // CB, 1, CB)},
        "gnorm": gla_norm_g[l].reshape(1, D), "s5_d": s5_d[l].reshape(1, D),
        "w_glu": s5_w_glu[l].astype(BF16), "w_br_gla": w_br_gla[l].astype(BF16),
        "w_br_s5": w_br_s5[l].astype(BF16), "w_br_lru": w_br_lru[l].astype(BF16),
        "w_out": w_out[l].astype(BF16),
        "ln1_g": ln1_g[l].reshape(1, D), "ln1_b": ln1_b[l].reshape(1, D),
        "ln2_g": ln2_g[l].reshape(1, D), "ln2_b": ln2_b[l].reshape(1, D),
        "w_mlp_in": w_mlp_in[l].astype(BF16), "w_mlp_out": w_mlp_out[l].astype(BF16),
    }


def _trunk_layer(x2, mod, lw, s5ops, init, B, T, D, alpha):
    N = B * T
    G = D // S5_GROUP
    S = S5_SUB
    R = T // S
    Pn = S5_STATE
    P, glr = _in_proj(x2, mod, lw["w_pack"], lw["w_glr"], T)
    gla0 = s5re0 = s5im0 = lru0 = None
    if init is not None:
        gla0, s5re0, s5im0, lru0 = init
        s5re0 = s5re0.transpose(2, 0, 1, 3).reshape(G, B, 2 * Pn)
        s5im0 = s5im0.transpose(2, 0, 1, 3).reshape(G, B, 2 * Pn)
    o_gla, gla_fin = _gla_core(P, glr, lw["gla_wg"], lw["gla_bg"], gla0, B, T, D)
    on_lanes = B == SUBLANES and R % LANES == 0 and mod.shape[0] == B
    if on_lanes:
        if s5re0 is None:
            s5re0 = jnp.zeros((G, B, 2 * Pn), F32)
            s5im0 = jnp.zeros((G, B, 2 * Pn), F32)
        xt = _s5_proj_t(x2, mod, lw["w_ut"], B, T)
        ys, fre, fim = _s5_core_t(xt, s5ops["col"], s5re0, s5im0, R, B)
    else:
        ucol = 3 * D
        u = P[:, ucol:ucol + D]
        xg = u.reshape(B, R, S, G, S5_GROUP).transpose(3, 1, 0, 2, 4).reshape(G, R * B, S * S5_GROUP)
        yg, fre, fim = _s5_core(xg, s5ops["row"], s5re0, s5im0, R, B)
        ys = yg.reshape(G, R, B, S, S5_GROUP).transpose(2, 1, 3, 0, 4).reshape(N, D)
    s5re_fin = fre.reshape(G, B, 2, Pn).transpose(1, 2, 0, 3)
    s5im_fin = fim.reshape(G, B, 2, Pn).transpose(1, 2, 0, 3)
    P3 = P.reshape(B, T, P.shape[1])
    hf, lf = _lru_sweep(P3, lw["lru"], lru0, None, False, B, T, D)
    ylru, lb = _lru_sweep(P3, lw["lru"], lru0, hf, True, B, T, D)
    lru_fin = jnp.stack([lf, lb], axis=1)
    x1 = _merge(x2, mod, o_gla, P, ys, ylru.reshape(N, -1), lw, T, alpha, on_lanes)
    x3 = _mlp(x1, mod, lw, T, alpha)
    return x3, (gla_fin, s5re_fin, s5im_fin, lru_fin)


def kernel(x_prompt, x_sample, state_gla, state_s5_re, state_s5_im, state_lru, c, c_ctx, w_ada, b_ada, w_in, gla_w_gate, gla_b_gate, gla_norm_g, s5_lam_re, s5_lam_im, s5_log_step, s5_b_re, s5_b_im, s5_c_re, s5_c_im, s5_d, s5_w_glu, lru_conv_w, lru_conv_b, lru_w_a, lru_b_a, lru_w_i, lru_b_i, lru_lam, w_br_gla, w_br_s5, w_br_lru, w_out, ln1_g, ln1_b, ln2_g, ln2_b, w_mlp_in, w_mlp_out):
    Bp, Tp, D = x_prompt.shape
    Bs, Ts, _ = x_sample.shape
    L = w_in.shape[0]
    alpha = (2.0 * L) ** 0.25

    n_rows = -(-(Bs + 1) // SUBLANES) * SUBLANES
    cc = jnp.concatenate([c, c_ctx[None], jnp.zeros((n_rows - Bs - 1, D), F32)], axis=0)
    mod = _ada_mod(cc, w_ada, b_ada)
    s5ops_all = _s5_operators(s5_lam_re, s5_lam_im, s5_log_step, s5_b_re, s5_b_im, s5_c_re, s5_c_im)

    xp = x_prompt.reshape(Bp * Tp, D)
    xs = _add_pos(x_sample, _grid_pos_table(Ts, D)).reshape(Bs * Ts, D)
    fins = []
    for l in range(L):
        lw = _pack_layer_weights(l, D, w_in, gla_w_gate, gla_b_gate, gla_norm_g, s5_d, s5_w_glu,
                                 lru_conv_w, lru_conv_b, lru_w_a, lru_b_a, lru_w_i, lru_b_i, lru_lam,
                                 w_br_gla, w_br_s5, w_br_lru, w_out, ln1_g, ln1_b, ln2_g, ln2_b,
                                 w_mlp_in, w_mlp_out)
        s5ops = {k: tuple(a[l] for a in v) for k, v in s5ops_all.items()}
        mod_ctx = mod[l, Bs:Bs + 1].reshape(1, 1, 6 * D)
        mod_lat = mod[l, :Bs].reshape(Bs, 1, 6 * D)
        xp, fin = _trunk_layer(xp, mod_ctx, lw, s5ops, None, Bp, Tp, D, alpha)
        fins.append(fin)
        cache = (state_gla[:, l], state_s5_re[:, l], state_s5_im[:, l], state_lru[:, l])
        xs, _ = _trunk_layer(xs, mod_lat, lw, s5ops, cache, Bs, Ts, D, alpha)
    sdt = x_prompt.dtype
    new_states = tuple(jnp.stack([f[i] for f in fins], axis=1).astype(sdt) for i in range(4))
    return (xp.reshape(Bp, Tp, D), xs.reshape(Bs, Ts, D)) + new_states
```

```python
import functools
import math

import jax
import jax.numpy as jnp
from jax import lax
from jax.experimental import pallas as pl
from jax.experimental.pallas import tpu as pltpu

F32 = jnp.float32
BF16 = jnp.bfloat16
HIGHEST = lax.Precision.HIGHEST

LANES = 128
SUBLANES = 8
VMEM_LIMIT = 56 * 1024 * 1024

GRID_W = 64
GLA_HEADS = 4
GATE_RANK = 16
GLA_TAU = 16.0
GLA_CHUNK = 64
GLA_BLOCK = 256
S5_GROUP = 16
S5_STATE = 64
S5_SUB = 16
LRU_BLOCK = 128
LRU_CONV = 4
LRU_C = 8.0
LN_EPS = 1e-5
F32_TINY = 1.1754944e-38
SCAN_UNROLL = 8


def _cparams(*sem):
    return pltpu.CompilerParams(dimension_semantics=sem, vmem_limit_bytes=VMEM_LIMIT)


def _dot(a, b):
    return jnp.dot(a.astype(BF16), b.astype(BF16), preferred_element_type=F32)


def _dot_nt(a, b):
    return lax.dot_general(a.astype(BF16), b.astype(BF16), (((1,), (1,)), ((), ())),
                           preferred_element_type=F32)


def _dot_tn(a, b):
    return lax.dot_general(a.astype(BF16), b.astype(BF16), (((0,), (0,)), ((), ())),
                           preferred_element_type=F32)


def _dot_f32(a, b):
    return jnp.dot(a, b, precision=HIGHEST, preferred_element_type=F32)


def _layer_norm(z, g, b):
    mu = jnp.mean(z, axis=-1, keepdims=True)
    zc = z - mu
    var = jnp.mean(zc * zc, axis=-1, keepdims=True)
    return zc * lax.rsqrt(var + LN_EPS) * g + b


def _sigmoid(x):
    return 0.5 * jnp.tanh(0.5 * x) + 0.5


def _silu(x):
    return x * _sigmoid(x)


def _ada_kernel(cc_ref, w_ref, b_ref, o_ref):
    s = _silu(cc_ref[...])
    o_ref[0] = _dot_f32(s, w_ref[0]) + b_ref[0]


def _ada_mod(cc, w_ada, b_ada):
    L, D, D6 = w_ada.shape
    R = cc.shape[0]
    tn = D6 // 4
    return pl.pallas_call(
        _ada_kernel,
        grid=(L, D6 // tn),
        in_specs=[pl.BlockSpec((R, D), lambda l, j: (0, 0)),
                  pl.BlockSpec((1, D, tn), lambda l, j: (l, 0, j)),
                  pl.BlockSpec((1, 1, tn), lambda l, j: (l, 0, j))],
        out_specs=pl.BlockSpec((1, R, tn), lambda l, j: (l, 0, j)),
        out_shape=jax.ShapeDtypeStruct((L, R, D6), F32),
        compiler_params=_cparams("parallel", "parallel"),
        name="ada_mod",
    )(cc, w_ada, b_ada.reshape(L, 1, D6))


def _addpos_kernel(x_ref, p_ref, o_ref):
    o_ref[0] = x_ref[0] + p_ref[...]


def _add_pos(x, pos):
    B, T, D = x.shape
    tt = min(T, 512)
    return pl.pallas_call(
        _addpos_kernel,
        grid=(T // tt, B),
        in_specs=[pl.BlockSpec((1, tt, D), lambda t, b: (b, t, 0)),
                  pl.BlockSpec((tt, D), lambda t, b: (t, 0))],
        out_specs=pl.BlockSpec((1, tt, D), lambda t, b: (b, t, 0)),
        out_shape=jax.ShapeDtypeStruct((B, T, D), F32),
        compiler_params=_cparams("parallel", "parallel"),
        name="add_pos",
    )(x, pos)


def _grid_pos_table(n_tokens, dim):
    rows = n_tokens // GRID_W
    quarter = dim // 4
    omega = 1.0 / (10000.0 ** (jnp.arange(quarter, dtype=F32) / quarter))
    r = jnp.arange(rows, dtype=F32)[:, None, None] * omega
    cl = jnp.arange(GRID_W, dtype=F32)[None, :, None] * omega
    shape = (rows, GRID_W, quarter)
    emb = jnp.concatenate([jnp.broadcast_to(jnp.sin(r), shape), jnp.broadcast_to(jnp.cos(r), shape),
                           jnp.broadcast_to(jnp.sin(cl), shape), jnp.broadcast_to(jnp.cos(cl), shape)], axis=-1)
    return emb.reshape(rows * GRID_W, dim)


def _inproj_kernel(x_ref, mod_ref, w_ref, wglr_ref, o_ref, glr_ref, h_ref, *, D):
    @pl.when(pl.program_id(1) == 0)
    def _():
        shift = mod_ref[0, :, 0:D]
        scale = mod_ref[0, :, D:2 * D]
        h = (x_ref[...] * (1.0 + scale) + shift).astype(BF16)
        h_ref[...] = h
        glr_ref[...] = jnp.dot(h, wglr_ref[...], preferred_element_type=F32)

    o_ref[...] = jnp.dot(h_ref[...], w_ref[...], preferred_element_type=F32)


def _row_tile(N, T, per_batch_mod, want):
    tm = min(want, T if per_batch_mod else N)
    while N % tm or (per_batch_mod and T % tm):
        tm //= 2
    return tm


def _in_proj(x2, mod, w_pack, w_glr, T):
    N, D = x2.shape
    NC = w_pack.shape[1]
    Bm = mod.shape[0]
    tm = _row_tile(N, T, Bm > 1, 1024)
    tn = D
    per_b = T // tm
    mod_map = (lambda i, j: (i // per_b, 0, 0)) if Bm > 1 else (lambda i, j: (0, 0, 0))
    return pl.pallas_call(
        functools.partial(_inproj_kernel, D=D),
        grid=(N // tm, NC // tn),
        in_specs=[pl.BlockSpec((tm, D), lambda i, j: (i, 0)),
                  pl.BlockSpec((1, 1, 6 * D), mod_map),
                  pl.BlockSpec((D, tn), lambda i, j: (0, j)),
                  pl.BlockSpec((D, LANES), lambda i, j: (0, 0))],
        out_specs=[pl.BlockSpec((tm, tn), lambda i, j: (i, j)),
                   pl.BlockSpec((tm, LANES), lambda i, j: (i, 0))],
        out_shape=[jax.ShapeDtypeStruct((N, NC), F32), jax.ShapeDtypeStruct((N, LANES), F32)],
        scratch_shapes=[pltpu.VMEM((tm, D), BF16)],
        compiler_params=_cparams("parallel", "arbitrary"),
        name="in_proj",
    )(x2, mod, w_pack, w_glr)


def _gla_kernel(*refs, T, DK, DV, has_init):
    if has_init:
        (q_ref, k_ref, v_ref, glr_ref, wg_ref, bg_ref, s0_ref,
         o_ref, sfin_ref, qb_ref, keb_ref, db_ref, sf_ref, sb_ref) = refs
    else:
        (q_ref, k_ref, v_ref, glr_ref, wg_ref, bg_ref,
         o_ref, sfin_ref, qb_ref, keb_ref, db_ref, sf_ref, sb_ref) = refs
    C = GLA_CHUNK
    BLK = GLA_BLOCK
    NS = BLK // C
    nblk = T // BLK
    qscale = DK ** -0.5
    shift = C.bit_length() - 1
    row = lax.broadcasted_iota(jnp.int32, (BLK, BLK), 0)
    col = lax.broadcasted_iota(jnp.int32, (BLK, BLK), 1)
    same = lax.shift_right_logical(row, shift) == lax.shift_right_logical(col, shift)
    lower = jnp.logical_and(same, row >= col)
    upper = jnp.logical_and(same, row <= col)
    tri = jnp.where(lower, 1.0, 0.0).astype(BF16)
    chunk_of_row = lax.shift_right_logical(lax.broadcasted_iota(jnp.int32, (BLK, DK), 0), shift)
    wg = wg_ref[0]
    bg = bg_ref[0]

    def chunk_cols(ke):
        return jnp.concatenate([jnp.where(chunk_of_row == c, ke, 0.0) for c in range(NS)], axis=1).astype(BF16)

    def per_chunk_last(x, r):
        return jnp.concatenate([jnp.broadcast_to(x[c * C + r:c * C + r + 1, :], (C, x.shape[1]))
                                for c in range(NS)], axis=0)

    if has_init:
        sf_ref[...] = s0_ref[0, 0, 0].T
        sb_ref[...] = s0_ref[0, 1, 0].T
    else:
        sf_ref[...] = jnp.zeros((DV, DK), F32)
        sb_ref[...] = jnp.zeros((DV, DK), F32)

    def fwd_body(i, carry):
        r0 = pl.multiple_of(i * BLK, BLK)
        rows = pl.ds(r0, BLK)
        qc = q_ref[rows, :] * qscale
        kc = k_ref[rows, :]
        vc = v_ref[rows, :]
        logits = _dot(glr_ref[rows, :], wg) + bg
        la = (jnp.minimum(logits, 0.0) - jnp.log1p(jnp.exp(-jnp.abs(logits)))) * (1.0 / GLA_TAU)
        p1 = la.astype(BF16)
        r1 = la - p1.astype(F32)
        p2 = r1.astype(BF16)
        p3 = (r1 - p2.astype(F32)).astype(BF16)
        pre = (jnp.dot(tri, p1, preferred_element_type=F32) + jnp.dot(tri, p2, preferred_element_type=F32)
               + jnp.dot(tri, p3, preferred_element_type=F32))
        tot = per_chunk_last(pre, C - 1)
        cum_f = pre[:, :DK]
        last_f = tot[:, :DK]
        cum_b = tot[:, DK:] - pre[:, DK:] + la[:, DK:]
        last_b = tot[:, DK:]
        q_f = qc * jnp.exp(cum_f)
        k_f = kc * jnp.exp(-cum_f)
        ke_f = kc * jnp.exp(last_f - cum_f)
        q_b = qc * jnp.exp(cum_b)
        k_b = kc * jnp.exp(-cum_b)
        ke_b = kc * jnp.exp(last_b - cum_b)
        sc = jnp.where(lower, _dot_nt(q_f, k_f), 0.0) + jnp.where(upper, _dot_nt(q_b, k_b), 0.0)
        o_blk = _dot(sc, vc)
        u_f = jnp.dot(vc.T.astype(BF16), chunk_cols(ke_f), preferred_element_type=F32)
        d_f = jnp.exp(last_f)
        s = sf_ref[...]
        for c in range(NS):
            rc = slice(c * C, (c + 1) * C)
            o_ref[pl.ds(r0 + c * C, C), :] = o_blk[rc] + _dot_nt(q_f[rc], s)
            s = s * d_f[c * C:c * C + 1, :] + u_f[:, c * DK:(c + 1) * DK]
        sf_ref[...] = s
        qb_ref[rows, :] = q_b
        keb_ref[rows, :] = ke_b
        d_b = jnp.exp(last_b)
        db_ref[pl.ds(i, 1), :] = jnp.concatenate([d_b[c * C:c * C + 1, :] for c in range(NS)], axis=1)
        return carry

    lax.fori_loop(0, nblk, fwd_body, 0)

    def bwd_body(j, carry):
        i = nblk - 1 - j
        r0 = pl.multiple_of(i * BLK, BLK)
        rows = pl.ds(r0, BLK)
        q_b = qb_ref[rows, :]
        u_b = jnp.dot(v_ref[rows, :].T.astype(BF16), chunk_cols(keb_ref[rows, :]), preferred_element_type=F32)
        d_b = db_ref[pl.ds(i, 1), :]
        s = sb_ref[...]
        for c in reversed(range(NS)):
            rc = pl.ds(r0 + c * C, C)
            o_ref[rc, :] = o_ref[rc, :] + _dot_nt(q_b[c * C:(c + 1) * C], s)
            s = s * d_b[:, c * DK:(c + 1) * DK] + u_b[:, c * DK:(c + 1) * DK]
        sb_ref[...] = s
        return carry

    lax.fori_loop(0, nblk, bwd_body, 0)

    sfin_ref[0, 0, 0] = sf_ref[...].T
    sfin_ref[0, 1, 0] = sb_ref[...].T


def _gla_core(P, glr, wg, bg, s0, B, T, D):
    H = GLA_HEADS
    DK = D // 2 // H
    DV = D // H
    N = B * T
    has_init = s0 is not None
    kcol = (D // 2) // DK
    vcol = D // DV
    in_specs = [pl.BlockSpec((T, DK), lambda b, h: (b, h)),
                pl.BlockSpec((T, DK), lambda b, h: (b, kcol + h)),
                pl.BlockSpec((T, DV), lambda b, h: (b, vcol + h)),
                pl.BlockSpec((T, LANES), lambda b, h: (b, 0)),
                pl.BlockSpec((1, LANES, 2 * DK), lambda b, h: (h, 0, 0)),
                pl.BlockSpec((1, 1, 2 * DK), lambda b, h: (h, 0, 0))]
    args = [P, P, P, glr, wg, bg]
    if has_init:
        in_specs.append(pl.BlockSpec((1, 2, 1, DK, DV), lambda b, h: (b, 0, h, 0, 0)))
        args.append(s0)
    o, sfin = pl.pallas_call(
        functools.partial(_gla_kernel, T=T, DK=DK, DV=DV, has_init=has_init),
        grid=(B, H),
        in_specs=in_specs,
        out_specs=[pl.BlockSpec((T, DV), lambda b, h: (b, h)),
                   pl.BlockSpec((1, 2, 1, DK, DV), lambda b, h: (b, 0, h, 0, 0))],
        out_shape=[jax.ShapeDtypeStruct((N, D), F32),
                   jax.ShapeDtypeStruct((B, 2, H, DK, DV), F32)],
        scratch_shapes=[pltpu.VMEM((T, DK), F32), pltpu.VMEM((T, DK), F32),
                        pltpu.VMEM((T // GLA_BLOCK, (GLA_BLOCK // GLA_CHUNK) * DK), F32),
                        pltpu.VMEM((DV, DK), F32), pltpu.VMEM((DV, DK), F32)],
        compiler_params=_cparams("parallel", "parallel"),
        name="gla_core",
    )(*args)
    return o, sfin


def _s5prep_kernel(lr_ref, li_ref, ls_ref, btr_ref, bti_ref, cr_ref, ci_ref,
                   kt_ref, wre_ref, wim_ref, cpre_ref, cpimn_ref, pre_ref, pim_ref):
    I = S5_GROUP
    lr = lr_ref[0, 0]
    li = li_ref[0, 0]
    step = jnp.exp(ls_ref[0, 0])
    mag = jnp.exp(lr * step)
    ang = li * step
    a_re = mag * jnp.cos(ang)
    a_im = mag * jnp.sin(ang)
    den = lr * lr + li * li
    nr = a_re - 1.0
    f_re = (nr * lr + a_im * li) / den
    f_im = (a_im * lr - nr * li) / den
    btr = btr_ref[0]
    bti = bti_ref[0]
    bb_re = f_re[:, None, :] * btr - f_im[:, None, :] * bti
    bb_im = f_re[:, None, :] * bti + f_im[:, None, :] * btr
    c_re = cr_ref[0]
    c_im = ci_ref[0]
    p_re = jnp.ones_like(lr)
    p_im = jnp.zeros_like(lr)
    batched_nt = (((2,), (2,)), ((0,), (0,)))
    for e in range(S5_SUB + 1):
        pr = p_re[:, None, :]
        pi = p_im[:, None, :]
        cp_re = c_re * pr - c_im * pi
        cp_im = c_re * pi + c_im * pr
        cpre_ref[0, 0, :, e * I:(e + 1) * I, :] = cp_re
        cpimn_ref[0, 0, :, e * I:(e + 1) * I, :] = -cp_im
        if e < S5_SUB:
            wre_ref[0, 0, :, e * I:(e + 1) * I, :] = pr * bb_re - pi * bb_im
            wim_ref[0, 0, :, e * I:(e + 1) * I, :] = pr * bb_im + pi * bb_re
            kt_ref[0, 0, :, e * I:(e + 1) * I, :] = (
                lax.dot_general(cp_re, bb_re, batched_nt, precision=HIGHEST, preferred_element_type=F32)
                - lax.dot_general(cp_im, bb_im, batched_nt, precision=HIGHEST, preferred_element_type=F32))
            p_re, p_im = p_re * a_re - p_im * a_im, p_re * a_im + p_im * a_re
    pre_ref[0, 0] = p_re
    pim_ref[0, 0] = p_im


def _s5_operators(lam_re, lam_im, log_step, b_re, b_im, c_re, c_im):
    L, _, G, Pn = lam_re.shape
    I = S5_GROUP
    S = S5_SUB
    ls = jnp.broadcast_to(log_step[..., None], lam_re.shape)
    btr = jnp.swapaxes(b_re, -1, -2)
    bti = jnp.swapaxes(b_im, -1, -2)
    Gb = SUBLANES
    lam_spec = pl.BlockSpec((1, 1, Gb, Pn), lambda l, d, g: (l, d, g, 0))
    gip_spec = pl.BlockSpec((1, Gb, I, Pn), lambda l, d, g: (l, g, 0, 0))

    def out_spec(rows, last):
        return pl.BlockSpec((1, 1, Gb, rows, last), lambda l, d, g: (l, d, g, 0, 0))

    kt, wre, wim, cpre, cpimn, pre, pim = pl.pallas_call(
        _s5prep_kernel,
        grid=(L, 2, G // Gb),
        in_specs=[lam_spec, lam_spec, lam_spec, gip_spec, gip_spec, gip_spec, gip_spec],
        out_specs=[out_spec(S * I, I), out_spec(S * I, Pn), out_spec(S * I, Pn),
                   out_spec((S + 1) * I, Pn), out_spec((S + 1) * I, Pn), lam_spec, lam_spec],
        out_shape=[jax.ShapeDtypeStruct((L, 2, G, S * I, I), F32),
                   jax.ShapeDtypeStruct((L, 2, G, S * I, Pn), F32),
                   jax.ShapeDtypeStruct((L, 2, G, S * I, Pn), F32),
                   jax.ShapeDtypeStruct((L, 2, G, (S + 1) * I, Pn), F32),
                   jax.ShapeDtypeStruct((L, 2, G, (S + 1) * I, Pn), F32),
                   jax.ShapeDtypeStruct((L, 2, G, Pn), F32),
                   jax.ShapeDtypeStruct((L, 2, G, Pn), F32)],
        compiler_params=_cparams("parallel", "parallel", "parallel"),
        name="s5_prep",
    )(lam_re, lam_im, ls, btr, bti, c_re, c_im)

    kt = kt.reshape(L, 2, G, S, I, I)
    s_idx = jnp.arange(S)[:, None]
    t_idx = jnp.arange(S)[None, :]

    def toeplitz(k, lag, valid):
        m = k[:, :, jnp.clip(lag, 0, S - 1)]
        m = jnp.where(valid[None, None, :, :, None, None], m, 0.0)
        return m.transpose(0, 1, 2, 5, 3, 4).reshape(L, G, S * I, S * I)

    mf = toeplitz(kt[:, 0], t_idx - s_idx, t_idx >= s_idx)
    mb = toeplitz(kt[:, 1], s_idx - t_idx, s_idx >= t_idx)

    def by_pos(w, reverse):
        w = w.reshape(L, G, S, I, Pn)
        if reverse:
            w = w[:, :, ::-1]
        return w.reshape(L, G, S * I, Pn)

    wall = jnp.concatenate([by_pos(wre[:, 0], True), by_pos(wre[:, 1], False),
                            by_pos(wim[:, 0], True), by_pos(wim[:, 1], False)], axis=-1)

    def readout(cp):
        cp = cp.reshape(L, 2, G, S + 1, I, Pn)
        f = cp[:, 0, :, 1:]
        b = cp[:, 1, :, 1:][:, :, ::-1]
        f = f.reshape(L, G, S * I, Pn).swapaxes(-1, -2)
        b = b.reshape(L, G, S * I, Pn).swapaxes(-1, -2)
        return jnp.concatenate([f, b], axis=2)

    vre = readout(cpre)
    vim = readout(cpimn)
    are = jnp.concatenate([pre[:, 0], pre[:, 1]], axis=-1)[:, :, None, :]
    aim = jnp.concatenate([pim[:, 0], pim[:, 1]], axis=-1)[:, :, None, :]
    t = lambda a: jnp.swapaxes(a, -1, -2)
    return {"row": (mf, mb, wall, vre, vim, are, aim),
            "col": (t(mf), t(mb), t(wall), t(vre), t(vim), are, aim)}


def _s5_kernel(*refs, R, B, has_init):
    if has_init:
        (x_ref, mf_ref, mb_ref, w_ref, vre_ref, vim_ref, are_ref, aim_ref, s0re_ref, s0im_ref,
         y_ref, fre_ref, fim_ref, zre_ref, zim_ref, hre_ref, him_ref) = refs
    else:
        (x_ref, mf_ref, mb_ref, w_ref, vre_ref, vim_ref, are_ref, aim_ref,
         y_ref, fre_ref, fim_ref, zre_ref, zim_ref, hre_ref, him_ref) = refs
    Pn = S5_STATE
    x = x_ref[0].astype(BF16)
    m = (mf_ref[0] + mb_ref[0]).astype(BF16)
    y_ref[0] = jnp.dot(x, m, preferred_element_type=F32)
    z = jnp.dot(x, w_ref[0].astype(BF16), preferred_element_type=F32)
    zre_ref[...] = z[:, :2 * Pn]
    zim_ref[...] = z[:, 2 * Pn:]
    a_re = are_ref[0]
    a_im = aim_ref[0]
    is_fwd = lax.broadcasted_iota(jnp.int32, (B, 2 * Pn), 1) < Pn
    if has_init:
        h0 = (s0re_ref[0], s0im_ref[0])
    else:
        h0 = (jnp.zeros((B, 2 * Pn), F32), jnp.zeros((B, 2 * Pn), F32))

    def fwd_step(r, h):
        hr, hi = h
        rows = pl.ds(pl.multiple_of(r * B, B), B)
        hre_ref[rows, :] = hr
        him_ref[rows, :] = hi
        return (hr * a_re - hi * a_im + zre_ref[rows, :], hr * a_im + hi * a_re + zim_ref[rows, :])

    hf = lax.fori_loop(0, R, fwd_step, h0, unroll=SCAN_UNROLL)

    def bwd_step(j, h):
        hr, hi = h
        rows = pl.ds(pl.multiple_of((R - 1 - j) * B, B), B)
        hre_ref[rows, :] = jnp.where(is_fwd, hre_ref[rows, :], hr)
        him_ref[rows, :] = jnp.where(is_fwd, him_ref[rows, :], hi)
        return (hr * a_re - hi * a_im + zre_ref[rows, :], hr * a_im + hi * a_re + zim_ref[rows, :])

    hb = lax.fori_loop(0, R, bwd_step, h0, unroll=SCAN_UNROLL)
    fre_ref[0] = jnp.where(is_fwd, hf[0], hb[0])
    fim_ref[0] = jnp.where(is_fwd, hf[1], hb[1])
    y_ref[0] = y_ref[0] + _dot(hre_ref[...], vre_ref[0]) + _dot(him_ref[...], vim_ref[0])


def _s5_core(xg, ops, s0re, s0im, R, B):
    G, RB, K = xg.shape
    mf, mb, wall, vre, vim, are, aim = ops
    P2 = 2 * S5_STATE
    has_init = s0re is not None
    gspec = lambda shape: pl.BlockSpec((1,) + shape, lambda g: (g, 0, 0))
    in_specs = [gspec((RB, K)), gspec((K, K)), gspec((K, K)), gspec((K, K)),
                gspec((P2, K)), gspec((P2, K)), gspec((1, P2)), gspec((1, P2))]
    args = [xg, mf, mb, wall, vre, vim, are, aim]
    if has_init:
        in_specs += [gspec((B, P2)), gspec((B, P2))]
        args += [s0re, s0im]
    return pl.pallas_call(
        functools.partial(_s5_kernel, R=R, B=B, has_init=has_init),
        grid=(G,),
        in_specs=in_specs,
        out_specs=[gspec((RB, K)), gspec((B, P2)), gspec((B, P2))],
        out_shape=[jax.ShapeDtypeStruct((G, RB, K), F32),
                   jax.ShapeDtypeStruct((G, B, P2), F32),
                   jax.ShapeDtypeStruct((G, B, P2), F32)],
        scratch_shapes=[pltpu.VMEM((RB, P2), F32) for _ in range(4)],
        compiler_params=_cparams("parallel"),
        name="s5_core",
    )(*args)


def _s5projt_kernel(x_ref, mod_ref, wt_ref, o_ref, *, D):
    shift = mod_ref[0, :, 0:D]
    scale = mod_ref[0, :, D:2 * D]
    h = (x_ref[...] * (1.0 + scale) + shift).astype(BF16)
    ut = lax.dot_general(wt_ref[...], h, (((1,), (1,)), ((), ())), preferred_element_type=F32)
    o_ref[...] = ut.reshape(o_ref.shape).astype(o_ref.dtype)


def _s5_proj_t(x2, mod, w_ut, B, T):
    N, D = x2.shape
    S = S5_SUB
    R = T // S
    G = D // S5_GROUP
    xv = x2.reshape(B * R, S * D)
    return pl.pallas_call(
        functools.partial(_s5projt_kernel, D=D),
        grid=(B, S),
        in_specs=[pl.BlockSpec((R, D), lambda b, s: (b, s)),
                  pl.BlockSpec((1, 1, 6 * D), lambda b, s: (b, 0, 0)),
                  pl.BlockSpec((D, D), lambda b, s: (0, 0))],
        out_specs=pl.BlockSpec((G, S5_GROUP, R), lambda b, s: (0, s, b)),
        out_shape=jax.ShapeDtypeStruct((G, S * S5_GROUP, B * R), BF16),
        compiler_params=_cparams("parallel", "parallel"),
        name="s5_proj_t",
    )(xv, mod, w_ut)


def _s5t_kernel(xt_ref, mft_ref, mbt_ref, wt_ref, vret_ref, vimt_ref, are_ref, aim_ref, s0re_ref, s0im_ref,
                yt_ref, fre_ref, fim_ref, zt_ref, zre_ref, zim_ref, hre_ref, him_ref, *, R, B):
    Pn = S5_STATE
    xt = xt_ref[0]
    mt = (mft_ref[0] + mbt_ref[0]).astype(BF16)
    yt_ref[0] = jnp.dot(mt, xt, preferred_element_type=F32)
    zt_ref[...] = jnp.dot(wt_ref[0].astype(BF16), xt, preferred_element_type=F32)
    for b in range(B):
        zb = zt_ref[:, b * R:(b + 1) * R].T
        zre_ref[pl.ds(b, R, stride=B), :] = zb[:, :2 * Pn]
        zim_ref[pl.ds(b, R, stride=B), :] = zb[:, 2 * Pn:]
    a_re = are_ref[0]
    a_im = aim_ref[0]
    is_fwd = lax.broadcasted_iota(jnp.int32, (B, 2 * Pn), 1) < Pn
    h0 = (s0re_ref[0], s0im_ref[0])

    def fwd_step(r, h):
        hr, hi = h
        rows = pl.ds(pl.multiple_of(r * B, B), B)
        hre_ref[rows, :] = hr
        him_ref[rows, :] = hi
        return (hr * a_re - hi * a_im + zre_ref[rows, :], hr * a_im + hi * a_re + zim_ref[rows, :])

    hf = lax.fori_loop(0, R, fwd_step, h0, unroll=SCAN_UNROLL)

    def bwd_step(j, h):
        hr, hi = h
        rows = pl.ds(pl.multiple_of((R - 1 - j) * B, B), B)
        hre_ref[rows, :] = jnp.where(is_fwd, hre_ref[rows, :], hr)
        him_ref[rows, :] = jnp.where(is_fwd, him_ref[rows, :], hi)
        return (hr * a_re - hi * a_im + zre_ref[rows, :], hr * a_im + hi * a_re + zim_ref[rows, :])

    hb = lax.fori_loop(0, R, bwd_step, h0, unroll=SCAN_UNROLL)
    fre_ref[0] = jnp.where(is_fwd, hf[0], hb[0])
    fim_ref[0] = jnp.where(is_fwd, hf[1], hb[1])
    vret = vret_ref[0]
    vimt = vimt_ref[0]
    for b in range(B):
        hre_b = hre_ref[pl.ds(b, R, stride=B), :]
        him_b = him_ref[pl.ds(b, R, stride=B), :]
        cols = slice(b * R, (b + 1) * R)
        yt_ref[0, :, cols] = yt_ref[0, :, cols] + _dot_nt(vret, hre_b) + _dot_nt(vimt, him_b)


def _s5_core_t(xt, ops, s0re, s0im, R, B):
    G, K, BR = xt.shape
    mft, mbt, wt, vret, vimt, are, aim = ops
    P2 = 2 * S5_STATE
    gspec = lambda shape: pl.BlockSpec((1,) + shape, lambda g: (g, 0, 0))
    return pl.pallas_call(
        functools.partial(_s5t_kernel, R=R, B=B),
        grid=(G,),
        in_specs=[gspec((K, BR)), gspec((K, K)), gspec((K, K)), gspec((K, K)),
                  gspec((K, P2)), gspec((K, P2)), gspec((1, P2)), gspec((1, P2)),
                  gspec((B, P2)), gspec((B, P2))],
        out_specs=[gspec((K, BR)), gspec((B, P2)), gspec((B, P2))],
        out_shape=[jax.ShapeDtypeStruct((G, K, BR), F32),
                   jax.ShapeDtypeStruct((G, B, P2), F32),
                   jax.ShapeDtypeStruct((G, B, P2), F32)],
        scratch_shapes=[pltpu.VMEM((K, BR), F32)] + [pltpu.VMEM((BR, P2), F32) for _ in range(4)],
        compiler_params=_cparams("parallel"),
        name="s5_core_t",
    )(xt, mft, mbt, wt, vret, vimt, are, aim, s0re, s0im)


def _s5rows_kernel(yt_ref, o_ref, *, R):
    I = S5_GROUP
    for t in range(S5_SUB):
        blk = yt_ref[:, t * I:(t + 1) * I, :]
        o_ref[pl.ds(t, R, stride=S5_SUB), :] = blk.reshape(LANES, R).T


def _s5_rows(yt, B, T, D):
    G = D // S5_GROUP
    R = T // S5_SUB
    gpt = LANES // S5_GROUP
    return pl.pallas_call(
        functools.partial(_s5rows_kernel, R=R),
        grid=(B, G // gpt),
        in_specs=[pl.BlockSpec((gpt, S5_SUB * S5_GROUP, R), lambda b, j: (j, 0, b))],
        out_specs=pl.BlockSpec((T, LANES), lambda b, j: (b, j)),
        out_shape=jax.ShapeDtypeStruct((B * T, D), F32),
        compiler_params=_cparams("parallel", "parallel"),
        name="s5_rows",
    )(yt)


def _lru_kernel(*refs, Tt, CB, reverse, has_init):
    refs = list(refs)
    lx_ref, prev_ref, next_ref, cw_ref, cb_ref, wg_ref, bgate_ref, lam_ref = refs[:8]
    pos = 8
    h0_ref = None
    if has_init:
        h0_ref = refs[pos]
        pos += 1
    if reverse:
        hf_ref, ly_ref = refs[pos:pos + 2]
        pos += 2
    out_ref, fin_ref, x_scr, a_scr, b_scr, h_scr, carry_ref = refs[pos:pos + 7]
    NBk = CB // LRU_BLOCK
    k = pl.program_id(2)
    nk = pl.num_programs(2)
    first_tile = k == 0

    @pl.when(first_tile)
    def _():
        if has_init:
            carry_ref[...] = h0_ref[...]
        else:
            carry_ref[...] = jnp.zeros((SUBLANES, CB), F32)

    tpos = (nk - 1 - k) if reverse else k
    has_prev = tpos > 0
    has_next = tpos < nk - 1
    cw = cw_ref[0]
    cbias = cb_ref[0]
    bgate = bgate_ref[0]
    lam = lam_ref[0]
    cfac = -LRU_C * (jnp.maximum(-lam, 0.0) + jnp.log1p(jnp.exp(-jnp.abs(lam))))

    left = LRU_CONV // 2
    for b in range(SUBLANES):
        cur = lx_ref[b]
        pv = jnp.where(has_prev, prev_ref[b], 0.0)
        nx = jnp.where(has_next, next_ref[b], 0.0)
        for c in range(NBk):
            sl = slice(c * LRU_BLOCK, (c + 1) * LRU_BLOCK)
            x_scr[c, pl.ds(left * SUBLANES + b, Tt, stride=SUBLANES), :] = cur[:, sl]
            for q in range(left):
                x_scr[c, pl.ds(q * SUBLANES + b, 1), :] = pv[SUBLANES - left + q:SUBLANES - left + q + 1, sl]
            for q in range(LRU_CONV - 1 - left):
                x_scr[c, pl.ds((Tt + left + q) * SUBLANES + b, 1), :] = nx[q:q + 1, sl]

    RC = min(Tt * SUBLANES, 512)
    for c in range(NBk):
        sl = slice(c * LRU_BLOCK, (c + 1) * LRU_BLOCK)
        bias_c = jnp.concatenate([bgate[:, sl], bgate[:, CB + c * LRU_BLOCK: CB + (c + 1) * LRU_BLOCK]], axis=1)
        for r0 in range(0, Tt * SUBLANES, RC):
            xc = cbias[:, sl] + sum(cw[j:j + 1, sl] * x_scr[c, pl.ds(r0 + j * SUBLANES, RC), :]
                                    for j in range(LRU_CONV))
            gates = _dot(xc, wg_ref[0, c]) + bias_c
            log_a = cfac[:, sl] * _sigmoid(gates[:, :LRU_BLOCK])
            a = jnp.exp(log_a)
            om = -jnp.tanh(log_a) * (1.0 + a * a)
            root = om * lax.rsqrt(jnp.maximum(om, F32_TINY))
            bt = root * (_sigmoid(gates[:, LRU_BLOCK:]) * xc)
            a_scr[c, pl.ds(r0, RC), :] = a
            b_scr[c, pl.ds(r0, RC), :] = bt

    def step(s, h):
        t = (Tt - 1 - s) if reverse else s
        rows = pl.ds(pl.multiple_of(t * SUBLANES, SUBLANES), SUBLANES)
        new = []
        for c in range(NBk):
            hc = a_scr[c, rows, :] * h[c] + b_scr[c, rows, :]
            h_scr[c, rows, :] = hc
            new.append(hc)
        return tuple(new)

    h_init = tuple(carry_ref[:, c * LRU_BLOCK:(c + 1) * LRU_BLOCK] for c in range(NBk))
    h_last = lax.fori_loop(0, Tt, step, h_init, unroll=SCAN_UNROLL)
    for c in range(NBk):
        carry_ref[:, c * LRU_BLOCK:(c + 1) * LRU_BLOCK] = h_last[c]
        fin_ref[:, c * LRU_BLOCK:(c + 1) * LRU_BLOCK] = h_last[c]

    for b in range(SUBLANES):
        hb = jnp.concatenate([h_scr[c, pl.ds(b, Tt, stride=SUBLANES), :] for c in range(NBk)], axis=1)
        if reverse:
            out_ref[b] = (hf_ref[b] + hb) * jax.nn.gelu(ly_ref[b])
        else:
            out_ref[b] = hb


def _lru_sweep(P3, lw, h0, hf, reverse, B, T, D):
    W = 3 * D // 2
    CB = 2 * LRU_BLOCK
    Tt = min(T, 256)
    nk = T // Tt
    nb8 = Tt // SUBLANES
    lxcol = (D // 2 + D // 2 + D + D + D) // CB
    lycol = lxcol + W // CB
    d = 1 if reverse else 0
    has_init = h0 is not None

    def tmap(k):
        return (nk - 1 - k) if reverse else k

    in_specs = [
        pl.BlockSpec((SUBLANES, Tt, CB), lambda g, j, k: (g, tmap(k), lxcol + j)),
        pl.BlockSpec((SUBLANES, SUBLANES, CB),
                     lambda g, j, k: (g, jnp.maximum(tmap(k) * nb8 - 1, 0), lxcol + j)),
        pl.BlockSpec((SUBLANES, SUBLANES, CB),
                     lambda g, j, k: (g, jnp.minimum((tmap(k) + 1) * nb8, T // SUBLANES - 1), lxcol + j)),
        pl.BlockSpec((1, LRU_CONV, CB), lambda g, j, k: (j, 0, 0)),
        pl.BlockSpec((1, 1, CB), lambda g, j, k: (j, 0, 0)),
        pl.BlockSpec((1, CB // LRU_BLOCK, LRU_BLOCK, 2 * LRU_BLOCK), lambda g, j, k: (j, 0, 0, 0)),
        pl.BlockSpec((1, 1, 2 * CB), lambda g, j, k: (j, 0, 0)),
        pl.BlockSpec((1, 1, CB), lambda g, j, k: (j, 0, 0)),
    ]
    args = [P3, P3, P3, lw["conv_w"], lw["conv_b"], lw["wg"][d], lw["bg"][d], lw["lam"][d]]
    if has_init:
        in_specs.append(pl.BlockSpec((SUBLANES, CB), lambda g, j, k: (g, j)))
        args.append(h0[:, d])
    if reverse:
        in_specs.append(pl.BlockSpec((SUBLANES, Tt, CB), lambda g, j, k: (g, tmap(k), j)))
        in_specs.append(pl.BlockSpec((SUBLANES, Tt, CB), lambda g, j, k: (g, tmap(k), lycol + j)))
        args += [hf, P3]
    out, fin = pl.pallas_call(
        functools.partial(_lru_kernel, Tt=Tt, CB=CB, reverse=reverse, has_init=has_init),
        grid=(B // SUBLANES, W // CB, nk),
        in_specs=in_specs,
        out_specs=[pl.BlockSpec((SUBLANES, Tt, CB), lambda g, j, k: (g, tmap(k), j)),
                   pl.BlockSpec((SUBLANES, CB), lambda g, j, k: (g, j))],
        out_shape=[jax.ShapeDtypeStruct((B, T, W), F32), jax.ShapeDtypeStruct((B, W), F32)],
        scratch_shapes=[pltpu.VMEM((CB // LRU_BLOCK, (Tt + LRU_CONV) * SUBLANES, LRU_BLOCK), F32)]
        + [pltpu.VMEM((CB // LRU_BLOCK, Tt * SUBLANES, LRU_BLOCK), F32) for _ in range(3)]
        + [pltpu.VMEM((SUBLANES, CB), F32)],
        compiler_params=_cparams("parallel", "parallel", "arbitrary"),
        name="lru_bwd" if reverse else "lru_fwd",
    )(*args)
    return out, fin


def _merge_kernel(x_ref, mod_ref, o_ref, r_ref, ys_ref, u_ref, yl_ref, gg_ref, gs_ref, gl_ref,
                  gn_ref, sd_ref, wglu_ref, wbg_ref, wbs_ref, wbl_ref, wo_ref, lg_ref, lb_ref,
                  out_ref, *, D, alpha):
    DV = D // GLA_HEADS
    gn = gn_ref[...]
    parts = []
    for h in range(GLA_HEADS):
        sl = slice(h * DV, (h + 1) * DV)
        o = o_ref[:, sl]
        mu = jnp.mean(o, axis=-1, keepdims=True)
        oc = o - mu
        var = jnp.mean(oc * oc, axis=-1, keepdims=True)
        parts.append(oc * lax.rsqrt(var + LN_EPS) * gn[:, sl] * _silu(r_ref[:, sl]))
    y_gla = jnp.concatenate(parts, axis=1)
    ys = jax.nn.gelu(ys_ref[...] + sd_ref[...] * u_ref[...])
    y_s5 = ys * _sigmoid(_dot(ys, wglu_ref[...]))
    merged = (_sigmoid(gg_ref[...]) * _dot(y_gla, wbg_ref[...])
              + _sigmoid(gs_ref[...]) * _dot(y_s5, wbs_ref[...])
              + _sigmoid(gl_ref[...]) * _dot(yl_ref[...], wbl_ref[...]))
    mix = _dot(merged, wo_ref[...])
    gate1 = mod_ref[0, :, 2 * D:3 * D]
    out_ref[...] = _layer_norm(alpha * x_ref[...] + gate1 * mix, lg_ref[...], lb_ref[...])


def _merge(x2, mod, o_gla, P, ys, ylru, lw, T, alpha):
    N, D = x2.shape
    W = 3 * D // 2
    Bm = mod.shape[0]
    rcol, ucol, gcol = 2, 3, 7
    tm = _row_tile(N, T, Bm > 1, 256)
    per_b = T // tm
    mod_map = (lambda i: (i // per_b, 0, 0)) if Bm > 1 else (lambda i: (0, 0, 0))
    row = lambda width, cb=0: pl.BlockSpec((tm, width), lambda i: (i, cb))
    full = lambda a: pl.BlockSpec(a.shape, lambda i: (0,) * a.ndim)
    weights = [lw["gnorm"], lw["s5_d"], lw["w_glu"], lw["w_br_gla"], lw["w_br_s5"], lw["w_br_lru"],
               lw["w_out"], lw["ln1_g"], lw["ln1_b"]]
    return pl.pallas_call(
        functools.partial(_merge_kernel, D=D, alpha=alpha),
        grid=(N // tm,),
        in_specs=[row(D), pl.BlockSpec((1, 1, 6 * D), mod_map), row(D), row(D, rcol), row(D),
                  row(D, ucol), row(W), row(D, gcol), row(D, gcol + 1), row(D, gcol + 2)]
        + [full(w) for w in weights],
        out_specs=row(D),
        out_shape=jax.ShapeDtypeStruct((N, D), F32),
        compiler_params=_cparams("parallel"),
        name="merge",
    )(x2, mod, o_gla, P, ys, P, ylru, P, P, P, *weights)


def _mlp_kernel(x_ref, mod_ref, w1_ref, w2_ref, lg_ref, lb_ref, out_ref, h_ref, acc_ref, *, D, alpha):
    j = pl.program_id(1)

    @pl.when(j == 0)
    def _():
        shift = mod_ref[0, :, 3 * D:4 * D]
        scale = mod_ref[0, :, 4 * D:5 * D]
        h_ref[...] = (x_ref[...] * (1.0 + scale) + shift).astype(BF16)
        acc_ref[...] = jnp.zeros_like(acc_ref)

    hid = jnp.dot(h_ref[...], w1_ref[...], preferred_element_type=F32)
    hid = jnp.square(jnp.maximum(hid, 0.0))
    acc_ref[...] += _dot(hid, w2_ref[...])

    @pl.when(j == pl.num_programs(1) - 1)
    def _():
        gate2 = mod_ref[0, :, 5 * D:6 * D]
        out_ref[...] = _layer_norm(alpha * x_ref[...] + gate2 * acc_ref[...], lg_ref[...], lb_ref[...])


def _mlp(x2, mod, lw, T, alpha):
    N, D = x2.shape
    HID = lw["w_mlp_in"].shape[1]
    Bm = mod.shape[0]
    tm = _row_tile(N, T, Bm > 1, 1024)
    th = min(HID, 1024)
    per_b = T // tm
    mod_map = (lambda i, j: (i // per_b, 0, 0)) if Bm > 1 else (lambda i, j: (0, 0, 0))
    return pl.pallas_call(
        functools.partial(_mlp_kernel, D=D, alpha=alpha),
        grid=(N // tm, HID // th),
        in_specs=[pl.BlockSpec((tm, D), lambda i, j: (i, 0)),
                  pl.BlockSpec((1, 1, 6 * D), mod_map),
                  pl.BlockSpec((D, th), lambda i, j: (0, j)),
                  pl.BlockSpec((th, D), lambda i, j: (j, 0)),
                  pl.BlockSpec((1, D), lambda i, j: (0, 0)),
                  pl.BlockSpec((1, D), lambda i, j: (0, 0))],
        out_specs=pl.BlockSpec((tm, D), lambda i, j: (i, 0)),
        out_shape=jax.ShapeDtypeStruct((N, D), F32),
        scratch_shapes=[pltpu.VMEM((tm, D), BF16), pltpu.VMEM((tm, D), F32)],
        compiler_params=_cparams("parallel", "arbitrary"),
        name="mlp",
    )(x2, mod, lw["w_mlp_in"], lw["w_mlp_out"], lw["ln2_g"], lw["ln2_b"])


def _pack_layer_weights(l, D, w_in, gla_w_gate, gla_b_gate, gla_norm_g, s5_d, s5_w_glu,
                        lru_conv_w, lru_conv_b, lru_w_a, lru_b_a, lru_w_i, lru_b_i, lru_lam,
                        w_br_gla, w_br_s5, w_br_lru, w_out, ln1_g, ln1_b, ln2_g, ln2_b,
                        w_mlp_in, w_mlp_out):
    H = GLA_HEADS
    KEY = D // 2
    DK = KEY // H
    W = 3 * D // 2
    CB = 2 * LRU_BLOCK
    NB = W // LRU_BLOCK
    widths = (KEY, KEY, D, D, 2 * GATE_RANK, D, W, W, 3 * D)
    offs = [0]
    for wd in widths:
        offs.append(offs[-1] + wd)
    wl = w_in[l]
    piece = lambda i: wl[:, offs[i]:offs[i + 1]]
    w_pack = jnp.concatenate([piece(0), piece(1), piece(2), piece(3), piece(5), piece(6), piece(7),
                              piece(8)], axis=1).astype(BF16)
    w_glr = jnp.concatenate([piece(4), jnp.zeros((D, LANES - 2 * GATE_RANK), wl.dtype)], axis=1).astype(BF16)
    w_ut = piece(5).T.astype(BF16)
    wgate = gla_w_gate[l]
    wg = jnp.zeros((H, LANES, 2 * DK), F32)
    wg = wg.at[:, 0:GATE_RANK, 0:DK].set(wgate[0].reshape(GATE_RANK, H, DK).transpose(1, 0, 2))
    wg = wg.at[:, GATE_RANK:2 * GATE_RANK, DK:].set(wgate[1].reshape(GATE_RANK, H, DK).transpose(1, 0, 2))
    bgate = gla_b_gate[l].reshape(2, H, DK).transpose(1, 0, 2).reshape(H, 1, 2 * DK)
    lru_wg = jnp.concatenate([lru_w_a[l], lru_w_i[l]], axis=-1)
    lru_wg = lru_wg.reshape(2, W // CB, CB // LRU_BLOCK, LRU_BLOCK, 2 * LRU_BLOCK).astype(BF16)
    lru_bg = jnp.concatenate([lru_b_a[l].reshape(2, W // CB, 1, CB), lru_b_i[l].reshape(2, W // CB, 1, CB)],
                             axis=-1)
    return {
        "w_pack": w_pack, "w_glr": w_glr, "w_ut": w_ut, "gla_wg": wg.astype(BF16), "gla_bg": bgate,
        "lru": {"conv_w": lru_conv_w[l].reshape(LRU_CONV, W // CB, CB).transpose(1, 0, 2),
                "conv_b": lru_conv_b[l].reshape(W // CB, 1, CB),
                "wg": lru_wg, "bg": lru_bg, "lam": lru_lam[l].reshape(2, W // CB, 1, CB)},
        "gnorm": gla_norm_g[l].reshape(1, D), "s5_d": s5_d[l].reshape(1, D),
        "w_glu": s5_w_glu[l].astype(BF16), "w_br_gla": w_br_gla[l].astype(BF16),
        "w_br_s5": w_br_s5[l].astype(BF16), "w_br_lru": w_br_lru[l].astype(BF16),
        "w_out": w_out[l].astype(BF16),
        "ln1_g": ln1_g[l].reshape(1, D), "ln1_b": ln1_b[l].reshape(1, D),
        "ln2_g": ln2_g[l].reshape(1, D), "ln2_b": ln2_b[l].reshape(1, D),
        "w_mlp_in": w_mlp_in[l].astype(BF16), "w_mlp_out": w_mlp_out[l].astype(BF16),
    }


def _trunk_layer(x2, mod, lw, s5ops, init, B, T, D, alpha):
    N = B * T
    G = D // S5_GROUP
    S = S5_SUB
    R = T // S
    Pn = S5_STATE
    P, glr = _in_proj(x2, mod, lw["w_pack"], lw["w_glr"], T)
    gla0 = s5re0 = s5im0 = lru0 = None
    if init is not None:
        gla0, s5re0, s5im0, lru0 = init
        s5re0 = s5re0.transpose(2, 0, 1, 3).reshape(G, B, 2 * Pn)
        s5im0 = s5im0.transpose(2, 0, 1, 3).reshape(G, B, 2 * Pn)
    o_gla, gla_fin = _gla_core(P, glr, lw["gla_wg"], lw["gla_bg"], gla0, B, T, D)
    on_lanes = B == SUBLANES and R % LANES == 0 and mod.shape[0] == B
    if on_lanes:
        if s5re0 is None:
            s5re0 = jnp.zeros((G, B, 2 * Pn), F32)
            s5im0 = jnp.zeros((G, B, 2 * Pn), F32)
        xt = _s5_proj_t(x2, mod, lw["w_ut"], B, T)
        yt, fre, fim = _s5_core_t(xt, s5ops["col"], s5re0, s5im0, R, B)
        ys = _s5_rows(yt, B, T, D)
    else:
        ucol = 3 * D
        u = P[:, ucol:ucol + D]
        xg = u.reshape(B, R, S, G, S5_GROUP).transpose(3, 1, 0, 2, 4).reshape(G, R * B, S * S5_GROUP)
        yg, fre, fim = _s5_core(xg, s5ops["row"], s5re0, s5im0, R, B)
        ys = yg.reshape(G, R, B, S, S5_GROUP).transpose(2, 1, 3, 0, 4).reshape(N, D)
    s5re_fin = fre.reshape(G, B, 2, Pn).transpose(1, 2, 0, 3)
    s5im_fin = fim.reshape(G, B, 2, Pn).transpose(1, 2, 0, 3)
    P3 = P.reshape(B, T, P.shape[1])
    hf, lf = _lru_sweep(P3, lw["lru"], lru0, None, False, B, T, D)
    ylru, lb = _lru_sweep(P3, lw["lru"], lru0, hf, True, B, T, D)
    lru_fin = jnp.stack([lf, lb], axis=1)
    x1 = _merge(x2, mod, o_gla, P, ys, ylru.reshape(N, -1), lw, T, alpha)
    x3 = _mlp(x1, mod, lw, T, alpha)
    return x3, (gla_fin, s5re_fin, s5im_fin, lru_fin)


def kernel(x_prompt, x_sample, state_gla, state_s5_re, state_s5_im, state_lru, c, c_ctx, w_ada, b_ada, w_in, gla_w_gate, gla_b_gate, gla_norm_g, s5_lam_re, s5_lam_im, s5_log_step, s5_b_re, s5_b_im, s5_c_re, s5_c_im, s5_d, s5_w_glu, lru_conv_w, lru_conv_b, lru_w_a, lru_b_a, lru_w_i, lru_b_i, lru_lam, w_br_gla, w_br_s5, w_br_lru, w_out, ln1_g, ln1_b, ln2_g, ln2_b, w_mlp_in, w_mlp_out):
    Bp, Tp, D = x_prompt.shape
    Bs, Ts, _ = x_sample.shape
    L = w_in.shape[0]
    alpha = (2.0 * L) ** 0.25

    n_rows = -(-(Bs + 1) // SUBLANES) * SUBLANES
    cc = jnp.concatenate([c, c_ctx[None], jnp.zeros((n_rows - Bs - 1, D), F32)], axis=0)
    mod = _ada_mod(cc, w_ada, b_ada)
    s5ops_all = _s5_operators(s5_lam_re, s5_lam_im, s5_log_step, s5_b_re, s5_b_im, s5_c_re, s5_c_im)

    xp = x_prompt.reshape(Bp * Tp, D)
    xs = _add_pos(x_sample, _grid_pos_table(Ts, D)).reshape(Bs * Ts, D)
    fins = []
    for l in range(L):
        lw = _pack_layer_weights(l, D, w_in, gla_w_gate, gla_b_gate, gla_norm_g, s5_d, s5_w_glu,
                                 lru_conv_w, lru_conv_b, lru_w_a, lru_b_a, lru_w_i, lru_b_i, lru_lam,
                                 w_br_gla, w_br_s5, w_br_lru, w_out, ln1_g, ln1_b, ln2_g, ln2_b,
                                 w_mlp_in, w_mlp_out)
        s5ops = {k: tuple(a[l] for a in v) for k, v in s5ops_all.items()}
        mod_ctx = mod[l, Bs:Bs + 1].reshape(1, 1, 6 * D)
        mod_lat = mod[l, :Bs].reshape(Bs, 1, 6 * D)
        xp, fin = _trunk_layer(xp, mod_ctx, lw, s5ops, None, Bp, Tp, D, alpha)
        fins.append(fin)
        cache = (state_gla[:, l], state_s5_re[:, l], state_s5_im[:, l], state_lru[:, l])
        xs, _ = _trunk_layer(xs, mod_lat, lw, s5ops, cache, Bs, Ts, D, alpha)
    sdt = x_prompt.dtype
    new_states = tuple(jnp.stack([f[i] for f in fins], axis=1).astype(sdt) for i in range(4))
    return (xp.reshape(Bp, Tp, D), xs.reshape(Bs, Ts, D)) + new_states
```

```python
import functools
import math

import jax
import jax.numpy as jnp
from jax import lax
from jax.experimental import pallas as pl
from jax.experimental.pallas import tpu as pltpu

F32 = jnp.float32
BF16 = jnp.bfloat16
HIGHEST = lax.Precision.HIGHEST

LANES = 128
SUBLANES = 8
VMEM_LIMIT = 56 * 1024 * 1024

GRID_W = 64
GLA_HEADS = 4
GATE_RANK = 16
GLA_TAU = 16.0
GLA_CHUNK = 64
GLA_BLOCK = 256
S5_GROUP = 16
S5_STATE = 64
S5_SUB = 16
LRU_BLOCK = 128
LRU_CONV = 4
LRU_C = 8.0
LN_EPS = 1e-5
F32_TINY = 1.1754944e-38
SCAN_UNROLL = 8


def _cparams(*sem):
    return pltpu.CompilerParams(dimension_semantics=sem, vmem_limit_bytes=VMEM_LIMIT)


def _dot(a, b):
    return jnp.dot(a.astype(BF16), b.astype(BF16), preferred_element_type=F32)


def _dot_nt(a, b):
    return lax.dot_general(a.astype(BF16), b.astype(BF16), (((1,), (1,)), ((), ())),
                           preferred_element_type=F32)


def _dot_tn(a, b):
    return lax.dot_general(a.astype(BF16), b.astype(BF16), (((0,), (0,)), ((), ())),
                           preferred_element_type=F32)


def _dot_f32(a, b):
    return jnp.dot(a, b, precision=HIGHEST, preferred_element_type=F32)


def _layer_norm(z, g, b):
    mu = jnp.mean(z, axis=-1, keepdims=True)
    zc = z - mu
    var = jnp.mean(zc * zc, axis=-1, keepdims=True)
    return zc * lax.rsqrt(var + LN_EPS) * g + b


def _sigmoid(x):
    return 0.5 * jnp.tanh(0.5 * x) + 0.5


def _silu(x):
    return x * _sigmoid(x)


def _ada_kernel(cc_ref, w_ref, b_ref, o_ref):
    s = _silu(cc_ref[...])
    o_ref[0] = _dot_f32(s, w_ref[0]) + b_ref[0]


def _ada_mod(cc, w_ada, b_ada):
    L, D, D6 = w_ada.shape
    R = cc.shape[0]
    tn = D6 // 4
    return pl.pallas_call(
        _ada_kernel,
        grid=(L, D6 // tn),
        in_specs=[pl.BlockSpec((R, D), lambda l, j: (0, 0)),
                  pl.BlockSpec((1, D, tn), lambda l, j: (l, 0, j)),
                  pl.BlockSpec((1, 1, tn), lambda l, j: (l, 0, j))],
        out_specs=pl.BlockSpec((1, R, tn), lambda l, j: (l, 0, j)),
        out_shape=jax.ShapeDtypeStruct((L, R, D6), F32),
        compiler_params=_cparams("parallel", "parallel"),
        name="ada_mod",
    )(cc, w_ada, b_ada.reshape(L, 1, D6))


def _addpos_kernel(x_ref, p_ref, o_ref):
    o_ref[0] = x_ref[0] + p_ref[...]


def _add_pos(x, pos):
    B, T, D = x.shape
    tt = min(T, 512)
    return pl.pallas_call(
        _addpos_kernel,
        grid=(T // tt, B),
        in_specs=[pl.BlockSpec((1, tt, D), lambda t, b: (b, t, 0)),
                  pl.BlockSpec((tt, D), lambda t, b: (t, 0))],
        out_specs=pl.BlockSpec((1, tt, D), lambda t, b: (b, t, 0)),
        out_shape=jax.ShapeDtypeStruct((B, T, D), F32),
        compiler_params=_cparams("parallel", "parallel"),
        name="add_pos",
    )(x, pos)


def _grid_pos_table(n_tokens, dim):
    rows = n_tokens // GRID_W
    quarter = dim // 4
    omega = 1.0 / (10000.0 ** (jnp.arange(quarter, dtype=F32) / quarter))
    r = jnp.arange(rows, dtype=F32)[:, None, None] * omega
    cl = jnp.arange(GRID_W, dtype=F32)[None, :, None] * omega
    shape = (rows, GRID_W, quarter)
    emb = jnp.concatenate([jnp.broadcast_to(jnp.sin(r), shape), jnp.broadcast_to(jnp.cos(r), shape),
                           jnp.broadcast_to(jnp.sin(cl), shape), jnp.broadcast_to(jnp.cos(cl), shape)], axis=-1)
    return emb.reshape(rows * GRID_W, dim)


def _inproj_kernel(x_ref, mod_ref, w_ref, wglr_ref, o_ref, glr_ref, h_ref, *, D):
    @pl.when(pl.program_id(1) == 0)
    def _():
        shift = mod_ref[0, :, 0:D]
        scale = mod_ref[0, :, D:2 * D]
        h = (x_ref[...] * (1.0 + scale) + shift).astype(BF16)
        h_ref[...] = h
        glr_ref[...] = jnp.dot(h, wglr_ref[...], preferred_element_type=F32)

    o_ref[...] = jnp.dot(h_ref[...], w_ref[...], preferred_element_type=F32)


def _row_tile(N, T, per_batch_mod, want):
    tm = min(want, T if per_batch_mod else N)
    while N % tm or (per_batch_mod and T % tm):
        tm //= 2
    return tm


def _in_proj(x2, mod, w_pack, w_glr, T):
    N, D = x2.shape
    NC = w_pack.shape[1]
    Bm = mod.shape[0]
    tm = _row_tile(N, T, Bm > 1, 1024)
    tn = D
    per_b = T // tm
    mod_map = (lambda i, j: (i // per_b, 0, 0)) if Bm > 1 else (lambda i, j: (0, 0, 0))
    return pl.pallas_call(
        functools.partial(_inproj_kernel, D=D),
        grid=(N // tm, NC // tn),
        in_specs=[pl.BlockSpec((tm, D), lambda i, j: (i, 0)),
                  pl.BlockSpec((1, 1, 6 * D), mod_map),
                  pl.BlockSpec((D, tn), lambda i, j: (0, j)),
                  pl.BlockSpec((D, LANES), lambda i, j: (0, 0))],
        out_specs=[pl.BlockSpec((tm, tn), lambda i, j: (i, j)),
                   pl.BlockSpec((tm, LANES), lambda i, j: (i, 0))],
        out_shape=[jax.ShapeDtypeStruct((N, NC), F32), jax.ShapeDtypeStruct((N, LANES), F32)],
        scratch_shapes=[pltpu.VMEM((tm, D), BF16)],
        compiler_params=_cparams("parallel", "arbitrary"),
        name="in_proj",
    )(x2, mod, w_pack, w_glr)


def _gla_kernel(*refs, T, DK, DV, has_init):
    if has_init:
        (q_ref, k_ref, v_ref, glr_ref, wg_ref, bg_ref, s0_ref,
         o_ref, sfin_ref, qb_ref, keb_ref, db_ref, sf_ref, sb_ref) = refs
    else:
        (q_ref, k_ref, v_ref, glr_ref, wg_ref, bg_ref,
         o_ref, sfin_ref, qb_ref, keb_ref, db_ref, sf_ref, sb_ref) = refs
    C = GLA_CHUNK
    BLK = GLA_BLOCK
    NS = BLK // C
    nblk = T // BLK
    qscale = DK ** -0.5
    shift = C.bit_length() - 1
    row = lax.broadcasted_iota(jnp.int32, (BLK, BLK), 0)
    col = lax.broadcasted_iota(jnp.int32, (BLK, BLK), 1)
    same = lax.shift_right_logical(row, shift) == lax.shift_right_logical(col, shift)
    lower = jnp.logical_and(same, row >= col)
    upper = jnp.logical_and(same, row <= col)
    tri = jnp.where(lower, 1.0, 0.0).astype(BF16)
    chunk_of_row = lax.shift_right_logical(lax.broadcasted_iota(jnp.int32, (BLK, DK), 0), shift)
    wg = wg_ref[0]
    bg = bg_ref[0]

    def chunk_cols(ke):
        return jnp.concatenate([jnp.where(chunk_of_row == c, ke, 0.0) for c in range(NS)], axis=1).astype(BF16)

    def per_chunk_last(x, r):
        return jnp.concatenate([jnp.broadcast_to(x[c * C + r:c * C + r + 1, :], (C, x.shape[1]))
                                for c in range(NS)], axis=0)

    if has_init:
        sf_ref[...] = s0_ref[0, 0, 0].T
        sb_ref[...] = s0_ref[0, 1, 0].T
    else:
        sf_ref[...] = jnp.zeros((DV, DK), F32)
        sb_ref[...] = jnp.zeros((DV, DK), F32)

    def fwd_body(i, carry):
        r0 = pl.multiple_of(i * BLK, BLK)
        rows = pl.ds(r0, BLK)
        qc = q_ref[rows, :] * qscale
        kc = k_ref[rows, :]
        vc = v_ref[rows, :]
        logits = _dot(glr_ref[rows, :], wg) + bg
        la = (jnp.minimum(logits, 0.0) - jnp.log(1.0 + jnp.exp(-jnp.abs(logits)))) * (1.0 / GLA_TAU)
        p1 = la.astype(BF16)
        r1 = la - p1.astype(F32)
        p2 = r1.astype(BF16)
        p3 = (r1 - p2.astype(F32)).astype(BF16)
        pre = (jnp.dot(tri, p1, preferred_element_type=F32) + jnp.dot(tri, p2, preferred_element_type=F32)
               + jnp.dot(tri, p3, preferred_element_type=F32))
        tot = per_chunk_last(pre, C - 1)
        cum_f = pre[:, :DK]
        last_f = tot[:, :DK]
        cum_b = tot[:, DK:] - pre[:, DK:] + la[:, DK:]
        last_b = tot[:, DK:]
        q_f = qc * jnp.exp(cum_f)
        k_f = kc * jnp.exp(-cum_f)
        ke_f = kc * jnp.exp(last_f - cum_f)
        q_b = qc * jnp.exp(cum_b)
        k_b = kc * jnp.exp(-cum_b)
        ke_b = kc * jnp.exp(last_b - cum_b)
        sc = jnp.where(lower, _dot_nt(q_f, k_f), 0.0) + jnp.where(upper, _dot_nt(q_b, k_b), 0.0)
        o_blk = _dot(sc, vc)
        u_f = jnp.dot(vc.T.astype(BF16), chunk_cols(ke_f), preferred_element_type=F32)
        d_f = jnp.exp(last_f)
        s = sf_ref[...]
        for c in range(NS):
            rc = slice(c * C, (c + 1) * C)
            o_ref[pl.ds(r0 + c * C, C), :] = o_blk[rc] + _dot_nt(q_f[rc], s)
            s = s * d_f[c * C:c * C + 1, :] + u_f[:, c * DK:(c + 1) * DK]
        sf_ref[...] = s
        qb_ref[rows, :] = q_b
        keb_ref[rows, :] = ke_b
        d_b = jnp.exp(last_b)
        db_ref[pl.ds(i, 1), :] = jnp.concatenate([d_b[c * C:c * C + 1, :] for c in range(NS)], axis=1)
        return carry

    lax.fori_loop(0, nblk, fwd_body, 0, unroll=min(4, nblk))

    def bwd_body(j, carry):
        i = nblk - 1 - j
        r0 = pl.multiple_of(i * BLK, BLK)
        rows = pl.ds(r0, BLK)
        q_b = qb_ref[rows, :]
        u_b = jnp.dot(v_ref[rows, :].T.astype(BF16), chunk_cols(keb_ref[rows, :]), preferred_element_type=F32)
        d_b = db_ref[pl.ds(i, 1), :]
        s = sb_ref[...]
        for c in reversed(range(NS)):
            rc = pl.ds(r0 + c * C, C)
            o_ref[rc, :] = o_ref[rc, :] + _dot_nt(q_b[c * C:(c + 1) * C], s)
            s = s * d_b[:, c * DK:(c + 1) * DK] + u_b[:, c * DK:(c + 1) * DK]
        sb_ref[...] = s
        return carry

    lax.fori_loop(0, nblk, bwd_body, 0, unroll=min(4, nblk))

    sfin_ref[0, 0, 0] = sf_ref[...].T
    sfin_ref[0, 1, 0] = sb_ref[...].T


def _gla_core(P, glr, wg, bg, s0, B, T, D):
    H = GLA_HEADS
    DK = D // 2 // H
    DV = D // H
    N = B * T
    has_init = s0 is not None
    kcol = (D // 2) // DK
    vcol = D // DV
    in_specs = [pl.BlockSpec((T, DK), lambda b, h: (b, h)),
                pl.BlockSpec((T, DK), lambda b, h: (b, kcol + h)),
                pl.BlockSpec((T, DV), lambda b, h: (b, vcol + h)),
                pl.BlockSpec((T, LANES), lambda b, h: (b, 0)),
                pl.BlockSpec((1, LANES, 2 * DK), lambda b, h: (h, 0, 0)),
                pl.BlockSpec((1, 1, 2 * DK), lambda b, h: (h, 0, 0))]
    args = [P, P, P, glr, wg, bg]
    if has_init:
        in_specs.append(pl.BlockSpec((1, 2, 1, DK, DV), lambda b, h: (b, 0, h, 0, 0)))
        args.append(s0)
    o, sfin = pl.pallas_call(
        functools.partial(_gla_kernel, T=T, DK=DK, DV=DV, has_init=has_init),
        grid=(B, H),
        in_specs=in_specs,
        out_specs=[pl.BlockSpec((T, DV), lambda b, h: (b, h)),
                   pl.BlockSpec((1, 2, 1, DK, DV), lambda b, h: (b, 0, h, 0, 0))],
        out_shape=[jax.ShapeDtypeStruct((N, D), F32),
                   jax.ShapeDtypeStruct((B, 2, H, DK, DV), F32)],
        scratch_shapes=[pltpu.VMEM((T, DK), F32), pltpu.VMEM((T, DK), F32),
                        pltpu.VMEM((T // GLA_BLOCK, (GLA_BLOCK // GLA_CHUNK) * DK), F32),
                        pltpu.VMEM((DV, DK), F32), pltpu.VMEM((DV, DK), F32)],
        compiler_params=_cparams("parallel", "parallel"),
        name="gla_core",
    )(*args)
    return o, sfin


def _s5prep_kernel(lr_ref, li_ref, ls_ref, btr_ref, bti_ref, cr_ref, ci_ref,
                   kt_ref, wre_ref, wim_ref, cpre_ref, cpimn_ref, pre_ref, pim_ref):
    I = S5_GROUP
    lr = lr_ref[0, 0]
    li = li_ref[0, 0]
    step = jnp.exp(ls_ref[0, 0])
    mag = jnp.exp(lr * step)
    ang = li * step
    a_re = mag * jnp.cos(ang)
    a_im = mag * jnp.sin(ang)
    den = lr * lr + li * li
    nr = a_re - 1.0
    f_re = (nr * lr + a_im * li) / den
    f_im = (a_im * lr - nr * li) / den
    btr = btr_ref[0]
    bti = bti_ref[0]
    bb_re = f_re[:, None, :] * btr - f_im[:, None, :] * bti
    bb_im = f_re[:, None, :] * bti + f_im[:, None, :] * btr
    c_re = cr_ref[0]
    c_im = ci_ref[0]
    p_re = jnp.ones_like(lr)
    p_im = jnp.zeros_like(lr)
    batched_nt = (((2,), (2,)), ((0,), (0,)))
    for e in range(S5_SUB + 1):
        pr = p_re[:, None, :]
        pi = p_im[:, None, :]
        cp_re = c_re * pr - c_im * pi
        cp_im = c_re * pi + c_im * pr
        cpre_ref[0, 0, :, e * I:(e + 1) * I, :] = cp_re
        cpimn_ref[0, 0, :, e * I:(e + 1) * I, :] = -cp_im
        if e < S5_SUB:
            wre_ref[0, 0, :, e * I:(e + 1) * I, :] = pr * bb_re - pi * bb_im
            wim_ref[0, 0, :, e * I:(e + 1) * I, :] = pr * bb_im + pi * bb_re
            kt_ref[0, 0, :, e * I:(e + 1) * I, :] = (
                lax.dot_general(cp_re, bb_re, batched_nt, precision=HIGHEST, preferred_element_type=F32)
                - lax.dot_general(cp_im, bb_im, batched_nt, precision=HIGHEST, preferred_element_type=F32))
            p_re, p_im = p_re * a_re - p_im * a_im, p_re * a_im + p_im * a_re
    pre_ref[0, 0] = p_re
    pim_ref[0, 0] = p_im


def _s5_operators(lam_re, lam_im, log_step, b_re, b_im, c_re, c_im):
    L, _, G, Pn = lam_re.shape
    I = S5_GROUP
    S = S5_SUB
    ls = jnp.broadcast_to(log_step[..., None], lam_re.shape)
    btr = jnp.swapaxes(b_re, -1, -2)
    bti = jnp.swapaxes(b_im, -1, -2)
    Gb = SUBLANES
    lam_spec = pl.BlockSpec((1, 1, Gb, Pn), lambda l, d, g: (l, d, g, 0))
    gip_spec = pl.BlockSpec((1, Gb, I, Pn), lambda l, d, g: (l, g, 0, 0))

    def out_spec(rows, last):
        return pl.BlockSpec((1, 1, Gb, rows, last), lambda l, d, g: (l, d, g, 0, 0))

    kt, wre, wim, cpre, cpimn, pre, pim = pl.pallas_call(
        _s5prep_kernel,
        grid=(L, 2, G // Gb),
        in_specs=[lam_spec, lam_spec, lam_spec, gip_spec, gip_spec, gip_spec, gip_spec],
        out_specs=[out_spec(S * I, I), out_spec(S * I, Pn), out_spec(S * I, Pn),
                   out_spec((S + 1) * I, Pn), out_spec((S + 1) * I, Pn), lam_spec, lam_spec],
        out_shape=[jax.ShapeDtypeStruct((L, 2, G, S * I, I), F32),
                   jax.ShapeDtypeStruct((L, 2, G, S * I, Pn), F32),
                   jax.ShapeDtypeStruct((L, 2, G, S * I, Pn), F32),
                   jax.ShapeDtypeStruct((L, 2, G, (S + 1) * I, Pn), F32),
                   jax.ShapeDtypeStruct((L, 2, G, (S + 1) * I, Pn), F32),
                   jax.ShapeDtypeStruct((L, 2, G, Pn), F32),
                   jax.ShapeDtypeStruct((L, 2, G, Pn), F32)],
        compiler_params=_cparams("parallel", "parallel", "parallel"),
        name="s5_prep",
    )(lam_re, lam_im, ls, btr, bti, c_re, c_im)

    kt = kt.reshape(L, 2, G, S, I, I)
    s_idx = jnp.arange(S)[:, None]
    t_idx = jnp.arange(S)[None, :]

    def toeplitz(k, lag, valid):
        m = k[:, :, jnp.clip(lag, 0, S - 1)]
        m = jnp.where(valid[None, None, :, :, None, None], m, 0.0)
        return m.transpose(0, 1, 2, 5, 3, 4).reshape(L, G, S * I, S * I)

    mf = toeplitz(kt[:, 0], t_idx - s_idx, t_idx >= s_idx)
    mb = toeplitz(kt[:, 1], s_idx - t_idx, s_idx >= t_idx)

    def by_pos(w, reverse):
        w = w.reshape(L, G, S, I, Pn)
        if reverse:
            w = w[:, :, ::-1]
        return w.reshape(L, G, S * I, Pn)

    wall = jnp.concatenate([by_pos(wre[:, 0], True), by_pos(wre[:, 1], False),
                            by_pos(wim[:, 0], True), by_pos(wim[:, 1], False)], axis=-1)

    def readout(cp):
        cp = cp.reshape(L, 2, G, S + 1, I, Pn)
        f = cp[:, 0, :, 1:]
        b = cp[:, 1, :, 1:][:, :, ::-1]
        f = f.reshape(L, G, S * I, Pn).swapaxes(-1, -2)
        b = b.reshape(L, G, S * I, Pn).swapaxes(-1, -2)
        return jnp.concatenate([f, b], axis=2)

    vre = readout(cpre)
    vim = readout(cpimn)
    are = jnp.concatenate([pre[:, 0], pre[:, 1]], axis=-1)[:, :, None, :]
    aim = jnp.concatenate([pim[:, 0], pim[:, 1]], axis=-1)[:, :, None, :]
    t = lambda a: jnp.swapaxes(a, -1, -2)
    return t(mf), t(mb), t(wall), t(vre), t(vim), are, aim


def _s5projt_kernel(x_ref, mod_ref, wt_ref, o_ref, *, D, R):
    Bm = mod_ref.shape[0]
    span = x_ref.shape[0] // Bm
    parts = []
    for b in range(Bm):
        shift = mod_ref[b, :, 0:D]
        scale = mod_ref[b, :, D:2 * D]
        parts.append((x_ref[b * span:(b + 1) * span, :] * (1.0 + scale) + shift).astype(BF16))
    h = parts[0] if Bm == 1 else jnp.concatenate(parts, axis=0)
    ut = lax.dot_general(wt_ref[...], h, (((1,), (1,)), ((), ())), preferred_element_type=F32)
    o_ref[...] = ut.reshape(o_ref.shape).astype(o_ref.dtype)


def _s5_proj_t(x2, mod, w_ut, B, T):
    N, D = x2.shape
    S = S5_SUB
    R = T // S
    G = D // S5_GROUP
    Bm = mod.shape[0]
    xv = x2.reshape(B * R, S * D)
    per_batch = Bm > 1 and R % LANES == 0
    rows = R if per_batch else B * R
    nrb = B * R // rows
    if per_batch:
        mod_spec = pl.BlockSpec((1, 1, 6 * D), lambda i, s: (i, 0, 0))
    else:
        mod_spec = pl.BlockSpec((Bm, 1, 6 * D), lambda i, s: (0, 0, 0))
    return pl.pallas_call(
        functools.partial(_s5projt_kernel, D=D, R=R),
        grid=(nrb, S),
        in_specs=[pl.BlockSpec((rows, D), lambda i, s: (i, s)),
                  mod_spec,
                  pl.BlockSpec((D, D), lambda i, s: (0, 0))],
        out_specs=pl.BlockSpec((G, S5_GROUP, rows), lambda i, s: (0, s, i)),
        out_shape=jax.ShapeDtypeStruct((G, S * S5_GROUP, B * R), BF16),
        compiler_params=_cparams("parallel", "parallel"),
        name="s5_proj_t",
    )(xv, mod, w_ut)


def _s5t_kernel(xt_ref, mft_ref, mbt_ref, wt_ref, vret_ref, vimt_ref, are_ref, aim_ref, s0re_ref, s0im_ref,
                yt_ref, fre_ref, fim_ref, zt_ref, zre_ref, zim_ref, hre_ref, him_ref, *, R, B, chunk_major):
    Pn = S5_STATE
    xt = xt_ref[0]
    mt = (mft_ref[0] + mbt_ref[0]).astype(BF16)
    yt_ref[0] = jnp.dot(mt, xt, preferred_element_type=F32)
    zt_ref[...] = jnp.dot(wt_ref[0].astype(BF16), xt, preferred_element_type=F32)
    if chunk_major:
        for b in range(B):
            zb = zt_ref[:, b * R:(b + 1) * R].T
            zre_ref[pl.ds(b, R, stride=B), :] = zb[:, :2 * Pn]
            zim_ref[pl.ds(b, R, stride=B), :] = zb[:, 2 * Pn:]
        step_rows = lambda r: pl.ds(pl.multiple_of(r * B, B), B)
    else:
        z = zt_ref[...].T
        zre_ref[...] = z[:, :2 * Pn]
        zim_ref[...] = z[:, 2 * Pn:]
        step_rows = lambda r: pl.ds(r, B, stride=R)
    a_re = are_ref[0]
    a_im = aim_ref[0]
    is_fwd = lax.broadcasted_iota(jnp.int32, (B, 2 * Pn), 1) < Pn
    h0 = (s0re_ref[0], s0im_ref[0])

    def fwd_step(r, h):
        hr, hi = h
        rows = step_rows(r)
        hre_ref[rows, :] = hr
        him_ref[rows, :] = hi
        return (hr * a_re - hi * a_im + zre_ref[rows, :], hr * a_im + hi * a_re + zim_ref[rows, :])

    hf = lax.fori_loop(0, R, fwd_step, h0, unroll=SCAN_UNROLL)

    def bwd_step(j, h):
        hr, hi = h
        rows = step_rows(R - 1 - j)
        hre_ref[rows, :] = jnp.where(is_fwd, hre_ref[rows, :], hr)
        him_ref[rows, :] = jnp.where(is_fwd, him_ref[rows, :], hi)
        return (hr * a_re - hi * a_im + zre_ref[rows, :], hr * a_im + hi * a_re + zim_ref[rows, :])

    hb = lax.fori_loop(0, R, bwd_step, h0, unroll=SCAN_UNROLL)
    fre_ref[0] = jnp.where(is_fwd, hf[0], hb[0])
    fim_ref[0] = jnp.where(is_fwd, hf[1], hb[1])
    vret = vret_ref[0]
    vimt = vimt_ref[0]
    if chunk_major:
        for b in range(B):
            hre_b = hre_ref[pl.ds(b, R, stride=B), :]
            him_b = him_ref[pl.ds(b, R, stride=B), :]
            cols = slice(b * R, (b + 1) * R)
            yt_ref[0, :, cols] = yt_ref[0, :, cols] + _dot_nt(vret, hre_b) + _dot_nt(vimt, him_b)
    else:
        yt_ref[0] = yt_ref[0] + _dot_nt(vret, hre_ref[...]) + _dot_nt(vimt, him_ref[...])


def _s5_core_t(xt, ops, s0re, s0im, R, B):
    G, K, BR = xt.shape
    mft, mbt, wt, vret, vimt, are, aim = ops
    P2 = 2 * S5_STATE
    chunk_major = B == SUBLANES and R % LANES == 0
    gspec = lambda shape: pl.BlockSpec((1,) + shape, lambda g: (g, 0, 0))
    return pl.pallas_call(
        functools.partial(_s5t_kernel, R=R, B=B, chunk_major=chunk_major),
        grid=(G,),
        in_specs=[gspec((K, BR)), gspec((K, K)), gspec((K, K)), gspec((K, K)),
                  gspec((K, P2)), gspec((K, P2)), gspec((1, P2)), gspec((1, P2)),
                  gspec((B, P2)), gspec((B, P2))],
        out_specs=[gspec((K, BR)), gspec((B, P2)), gspec((B, P2))],
        out_shape=[jax.ShapeDtypeStruct((G, K, BR), F32),
                   jax.ShapeDtypeStruct((G, B, P2), F32),
                   jax.ShapeDtypeStruct((G, B, P2), F32)],
        scratch_shapes=[pltpu.VMEM((K, BR), F32)] + [pltpu.VMEM((BR, P2), F32) for _ in range(4)],
        compiler_params=_cparams("parallel"),
        name="s5_core_t",
    )(xt, mft, mbt, wt, vret, vimt, are, aim, s0re, s0im)


def _s5rows_kernel(yt_ref, o_ref):
    I = S5_GROUP
    nc = yt_ref.shape[-1]
    for t in range(S5_SUB):
        blk = yt_ref[:, t * I:(t + 1) * I, :]
        o_ref[pl.ds(t, nc, stride=S5_SUB), :] = blk.reshape(LANES, nc).T


def _s5_rows(yt, B, T, D):
    G = D // S5_GROUP
    BR = B * T // S5_SUB
    gpt = LANES // S5_GROUP
    nc = 2 * LANES if BR % (2 * LANES) == 0 else LANES
    return pl.pallas_call(
        _s5rows_kernel,
        grid=(BR // nc, G // gpt),
        in_specs=[pl.BlockSpec((gpt, S5_SUB * S5_GROUP, nc), lambda i, j: (j, 0, i))],
        out_specs=pl.BlockSpec((nc * S5_SUB, LANES), lambda i, j: (i, j)),
        out_shape=jax.ShapeDtypeStruct((B * T, D), F32),
        compiler_params=_cparams("parallel", "parallel"),
        name="s5_rows",
    )(yt)


def _lru_kernel(*refs, Tt, CB, reverse, has_init):
    refs = list(refs)
    lx_ref, prev_ref, next_ref, cw_ref, cb_ref, wg_ref, bgate_ref, lam_ref = refs[:8]
    pos = 8
    h0_ref = None
    if has_init:
        h0_ref = refs[pos]
        pos += 1
    if reverse:
        hf_ref, ly_ref = refs[pos:pos + 2]
        pos += 2
    out_ref, fin_ref, x_scr, a_scr, b_scr, h_scr, carry_ref = refs[pos:pos + 7]
    NBk = CB // LRU_BLOCK
    k = pl.program_id(2)
    nk = pl.num_programs(2)
    first_tile = k == 0

    @pl.when(first_tile)
    def _():
        if has_init:
            carry_ref[...] = h0_ref[...]
        else:
            carry_ref[...] = jnp.zeros((SUBLANES, CB), F32)

    tpos = (nk - 1 - k) if reverse else k
    has_prev = tpos > 0
    has_next = tpos < nk - 1
    cw = cw_ref[0]
    cbias = cb_ref[0]
    bgate = bgate_ref[0]
    lam = lam_ref[0]
    half_cfac = (-0.5 * LRU_C) * (jnp.maximum(-lam, 0.0) + jnp.log1p(jnp.exp(-jnp.abs(lam))))

    left = LRU_CONV // 2
    for b in range(SUBLANES):
        cur = lx_ref[b]
        pv = jnp.where(has_prev, prev_ref[b], 0.0)
        nx = jnp.where(has_next, next_ref[b], 0.0)
        for c in range(NBk):
            sl = slice(c * LRU_BLOCK, (c + 1) * LRU_BLOCK)
            x_scr[c, pl.ds(left * SUBLANES + b, Tt, stride=SUBLANES), :] = cur[:, sl]
            for q in range(left):
                x_scr[c, pl.ds(q * SUBLANES + b, 1), :] = pv[SUBLANES - left + q:SUBLANES - left + q + 1, sl]
            for q in range(LRU_CONV - 1 - left):
                x_scr[c, pl.ds((Tt + left + q) * SUBLANES + b, 1), :] = nx[q:q + 1, sl]

    RC = min(Tt * SUBLANES, 512)
    for c in range(NBk):
        sl = slice(c * LRU_BLOCK, (c + 1) * LRU_BLOCK)
        bias_c = jnp.concatenate([bgate[:, sl], bgate[:, CB + c * LRU_BLOCK: CB + (c + 1) * LRU_BLOCK]], axis=1)
        for r0 in range(0, Tt * SUBLANES, RC):
            xc = cbias[:, sl] + sum(cw[j:j + 1, sl] * x_scr[c, pl.ds(r0 + j * SUBLANES, RC), :]
                                    for j in range(LRU_CONV))
            th = jnp.tanh(_dot(xc, wg_ref[0, c]) + bias_c)
            log_a = half_cfac[:, sl] * th[:, :LRU_BLOCK] + half_cfac[:, sl]
            a = jnp.exp(log_a)
            om = -jnp.tanh(log_a) * (1.0 + a * a)
            root = om * lax.rsqrt(jnp.maximum(om, F32_TINY))
            bt = root * ((0.5 * th[:, LRU_BLOCK:] + 0.5) * xc)
            a_scr[c, pl.ds(r0, RC), :] = a
            b_scr[c, pl.ds(r0, RC), :] = bt

    def step(s, h):
        t = (Tt - 1 - s) if reverse else s
        rows = pl.ds(pl.multiple_of(t * SUBLANES, SUBLANES), SUBLANES)
        new = []
        for c in range(NBk):
            hc = a_scr[c, rows, :] * h[c] + b_scr[c, rows, :]
            h_scr[c, rows, :] = hc
            new.append(hc)
        return tuple(new)

    h_init = tuple(carry_ref[:, c * LRU_BLOCK:(c + 1) * LRU_BLOCK] for c in range(NBk))
    h_last = lax.fori_loop(0, Tt, step, h_init, unroll=SCAN_UNROLL)
    for c in range(NBk):
        carry_ref[:, c * LRU_BLOCK:(c + 1) * LRU_BLOCK] = h_last[c]
        fin_ref[:, c * LRU_BLOCK:(c + 1) * LRU_BLOCK] = h_last[c]

    for b in range(SUBLANES):
        hb = jnp.concatenate([h_scr[c, pl.ds(b, Tt, stride=SUBLANES), :] for c in range(NBk)], axis=1)
        if reverse:
            out_ref[b] = (hf_ref[b] + hb) * jax.nn.gelu(ly_ref[b])
        else:
            out_ref[b] = hb


def _lru_sweep(P3, lw, h0, hf, reverse, B, T, D):
    W = 3 * D // 2
    CB = 2 * LRU_BLOCK
    Tt = min(T, 256)
    nk = T // Tt
    nb8 = Tt // SUBLANES
    lxcol = (D // 2 + D // 2 + D + D + D) // CB
    lycol = lxcol + W // CB
    d = 1 if reverse else 0
    has_init = h0 is not None

    def tmap(k):
        return (nk - 1 - k) if reverse else k

    in_specs = [
        pl.BlockSpec((SUBLANES, Tt, CB), lambda g, j, k: (g, tmap(k), lxcol + j)),
        pl.BlockSpec((SUBLANES, SUBLANES, CB),
                     lambda g, j, k: (g, jnp.maximum(tmap(k) * nb8 - 1, 0), lxcol + j)),
        pl.BlockSpec((SUBLANES, SUBLANES, CB),
                     lambda g, j, k: (g, jnp.minimum((tmap(k) + 1) * nb8, T // SUBLANES - 1), lxcol + j)),
        pl.BlockSpec((1, LRU_CONV, CB), lambda g, j, k: (j, 0, 0)),
        pl.BlockSpec((1, 1, CB), lambda g, j, k: (j, 0, 0)),
        pl.BlockSpec((1, CB // LRU_BLOCK, LRU_BLOCK, 2 * LRU_BLOCK), lambda g, j, k: (j, 0, 0, 0)),
        pl.BlockSpec((1, 1, 2 * CB), lambda g, j, k: (j, 0, 0)),
        pl.BlockSpec((1, 1, CB), lambda g, j, k: (j, 0, 0)),
    ]
    args = [P3, P3, P3, lw["conv_w"], lw["conv_b"], lw["wg"][d], lw["bg"][d], lw["lam"][d]]
    if has_init:
        in_specs.append(pl.BlockSpec((SUBLANES, CB), lambda g, j, k: (g, j)))
        args.append(h0[:, d])
    if reverse:
        in_specs.append(pl.BlockSpec((SUBLANES, Tt, CB), lambda g, j, k: (g, tmap(k), j)))
        in_specs.append(pl.BlockSpec((SUBLANES, Tt, CB), lambda g, j, k: (g, tmap(k), lycol + j)))
        args += [hf, P3]
    out, fin = pl.pallas_call(
        functools.partial(_lru_kernel, Tt=Tt, CB=CB, reverse=reverse, has_init=has_init),
        grid=(B // SUBLANES, W // CB, nk),
        in_specs=in_specs,
        out_specs=[pl.BlockSpec((SUBLANES, Tt, CB), lambda g, j, k: (g, tmap(k), j)),
                   pl.BlockSpec((SUBLANES, CB), lambda g, j, k: (g, j))],
        out_shape=[jax.ShapeDtypeStruct((B, T, W), F32), jax.ShapeDtypeStruct((B, W), F32)],
        scratch_shapes=[pltpu.VMEM((CB // LRU_BLOCK, (Tt + LRU_CONV) * SUBLANES, LRU_BLOCK), F32)]
        + [pltpu.VMEM((CB // LRU_BLOCK, Tt * SUBLANES, LRU_BLOCK), F32) for _ in range(3)]
        + [pltpu.VMEM((SUBLANES, CB), F32)],
        compiler_params=_cparams("parallel", "parallel", "arbitrary"),
        name="lru_bwd" if reverse else "lru_fwd",
    )(*args)
    return out, fin


def _merge_kernel(x_ref, mod_ref, o_ref, r_ref, ys_ref, u_ref, yl_ref, gg_ref, gs_ref, gl_ref,
                  gn_ref, sd_ref, wglu_ref, wbg_ref, wbs_ref, wbl_ref, wo_ref, lg_ref, lb_ref,
                  out_ref, *, D, alpha):
    DV = D // GLA_HEADS
    gn = gn_ref[...]
    parts = []
    for h in range(GLA_HEADS):
        sl = slice(h * DV, (h + 1) * DV)
        o = o_ref[:, sl]
        mu = jnp.mean(o, axis=-1, keepdims=True)
        oc = o - mu
        var = jnp.mean(oc * oc, axis=-1, keepdims=True)
        parts.append(oc * lax.rsqrt(var + LN_EPS) * gn[:, sl] * _silu(r_ref[:, sl]))
    y_gla = jnp.concatenate(parts, axis=1)
    ys = jax.nn.gelu(ys_ref[...] + sd_ref[...] * u_ref[...])
    y_s5 = ys * _sigmoid(_dot(ys, wglu_ref[...]))
    merged = (_sigmoid(gg_ref[...]) * _dot(y_gla, wbg_ref[...])
              + _sigmoid(gs_ref[...]) * _dot(y_s5, wbs_ref[...])
              + _sigmoid(gl_ref[...]) * _dot(yl_ref[...], wbl_ref[...]))
    mix = _dot(merged, wo_ref[...])
    gate1 = mod_ref[0, :, 2 * D:3 * D]
    out_ref[...] = _layer_norm(alpha * x_ref[...] + gate1 * mix, lg_ref[...], lb_ref[...])


def _merge(x2, mod, o_gla, P, ys, ylru, lw, T, alpha):
    N, D = x2.shape
    W = 3 * D // 2
    Bm = mod.shape[0]
    rcol, ucol, gcol = 2, 3, 7
    tm = _row_tile(N, T, Bm > 1, 256)
    per_b = T // tm
    mod_map = (lambda i: (i // per_b, 0, 0)) if Bm > 1 else (lambda i: (0, 0, 0))
    row = lambda width, cb=0: pl.BlockSpec((tm, width), lambda i: (i, cb))
    full = lambda a: pl.BlockSpec(a.shape, lambda i: (0,) * a.ndim)
    weights = [lw["gnorm"], lw["s5_d"], lw["w_glu"], lw["w_br_gla"], lw["w_br_s5"], lw["w_br_lru"],
               lw["w_out"], lw["ln1_g"], lw["ln1_b"]]
    return pl.pallas_call(
        functools.partial(_merge_kernel, D=D, alpha=alpha),
        grid=(N // tm,),
        in_specs=[row(D), pl.BlockSpec((1, 1, 6 * D), mod_map), row(D), row(D, rcol), row(D),
                  row(D, ucol), row(W), row(D, gcol), row(D, gcol + 1), row(D, gcol + 2)]
        + [full(w) for w in weights],
        out_specs=row(D),
        out_shape=jax.ShapeDtypeStruct((N, D), F32),
        compiler_params=_cparams("parallel"),
        name="merge",
    )(x2, mod, o_gla, P, ys, P, ylru, P, P, P, *weights)


def _mlp_kernel(x_ref, mod_ref, w1_ref, w2_ref, lg_ref, lb_ref, out_ref, h_ref, acc_ref, *, D, alpha):
    j = pl.program_id(1)

    @pl.when(j == 0)
    def _():
        shift = mod_ref[0, :, 3 * D:4 * D]
        scale = mod_ref[0, :, 4 * D:5 * D]
        h_ref[...] = (x_ref[...] * (1.0 + scale) + shift).astype(BF16)
        acc_ref[...] = jnp.zeros_like(acc_ref)

    hid = jnp.dot(h_ref[...], w1_ref[...], preferred_element_type=F32)
    hid = jnp.square(jnp.maximum(hid, 0.0))
    acc_ref[...] += _dot(hid, w2_ref[...])

    @pl.when(j == pl.num_programs(1) - 1)
    def _():
        gate2 = mod_ref[0, :, 5 * D:6 * D]
        out_ref[...] = _layer_norm(alpha * x_ref[...] + gate2 * acc_ref[...], lg_ref[...], lb_ref[...])


def _mlp(x2, mod, lw, T, alpha):
    N, D = x2.shape
    HID = lw["w_mlp_in"].shape[1]
    Bm = mod.shape[0]
    tm = _row_tile(N, T, Bm > 1, 1024)
    th = min(HID, 1024)
    per_b = T // tm
    mod_map = (lambda i, j: (i // per_b, 0, 0)) if Bm > 1 else (lambda i, j: (0, 0, 0))
    return pl.pallas_call(
        functools.partial(_mlp_kernel, D=D, alpha=alpha),
        grid=(N // tm, HID // th),
        in_specs=[pl.BlockSpec((tm, D), lambda i, j: (i, 0)),
                  pl.BlockSpec((1, 1, 6 * D), mod_map),
                  pl.BlockSpec((D, th), lambda i, j: (0, j)),
                  pl.BlockSpec((th, D), lambda i, j: (j, 0)),
                  pl.BlockSpec((1, D), lambda i, j: (0, 0)),
                  pl.BlockSpec((1, D), lambda i, j: (0, 0))],
        out_specs=pl.BlockSpec((tm, D), lambda i, j: (i, 0)),
        out_shape=jax.ShapeDtypeStruct((N, D), F32),
        scratch_shapes=[pltpu.VMEM((tm, D), BF16), pltpu.VMEM((tm, D), F32)],
        compiler_params=_cparams("parallel", "arbitrary"),
        name="mlp",
    )(x2, mod, lw["w_mlp_in"], lw["w_mlp_out"], lw["ln2_g"], lw["ln2_b"])


def _pack_layer_weights(l, D, w_in, gla_w_gate, gla_b_gate, gla_norm_g, s5_d, s5_w_glu,
                        lru_conv_w, lru_conv_b, lru_w_a, lru_b_a, lru_w_i, lru_b_i, lru_lam,
                        w_br_gla, w_br_s5, w_br_lru, w_out, ln1_g, ln1_b, ln2_g, ln2_b,
                        w_mlp_in, w_mlp_out):
    H = GLA_HEADS
    KEY = D // 2
    DK = KEY // H
    W = 3 * D // 2
    CB = 2 * LRU_BLOCK
    NB = W // LRU_BLOCK
    widths = (KEY, KEY, D, D, 2 * GATE_RANK, D, W, W, 3 * D)
    offs = [0]
    for wd in widths:
        offs.append(offs[-1] + wd)
    wl = w_in[l]
    piece = lambda i: wl[:, offs[i]:offs[i + 1]]
    w_pack = jnp.concatenate([piece(0), piece(1), piece(2), piece(3), piece(5), piece(6), piece(7),
                              piece(8)], axis=1).astype(BF16)
    w_glr = jnp.concatenate([piece(4), jnp.zeros((D, LANES - 2 * GATE_RANK), wl.dtype)], axis=1).astype(BF16)
    w_ut = piece(5).T.astype(BF16)
    wgate = gla_w_gate[l]
    wg = jnp.zeros((H, LANES, 2 * DK), F32)
    wg = wg.at[:, 0:GATE_RANK, 0:DK].set(wgate[0].reshape(GATE_RANK, H, DK).transpose(1, 0, 2))
    wg = wg.at[:, GATE_RANK:2 * GATE_RANK, DK:].set(wgate[1].reshape(GATE_RANK, H, DK).transpose(1, 0, 2))
    bgate = gla_b_gate[l].reshape(2, H, DK).transpose(1, 0, 2).reshape(H, 1, 2 * DK)
    lru_wg = 0.5 * jnp.concatenate([lru_w_a[l], lru_w_i[l]], axis=-1)
    lru_wg = lru_wg.reshape(2, W // CB, CB // LRU_BLOCK, LRU_BLOCK, 2 * LRU_BLOCK).astype(BF16)
    lru_bg = 0.5 * jnp.concatenate([lru_b_a[l].reshape(2, W // CB, 1, CB), lru_b_i[l].reshape(2, W // CB, 1, CB)],
                                   axis=-1)
    return {
        "w_pack": w_pack, "w_glr": w_glr, "w_ut": w_ut, "gla_wg": wg.astype(BF16), "gla_bg": bgate,
        "lru": {"conv_w": lru_conv_w[l].reshape(LRU_CONV, W // CB, CB).transpose(1, 0, 2),
                "conv_b": lru_conv_b[l].reshape(W // CB, 1, CB),
                "wg": lru_wg, "bg": lru_bg, "lam": lru_lam[l].reshape(2, W // CB, 1, CB)},
        "gnorm": gla_norm_g[l].reshape(1, D), "s5_d": s5_d[l].reshape(1, D),
        "w_glu": s5_w_glu[l].astype(BF16), "w_br_gla": w_br_gla[l].astype(BF16),
        "w_br_s5": w_br_s5[l].astype(BF16), "w_br_lru": w_br_lru[l].astype(BF16),
        "w_out": w_out[l].astype(BF16),
        "ln1_g": ln1_g[l].reshape(1, D), "ln1_b": ln1_b[l].reshape(1, D),
        "ln2_g": ln2_g[l].reshape(1, D), "ln2_b": ln2_b[l].reshape(1, D),
        "w_mlp_in": w_mlp_in[l].astype(BF16), "w_mlp_out": w_mlp_out[l].astype(BF16),
    }


def _trunk_layer(x2, mod, lw, s5ops, init, B, T, D, alpha):
    N = B * T
    G = D // S5_GROUP
    S = S5_SUB
    R = T // S
    Pn = S5_STATE
    P, glr = _in_proj(x2, mod, lw["w_pack"], lw["w_glr"], T)
    gla0 = s5re0 = s5im0 = lru0 = None
    if init is not None:
        gla0, s5re0, s5im0, lru0 = init
        s5re0 = s5re0.transpose(2, 0, 1, 3).reshape(G, B, 2 * Pn)
        s5im0 = s5im0.transpose(2, 0, 1, 3).reshape(G, B, 2 * Pn)
    o_gla, gla_fin = _gla_core(P, glr, lw["gla_wg"], lw["gla_bg"], gla0, B, T, D)
    if s5re0 is None:
        s5re0 = jnp.zeros((G, B, 2 * Pn), F32)
        s5im0 = jnp.zeros((G, B, 2 * Pn), F32)
    xt = _s5_proj_t(x2, mod, lw["w_ut"], B, T)
    yt, fre, fim = _s5_core_t(xt, s5ops, s5re0, s5im0, R, B)
    ys = _s5_rows(yt, B, T, D)
    s5re_fin = fre.reshape(G, B, 2, Pn).transpose(1, 2, 0, 3)
    s5im_fin = fim.reshape(G, B, 2, Pn).transpose(1, 2, 0, 3)
    P3 = P.reshape(B, T, P.shape[1])
    hf, lf = _lru_sweep(P3, lw["lru"], lru0, None, False, B, T, D)
    ylru, lb = _lru_sweep(P3, lw["lru"], lru0, hf, True, B, T, D)
    lru_fin = jnp.stack([lf, lb], axis=1)
    x1 = _merge(x2, mod, o_gla, P, ys, ylru.reshape(N, -1), lw, T, alpha)
    x3 = _mlp(x1, mod, lw, T, alpha)
    return x3, (gla_fin, s5re_fin, s5im_fin, lru_fin)


def kernel(x_prompt, x_sample, state_gla, state_s5_re, state_s5_im, state_lru, c, c_ctx, w_ada, b_ada, w_in, gla_w_gate, gla_b_gate, gla_norm_g, s5_lam_re, s5_lam_im, s5_log_step, s5_b_re, s5_b_im, s5_c_re, s5_c_im, s5_d, s5_w_glu, lru_conv_w, lru_conv_b, lru_w_a, lru_b_a, lru_w_i, lru_b_i, lru_lam, w_br_gla, w_br_s5, w_br_lru, w_out, ln1_g, ln1_b, ln2_g, ln2_b, w_mlp_in, w_mlp_out):
    Bp, Tp, D = x_prompt.shape
    Bs, Ts, _ = x_sample.shape
    L = w_in.shape[0]
    alpha = (2.0 * L) ** 0.25

    n_rows = -(-(Bs + 1) // SUBLANES) * SUBLANES
    cc = jnp.concatenate([c, c_ctx[None], jnp.zeros((n_rows - Bs - 1, D), F32)], axis=0)
    mod = _ada_mod(cc, w_ada, b_ada)
    s5ops_all = _s5_operators(s5_lam_re, s5_lam_im, s5_log_step, s5_b_re, s5_b_im, s5_c_re, s5_c_im)

    xp = x_prompt.reshape(Bp * Tp, D)
    xs = _add_pos(x_sample, _grid_pos_table(Ts, D)).reshape(Bs * Ts, D)
    fins = []
    for l in range(L):
        lw = _pack_layer_weights(l, D, w_in, gla_w_gate, gla_b_gate, gla_norm_g, s5_d, s5_w_glu,
                                 lru_conv_w, lru_conv_b, lru_w_a, lru_b_a, lru_w_i, lru_b_i, lru_lam,
                                 w_br_gla, w_br_s5, w_br_lru, w_out, ln1_g, ln1_b, ln2_g, ln2_b,
                                 w_mlp_in, w_mlp_out)
        s5ops = tuple(a[l] for a in s5ops_all)
        mod_ctx = mod[l, Bs:Bs + 1].reshape(1, 1, 6 * D)
        mod_lat = mod[l, :Bs].reshape(Bs, 1, 6 * D)
        xp, fin = _trunk_layer(xp, mod_ctx, lw, s5ops, None, Bp, Tp, D, alpha)
        fins.append(fin)
        cache = (state_gla[:, l], state_s5_re[:, l], state_s5_im[:, l], state_lru[:, l])
        xs, _ = _trunk_layer(xs, mod_lat, lw, s5ops, cache, Bs, Ts, D, alpha)
    sdt = x_prompt.dtype
    new_states = tuple(jnp.stack([f[i] for f in fins], axis=1).astype(sdt) for i in range(4))
    return (xp.reshape(Bp, Tp, D), xs.reshape(Bs, Ts, D)) + new_states
```

```python
import functools
import math

import jax
import jax.numpy as jnp
from jax import lax
from jax.experimental import pallas as pl
from jax.experimental.pallas import tpu as pltpu

F32 = jnp.float32
BF16 = jnp.bfloat16
HIGHEST = lax.Precision.HIGHEST

LANES = 128
SUBLANES = 8
VMEM_LIMIT = 56 * 1024 * 1024

GRID_W = 64
GLA_HEADS = 4
GATE_RANK = 16
GLA_TAU = 16.0
GLA_CHUNK = 64
GLA_BLOCK = 256
S5_GROUP = 16
S5_STATE = 64
S5_SUB = 16
LRU_BLOCK = 128
LRU_CONV = 4
LRU_C = 8.0
LN_EPS = 1e-5
F32_TINY = 1.1754944e-38
SCAN_UNROLL = 8


def _cparams(*sem):
    return pltpu.CompilerParams(dimension_semantics=sem, vmem_limit_bytes=VMEM_LIMIT)


def _dot(a, b):
    return jnp.dot(a.astype(BF16), b.astype(BF16), preferred_element_type=F32)


def _dot_nt(a, b):
    return lax.dot_general(a.astype(BF16), b.astype(BF16), (((1,), (1,)), ((), ())),
                           preferred_element_type=F32)


def _dot_tn(a, b):
    return lax.dot_general(a.astype(BF16), b.astype(BF16), (((0,), (0,)), ((), ())),
                           preferred_element_type=F32)


def _dot_f32(a, b):
    return jnp.dot(a, b, precision=HIGHEST, preferred_element_type=F32)


def _layer_norm(z, g, b):
    mu = jnp.mean(z, axis=-1, keepdims=True)
    zc = z - mu
    var = jnp.mean(zc * zc, axis=-1, keepdims=True)
    return zc * lax.rsqrt(var + LN_EPS) * g + b


def _sigmoid(x):
    return 0.5 * jnp.tanh(0.5 * x) + 0.5


def _silu(x):
    return x * _sigmoid(x)


def _ada_kernel(cc_ref, w_ref, b_ref, o_ref):
    s = _silu(cc_ref[...])
    o_ref[0] = _dot_f32(s, w_ref[0]) + b_ref[0]


def _ada_mod(cc, w_ada, b_ada):
    L, D, D6 = w_ada.shape
    R = cc.shape[0]
    tn = D6 // 4
    return pl.pallas_call(
        _ada_kernel,
        grid=(L, D6 // tn),
        in_specs=[pl.BlockSpec((R, D), lambda l, j: (0, 0)),
                  pl.BlockSpec((1, D, tn), lambda l, j: (l, 0, j)),
                  pl.BlockSpec((1, 1, tn), lambda l, j: (l, 0, j))],
        out_specs=pl.BlockSpec((1, R, tn), lambda l, j: (l, 0, j)),
        out_shape=jax.ShapeDtypeStruct((L, R, D6), F32),
        compiler_params=_cparams("parallel", "parallel"),
        name="ada_mod",
    )(cc, w_ada, b_ada.reshape(L, 1, D6))


def _addpos_kernel(x_ref, p_ref, o_ref):
    o_ref[0] = x_ref[0] + p_ref[...]


def _add_pos(x, pos):
    B, T, D = x.shape
    tt = min(T, 512)
    return pl.pallas_call(
        _addpos_kernel,
        grid=(T // tt, B),
        in_specs=[pl.BlockSpec((1, tt, D), lambda t, b: (b, t, 0)),
                  pl.BlockSpec((tt, D), lambda t, b: (t, 0))],
        out_specs=pl.BlockSpec((1, tt, D), lambda t, b: (b, t, 0)),
        out_shape=jax.ShapeDtypeStruct((B, T, D), F32),
        compiler_params=_cparams("parallel", "parallel"),
        name="add_pos",
    )(x, pos)


def _grid_pos_table(n_tokens, dim):
    rows = n_tokens // GRID_W
    quarter = dim // 4
    omega = 1.0 / (10000.0 ** (jnp.arange(quarter, dtype=F32) / quarter))
    r = jnp.arange(rows, dtype=F32)[:, None, None] * omega
    cl = jnp.arange(GRID_W, dtype=F32)[None, :, None] * omega
    shape = (rows, GRID_W, quarter)
    emb = jnp.concatenate([jnp.broadcast_to(jnp.sin(r), shape), jnp.broadcast_to(jnp.cos(r), shape),
                           jnp.broadcast_to(jnp.sin(cl), shape), jnp.broadcast_to(jnp.cos(cl), shape)], axis=-1)
    return emb.reshape(rows * GRID_W, dim)


def _inproj_kernel(x_ref, mod_ref, w_ref, wglr_ref, o_ref, glr_ref, h_ref, *, D):
    @pl.when(pl.program_id(1) == 0)
    def _():
        shift = mod_ref[0, :, 0:D]
        scale = mod_ref[0, :, D:2 * D]
        h = (x_ref[...] * (1.0 + scale) + shift).astype(BF16)
        h_ref[...] = h
        glr_ref[...] = jnp.dot(h, wglr_ref[...], preferred_element_type=F32)

    o_ref[...] = jnp.dot(h_ref[...], w_ref[...], preferred_element_type=F32)


def _row_tile(N, T, per_batch_mod, want):
    tm = min(want, T if per_batch_mod else N)
    while N % tm or (per_batch_mod and T % tm):
        tm //= 2
    return tm


def _in_proj(x2, mod, w_pack, w_glr, T):
    N, D = x2.shape
    NC = w_pack.shape[1]
    Bm = mod.shape[0]
    tm = _row_tile(N, T, Bm > 1, 1024)
    tn = D
    per_b = T // tm
    mod_map = (lambda i, j: (i // per_b, 0, 0)) if Bm > 1 else (lambda i, j: (0, 0, 0))
    return pl.pallas_call(
        functools.partial(_inproj_kernel, D=D),
        grid=(N // tm, NC // tn),
        in_specs=[pl.BlockSpec((tm, D), lambda i, j: (i, 0)),
                  pl.BlockSpec((1, 1, 6 * D), mod_map),
                  pl.BlockSpec((D, tn), lambda i, j: (0, j)),
                  pl.BlockSpec((D, LANES), lambda i, j: (0, 0))],
        out_specs=[pl.BlockSpec((tm, tn), lambda i, j: (i, j)),
                   pl.BlockSpec((tm, LANES), lambda i, j: (i, 0))],
        out_shape=[jax.ShapeDtypeStruct((N, NC), F32), jax.ShapeDtypeStruct((N, LANES), F32)],
        scratch_shapes=[pltpu.VMEM((tm, D), BF16)],
        compiler_params=_cparams("parallel", "arbitrary"),
        name="in_proj",
    )(x2, mod, w_pack, w_glr)


def _gla_kernel(*refs, T, DK, DV, has_init):
    if has_init:
        (q_ref, k_ref, v_ref, glr_ref, wg_ref, bg_ref, s0_ref,
         o_ref, sfin_ref, qb_ref, keb_ref, db_ref, sf_ref, sb_ref) = refs
    else:
        (q_ref, k_ref, v_ref, glr_ref, wg_ref, bg_ref,
         o_ref, sfin_ref, qb_ref, keb_ref, db_ref, sf_ref, sb_ref) = refs
    C = GLA_CHUNK
    BLK = GLA_BLOCK
    NS = BLK // C
    nblk = T // BLK
    qscale = DK ** -0.5
    shift = C.bit_length() - 1
    row = lax.broadcasted_iota(jnp.int32, (BLK, BLK), 0)
    col = lax.broadcasted_iota(jnp.int32, (BLK, BLK), 1)
    same = lax.shift_right_logical(row, shift) == lax.shift_right_logical(col, shift)
    lower = jnp.logical_and(same, row >= col)
    upper = jnp.logical_and(same, row <= col)
    tri = jnp.where(lower, 1.0, 0.0).astype(BF16)
    chunk_of_row = lax.shift_right_logical(lax.broadcasted_iota(jnp.int32, (BLK, DK), 0), shift)
    wg = wg_ref[0]
    bg = bg_ref[0]

    def chunk_cols(ke):
        return jnp.concatenate([jnp.where(chunk_of_row == c, ke, 0.0) for c in range(NS)], axis=1).astype(BF16)

    def per_chunk_last(x, r):
        return jnp.concatenate([jnp.broadcast_to(x[c * C + r:c * C + r + 1, :], (C, x.shape[1]))
                                for c in range(NS)], axis=0)

    if has_init:
        sf_ref[...] = s0_ref[0, 0, 0].T
        sb_ref[...] = s0_ref[0, 1, 0].T
    else:
        sf_ref[...] = jnp.zeros((DV, DK), F32)
        sb_ref[...] = jnp.zeros((DV, DK), F32)

    def fwd_body(i, carry):
        r0 = pl.multiple_of(i * BLK, BLK)
        rows = pl.ds(r0, BLK)
        qc = q_ref[rows, :] * qscale
        kc = k_ref[rows, :]
        vc = v_ref[rows, :]
        logits = _dot(glr_ref[rows, :], wg) + bg
        la = (jnp.minimum(logits, 0.0) - jnp.log(1.0 + jnp.exp(-jnp.abs(logits)))) * (1.0 / GLA_TAU)
        p1 = la.astype(BF16)
        r1 = la - p1.astype(F32)
        p2 = r1.astype(BF16)
        p3 = (r1 - p2.astype(F32)).astype(BF16)
        pre = (jnp.dot(tri, p1, preferred_element_type=F32) + jnp.dot(tri, p2, preferred_element_type=F32)
               + jnp.dot(tri, p3, preferred_element_type=F32))
        tot = per_chunk_last(pre, C - 1)
        cum_f = pre[:, :DK]
        last_f = tot[:, :DK]
        cum_b = tot[:, DK:] - pre[:, DK:] + la[:, DK:]
        last_b = tot[:, DK:]
        q_f = qc * jnp.exp(cum_f)
        k_f = kc * jnp.exp(-cum_f)
        ke_f = kc * jnp.exp(last_f - cum_f)
        q_b = qc * jnp.exp(cum_b)
        k_b = kc * jnp.exp(-cum_b)
        ke_b = kc * jnp.exp(last_b - cum_b)
        sc = jnp.where(lower, _dot_nt(q_f, k_f), 0.0) + jnp.where(upper, _dot_nt(q_b, k_b), 0.0)
        o_blk = _dot(sc, vc)
        u_f = jnp.dot(vc.T.astype(BF16), chunk_cols(ke_f), preferred_element_type=F32)
        d_f = jnp.exp(last_f)
        s = sf_ref[...]
        for c in range(NS):
            rc = slice(c * C, (c + 1) * C)
            o_ref[pl.ds(r0 + c * C, C), :] = o_blk[rc] + _dot_nt(q_f[rc], s)
            s = s * d_f[c * C:c * C + 1, :] + u_f[:, c * DK:(c + 1) * DK]
        sf_ref[...] = s
        qb_ref[rows, :] = q_b
        keb_ref[rows, :] = ke_b
        d_b = jnp.exp(last_b)
        db_ref[pl.ds(i, 1), :] = jnp.concatenate([d_b[c * C:c * C + 1, :] for c in range(NS)], axis=1)
        return carry

    lax.fori_loop(0, nblk, fwd_body, 0, unroll=min(8, nblk))

    def bwd_body(j, carry):
        i = nblk - 1 - j
        r0 = pl.multiple_of(i * BLK, BLK)
        rows = pl.ds(r0, BLK)
        q_b = qb_ref[rows, :]
        u_b = jnp.dot(v_ref[rows, :].T.astype(BF16), chunk_cols(keb_ref[rows, :]), preferred_element_type=F32)
        d_b = db_ref[pl.ds(i, 1), :]
        s = sb_ref[...]
        for c in reversed(range(NS)):
            rc = pl.ds(r0 + c * C, C)
            o_ref[rc, :] = o_ref[rc, :] + _dot_nt(q_b[c * C:(c + 1) * C], s)
            s = s * d_b[:, c * DK:(c + 1) * DK] + u_b[:, c * DK:(c + 1) * DK]
        sb_ref[...] = s
        return carry

    lax.fori_loop(0, nblk, bwd_body, 0, unroll=min(8, nblk))

    sfin_ref[0, 0, 0] = sf_ref[...].T
    sfin_ref[0, 1, 0] = sb_ref[...].T


def _gla_core(P, glr, wg, bg, s0, B, T, D):
    H = GLA_HEADS
    DK = D // 2 // H
    DV = D // H
    N = B * T
    has_init = s0 is not None
    kcol = (D // 2) // DK
    vcol = D // DV
    in_specs = [pl.BlockSpec((T, DK), lambda b, h: (b, h)),
                pl.BlockSpec((T, DK), lambda b, h: (b, kcol + h)),
                pl.BlockSpec((T, DV), lambda b, h: (b, vcol + h)),
                pl.BlockSpec((T, LANES), lambda b, h: (b, 0)),
                pl.BlockSpec((1, LANES, 2 * DK), lambda b, h: (h, 0, 0)),
                pl.BlockSpec((1, 1, 2 * DK), lambda b, h: (h, 0, 0))]
    args = [P, P, P, glr, wg, bg]
    if has_init:
        in_specs.append(pl.BlockSpec((1, 2, 1, DK, DV), lambda b, h: (b, 0, h, 0, 0)))
        args.append(s0)
    o, sfin = pl.pallas_call(
        functools.partial(_gla_kernel, T=T, DK=DK, DV=DV, has_init=has_init),
        grid=(B, H),
        in_specs=in_specs,
        out_specs=[pl.BlockSpec((T, DV), lambda b, h: (b, h)),
                   pl.BlockSpec((1, 2, 1, DK, DV), lambda b, h: (b, 0, h, 0, 0))],
        out_shape=[jax.ShapeDtypeStruct((N, D), F32),
                   jax.ShapeDtypeStruct((B, 2, H, DK, DV), F32)],
        scratch_shapes=[pltpu.VMEM((T, DK), F32), pltpu.VMEM((T, DK), F32),
                        pltpu.VMEM((T // GLA_BLOCK, (GLA_BLOCK // GLA_CHUNK) * DK), F32),
                        pltpu.VMEM((DV, DK), F32), pltpu.VMEM((DV, DK), F32)],
        compiler_params=_cparams("parallel", "parallel"),
        name="gla_core",
    )(*args)
    return o, sfin


def _s5prep_kernel(lr_ref, li_ref, ls_ref, btr_ref, bti_ref, cr_ref, ci_ref,
                   kt_ref, wre_ref, wim_ref, cpre_ref, cpimn_ref, pre_ref, pim_ref):
    I = S5_GROUP
    lr = lr_ref[0, 0]
    li = li_ref[0, 0]
    step = jnp.exp(ls_ref[0, 0])
    mag = jnp.exp(lr * step)
    ang = li * step
    a_re = mag * jnp.cos(ang)
    a_im = mag * jnp.sin(ang)
    den = lr * lr + li * li
    nr = a_re - 1.0
    f_re = (nr * lr + a_im * li) / den
    f_im = (a_im * lr - nr * li) / den
    btr = btr_ref[0]
    bti = bti_ref[0]
    bb_re = f_re[:, None, :] * btr - f_im[:, None, :] * bti
    bb_im = f_re[:, None, :] * bti + f_im[:, None, :] * btr
    c_re = cr_ref[0]
    c_im = ci_ref[0]
    p_re = jnp.ones_like(lr)
    p_im = jnp.zeros_like(lr)
    batched_nt = (((2,), (2,)), ((0,), (0,)))
    for e in range(S5_SUB + 1):
        pr = p_re[:, None, :]
        pi = p_im[:, None, :]
        cp_re = c_re * pr - c_im * pi
        cp_im = c_re * pi + c_im * pr
        cpre_ref[0, 0, :, e * I:(e + 1) * I, :] = cp_re
        cpimn_ref[0, 0, :, e * I:(e + 1) * I, :] = -cp_im
        if e < S5_SUB:
            wre_ref[0, 0, :, e * I:(e + 1) * I, :] = pr * bb_re - pi * bb_im
            wim_ref[0, 0, :, e * I:(e + 1) * I, :] = pr * bb_im + pi * bb_re
            kt_ref[0, 0, :, e * I:(e + 1) * I, :] = (
                lax.dot_general(cp_re, bb_re, batched_nt, precision=HIGHEST, preferred_element_type=F32)
                - lax.dot_general(cp_im, bb_im, batched_nt, precision=HIGHEST, preferred_element_type=F32))
            p_re, p_im = p_re * a_re - p_im * a_im, p_re * a_im + p_im * a_re
    pre_ref[0, 0] = p_re
    pim_ref[0, 0] = p_im


def _s5_operators(lam_re, lam_im, log_step, b_re, b_im, c_re, c_im):
    L, _, G, Pn = lam_re.shape
    I = S5_GROUP
    S = S5_SUB
    ls = jnp.broadcast_to(log_step[..., None], lam_re.shape)
    btr = jnp.swapaxes(b_re, -1, -2)
    bti = jnp.swapaxes(b_im, -1, -2)
    Gb = SUBLANES
    lam_spec = pl.BlockSpec((1, 1, Gb, Pn), lambda l, d, g: (l, d, g, 0))
    gip_spec = pl.BlockSpec((1, Gb, I, Pn), lambda l, d, g: (l, g, 0, 0))

    def out_spec(rows, last):
        return pl.BlockSpec((1, 1, Gb, rows, last), lambda l, d, g: (l, d, g, 0, 0))

    kt, wre, wim, cpre, cpimn, pre, pim = pl.pallas_call(
        _s5prep_kernel,
        grid=(L, 2, G // Gb),
        in_specs=[lam_spec, lam_spec, lam_spec, gip_spec, gip_spec, gip_spec, gip_spec],
        out_specs=[out_spec(S * I, I), out_spec(S * I, Pn), out_spec(S * I, Pn),
                   out_spec((S + 1) * I, Pn), out_spec((S + 1) * I, Pn), lam_spec, lam_spec],
        out_shape=[jax.ShapeDtypeStruct((L, 2, G, S * I, I), F32),
                   jax.ShapeDtypeStruct((L, 2, G, S * I, Pn), F32),
                   jax.ShapeDtypeStruct((L, 2, G, S * I, Pn), F32),
                   jax.ShapeDtypeStruct((L, 2, G, (S + 1) * I, Pn), F32),
                   jax.ShapeDtypeStruct((L, 2, G, (S + 1) * I, Pn), F32),
                   jax.ShapeDtypeStruct((L, 2, G, Pn), F32),
                   jax.ShapeDtypeStruct((L, 2, G, Pn), F32)],
        compiler_params=_cparams("parallel", "parallel", "parallel"),
        name="s5_prep",
    )(lam_re, lam_im, ls, btr, bti, c_re, c_im)

    kt = kt.reshape(L, 2, G, S, I, I)
    s_idx = jnp.arange(S)[:, None]
    t_idx = jnp.arange(S)[None, :]

    def toeplitz(k, lag, valid):
        m = k[:, :, jnp.clip(lag, 0, S - 1)]
        m = jnp.where(valid[None, None, :, :, None, None], m, 0.0)
        return m.transpose(0, 1, 3, 4, 2, 5).reshape(L, G, S * I, S * I)

    mf = toeplitz(kt[:, 0], t_idx - s_idx, t_idx >= s_idx)
    mb = toeplitz(kt[:, 1], s_idx - t_idx, s_idx >= t_idx)

    def by_pos(w, reverse):
        w = w.reshape(L, G, S, I, Pn)
        if reverse:
            w = w[:, :, ::-1]
        return w.reshape(L, G, S * I, Pn).swapaxes(-1, -2)

    wall = jnp.concatenate([by_pos(wre[:, 0], True), by_pos(wre[:, 1], False),
                            by_pos(wim[:, 0], True), by_pos(wim[:, 1], False)], axis=2)

    def readout(cp):
        cp = cp.reshape(L, 2, G, S + 1, I, Pn)
        f = cp[:, 0, :, 1:]
        b = cp[:, 1, :, 1:][:, :, ::-1]
        return jnp.concatenate([f.reshape(L, G, S * I, Pn), b.reshape(L, G, S * I, Pn)], axis=-1)

    vre = readout(cpre)
    vim = readout(cpimn)
    are = jnp.concatenate([pre[:, 0], pre[:, 1]], axis=-1)[:, :, None, :]
    aim = jnp.concatenate([pim[:, 0], pim[:, 1]], axis=-1)[:, :, None, :]
    return mf, mb, wall, vre, vim, are, aim


def _s5projt_kernel(x_ref, mod_ref, wt_ref, o_ref, *, D, R):
    Bm = mod_ref.shape[0]
    span = x_ref.shape[0] // Bm
    parts = []
    for b in range(Bm):
        shift = mod_ref[b, :, 0:D]
        scale = mod_ref[b, :, D:2 * D]
        parts.append((x_ref[b * span:(b + 1) * span, :] * (1.0 + scale) + shift).astype(BF16))
    h = parts[0] if Bm == 1 else jnp.concatenate(parts, axis=0)
    ut = lax.dot_general(wt_ref[...], h, (((1,), (1,)), ((), ())), preferred_element_type=F32)
    o_ref[...] = ut.reshape(o_ref.shape).astype(o_ref.dtype)


def _s5_proj_t(x2, mod, w_ut, B, T):
    N, D = x2.shape
    S = S5_SUB
    R = T // S
    G = D // S5_GROUP
    Bm = mod.shape[0]
    xv = x2.reshape(B * R, S * D)
    per_batch = Bm > 1 and R % LANES == 0
    rows = R if per_batch else B * R
    nrb = B * R // rows
    if per_batch:
        mod_spec = pl.BlockSpec((1, 1, 6 * D), lambda i, s: (i, 0, 0))
    else:
        mod_spec = pl.BlockSpec((Bm, 1, 6 * D), lambda i, s: (0, 0, 0))
    return pl.pallas_call(
        functools.partial(_s5projt_kernel, D=D, R=R),
        grid=(nrb, S),
        in_specs=[pl.BlockSpec((rows, D), lambda i, s: (i, s)),
                  mod_spec,
                  pl.BlockSpec((D, D), lambda i, s: (0, 0))],
        out_specs=pl.BlockSpec((G, S5_GROUP, rows), lambda i, s: (0, s, i)),
        out_shape=jax.ShapeDtypeStruct((G, S * S5_GROUP, B * R), F32),
        compiler_params=_cparams("parallel", "parallel"),
        name="s5_proj_t",
    )(xv, mod, w_ut)


def _s5t_kernel(xt_ref, dcol_ref, mft_ref, mbt_ref, wt_ref, vret_ref, vimt_ref, are_ref, aim_ref,
                s0re_ref, s0im_ref, yt_ref, fre_ref, fim_ref,
                zt_ref, zre_ref, zim_ref, hfre_ref, hfim_ref, hbre_ref, hbim_ref, *, R, B, chunk_major):
    Pn = S5_STATE
    xt32 = xt_ref[0]
    xt = xt32.astype(BF16)
    mt = (mft_ref[0] + mbt_ref[0]).astype(BF16)
    yt_ref[0] = jnp.dot(mt, xt, preferred_element_type=F32) + dcol_ref[0] * xt32
    zt_ref[...] = jnp.dot(wt_ref[0].astype(BF16), xt, preferred_element_type=F32)
    if chunk_major:
        for b in range(B):
            zb = zt_ref[:, b * R:(b + 1) * R].T
            zre_ref[pl.ds(b, R, stride=B), :] = zb[:, :2 * Pn]
            zim_ref[pl.ds(b, R, stride=B), :] = zb[:, 2 * Pn:]
        step_rows = lambda r: pl.ds(pl.multiple_of(r * B, B), B)
    else:
        z = zt_ref[...].T
        zre_ref[...] = z[:, :2 * Pn]
        zim_ref[...] = z[:, 2 * Pn:]
        step_rows = lambda r: pl.ds(r, B, stride=R)
    a_re = are_ref[0]
    a_im = aim_ref[0]
    h0 = (s0re_ref[0], s0im_ref[0])

    def advance(h, rows):
        hr, hi = h
        return (hr * a_re - hi * a_im + zre_ref[rows, :], hr * a_im + hi * a_re + zim_ref[rows, :])

    def step(r, carry):
        hf, hb = carry
        rows_f = step_rows(r)
        rows_b = step_rows(R - 1 - r)
        hfre_ref[rows_f, :] = hf[0]
        hfim_ref[rows_f, :] = hf[1]
        hbre_ref[rows_b, :] = hb[0]
        hbim_ref[rows_b, :] = hb[1]
        return advance(hf, rows_f), advance(hb, rows_b)

    hf, hb = lax.fori_loop(0, R, step, (h0, h0), unroll=SCAN_UNROLL)

    def own_half(f, b):
        return jnp.where(lax.broadcasted_iota(jnp.int32, f.shape, 1) < Pn, f, b)

    fre_ref[0] = own_half(hf[0], hb[0])
    fim_ref[0] = own_half(hf[1], hb[1])
    vret = vret_ref[0]
    vimt = vimt_ref[0]
    if chunk_major:
        for b in range(B):
            sel = pl.ds(b, R, stride=B)
            hre_b = own_half(hfre_ref[sel, :], hbre_ref[sel, :])
            him_b = own_half(hfim_ref[sel, :], hbim_ref[sel, :])
            cols = slice(b * R, (b + 1) * R)
            yt_ref[0, :, cols] = yt_ref[0, :, cols] + _dot_nt(vret, hre_b) + _dot_nt(vimt, him_b)
    else:
        hre = own_half(hfre_ref[...], hbre_ref[...])
        him = own_half(hfim_ref[...], hbim_ref[...])
        yt_ref[0] = yt_ref[0] + _dot_nt(vret, hre) + _dot_nt(vimt, him)


def _s5_core_t(xt, dcol, ops, s0re, s0im, R, B):
    G, K, BR = xt.shape
    mft, mbt, wt, vret, vimt, are, aim = ops
    P2 = 2 * S5_STATE
    chunk_major = B == SUBLANES and R % LANES == 0
    gspec = lambda shape: pl.BlockSpec((1,) + shape, lambda g: (g, 0, 0))
    return pl.pallas_call(
        functools.partial(_s5t_kernel, R=R, B=B, chunk_major=chunk_major),
        grid=(G,),
        in_specs=[gspec((K, BR)), gspec((K, 1)), gspec((K, K)), gspec((K, K)), gspec((K, K)),
                  gspec((K, P2)), gspec((K, P2)), gspec((1, P2)), gspec((1, P2)),
                  gspec((B, P2)), gspec((B, P2))],
        out_specs=[gspec((K, BR)), gspec((B, P2)), gspec((B, P2))],
        out_shape=[jax.ShapeDtypeStruct((G, K, BR), F32),
                   jax.ShapeDtypeStruct((G, B, P2), F32),
                   jax.ShapeDtypeStruct((G, B, P2), F32)],
        scratch_shapes=[pltpu.VMEM((K, BR), F32)] + [pltpu.VMEM((BR, P2), F32) for _ in range(6)],
        compiler_params=_cparams("parallel"),
        name="s5_core_t",
    )(xt, dcol, mft, mbt, wt, vret, vimt, are, aim, s0re, s0im)


def _s5rows_kernel(yt_ref, o_ref):
    I = S5_GROUP
    nc = yt_ref.shape[-1]
    for t in range(S5_SUB):
        blk = yt_ref[:, t * I:(t + 1) * I, :]
        o_ref[pl.ds(t, nc, stride=S5_SUB), :] = blk.reshape(LANES, nc).T


def _s5_rows(yt, B, T, D):
    G = D // S5_GROUP
    BR = B * T // S5_SUB
    gpt = LANES // S5_GROUP
    nc = 2 * LANES if BR % (2 * LANES) == 0 else LANES
    return pl.pallas_call(
        _s5rows_kernel,
        grid=(BR // nc, G // gpt),
        in_specs=[pl.BlockSpec((gpt, S5_SUB * S5_GROUP, nc), lambda i, j: (j, 0, i))],
        out_specs=pl.BlockSpec((nc * S5_SUB, LANES), lambda i, j: (i, j)),
        out_shape=jax.ShapeDtypeStruct((B * T, D), F32),
        compiler_params=_cparams("parallel", "parallel"),
        name="s5_rows",
    )(yt)


def _lru_kernel(*refs, Tt, CB, reverse, has_init):
    refs = list(refs)
    lx_ref, prev_ref, next_ref, cw_ref, cb_ref, wg_ref, bgate_ref, lam_ref = refs[:8]
    pos = 8
    h0_ref = None
    if has_init:
        h0_ref = refs[pos]
        pos += 1
    if reverse:
        hf_ref, ly_ref = refs[pos:pos + 2]
        pos += 2
    out_ref, fin_ref, x_scr, a_scr, b_scr, h_scr, carry_ref = refs[pos:pos + 7]
    NBk = CB // LRU_BLOCK
    k = pl.program_id(2)
    nk = pl.num_programs(2)
    first_tile = k == 0

    @pl.when(first_tile)
    def _():
        if has_init:
            carry_ref[...] = h0_ref[...]
        else:
            carry_ref[...] = jnp.zeros((SUBLANES, CB), F32)

    tpos = (nk - 1 - k) if reverse else k
    has_prev = tpos > 0
    has_next = tpos < nk - 1
    cw = cw_ref[0]
    cbias = cb_ref[0]
    bgate = bgate_ref[0]
    lam = lam_ref[0]
    half_cfac = (-0.5 * LRU_C) * (jnp.maximum(-lam, 0.0) + jnp.log1p(jnp.exp(-jnp.abs(lam))))

    left = LRU_CONV // 2
    for b in range(SUBLANES):
        cur = lx_ref[b]
        pv = jnp.where(has_prev, prev_ref[b], 0.0)
        nx = jnp.where(has_next, next_ref[b], 0.0)
        for c in range(NBk):
            sl = slice(c * LRU_BLOCK, (c + 1) * LRU_BLOCK)
            x_scr[c, pl.ds(left * SUBLANES + b, Tt, stride=SUBLANES), :] = cur[:, sl]
            for q in range(left):
                x_scr[c, pl.ds(q * SUBLANES + b, 1), :] = pv[SUBLANES - left + q:SUBLANES - left + q + 1, sl]
            for q in range(LRU_CONV - 1 - left):
                x_scr[c, pl.ds((Tt + left + q) * SUBLANES + b, 1), :] = nx[q:q + 1, sl]

    RC = min(Tt * SUBLANES, 512)
    for c in range(NBk):
        sl = slice(c * LRU_BLOCK, (c + 1) * LRU_BLOCK)
        bias_c = jnp.concatenate([bgate[:, sl], bgate[:, CB + c * LRU_BLOCK: CB + (c + 1) * LRU_BLOCK]], axis=1)
        for r0 in range(0, Tt * SUBLANES, RC):
            xc = cbias[:, sl] + sum(cw[j:j + 1, sl] * x_scr[c, pl.ds(r0 + j * SUBLANES, RC), :]
                                    for j in range(LRU_CONV))
            th = jnp.tanh(_dot(xc, wg_ref[0, c]) + bias_c)
            log_a = half_cfac[:, sl] * th[:, :LRU_BLOCK] + half_cfac[:, sl]
            a = jnp.exp(log_a)
            om = -jnp.tanh(log_a) * (1.0 + a * a)
            root = om * lax.rsqrt(jnp.maximum(om, F32_TINY))
            bt = root * ((0.5 * th[:, LRU_BLOCK:] + 0.5) * xc)
            a_scr[c, pl.ds(r0, RC), :] = a
            b_scr[c, pl.ds(r0, RC), :] = bt

    def step(s, h):
        t = (Tt - 1 - s) if reverse else s
        rows = pl.ds(pl.multiple_of(t * SUBLANES, SUBLANES), SUBLANES)
        new = []
        for c in range(NBk):
            hc = a_scr[c, rows, :] * h[c] + b_scr[c, rows, :]
            h_scr[c, rows, :] = hc
            new.append(hc)
        return tuple(new)

    h_init = tuple(carry_ref[:, c * LRU_BLOCK:(c + 1) * LRU_BLOCK] for c in range(NBk))
    h_last = lax.fori_loop(0, Tt, step, h_init, unroll=SCAN_UNROLL)
    for c in range(NBk):
        carry_ref[:, c * LRU_BLOCK:(c + 1) * LRU_BLOCK] = h_last[c]
        fin_ref[:, c * LRU_BLOCK:(c + 1) * LRU_BLOCK] = h_last[c]

    for b in range(SUBLANES):
        hb = jnp.concatenate([h_scr[c, pl.ds(b, Tt, stride=SUBLANES), :] for c in range(NBk)], axis=1)
        if reverse:
            out_ref[b] = (hf_ref[b] + hb) * jax.nn.gelu(ly_ref[b])
        else:
            out_ref[b] = hb


def _lru_sweep(P3, lw, h0, hf, reverse, B, T, D):
    W = 3 * D // 2
    CB = 2 * LRU_BLOCK
    Tt = min(T, 256)
    nk = T // Tt
    nb8 = Tt // SUBLANES
    lxcol = (D // 2 + D // 2 + D + D) // CB
    lycol = lxcol + W // CB
    d = 1 if reverse else 0
    has_init = h0 is not None

    def tmap(k):
        return (nk - 1 - k) if reverse else k

    in_specs = [
        pl.BlockSpec((SUBLANES, Tt, CB), lambda g, j, k: (g, tmap(k), lxcol + j)),
        pl.BlockSpec((SUBLANES, SUBLANES, CB),
                     lambda g, j, k: (g, jnp.maximum(tmap(k) * nb8 - 1, 0), lxcol + j)),
        pl.BlockSpec((SUBLANES, SUBLANES, CB),
                     lambda g, j, k: (g, jnp.minimum((tmap(k) + 1) * nb8, T // SUBLANES - 1), lxcol + j)),
        pl.BlockSpec((1, LRU_CONV, CB), lambda g, j, k: (j, 0, 0)),
        pl.BlockSpec((1, 1, CB), lambda g, j, k: (j, 0, 0)),
        pl.BlockSpec((1, CB // LRU_BLOCK, LRU_BLOCK, 2 * LRU_BLOCK), lambda g, j, k: (j, 0, 0, 0)),
        pl.BlockSpec((1, 1, 2 * CB), lambda g, j, k: (j, 0, 0)),
        pl.BlockSpec((1, 1, CB), lambda g, j, k: (j, 0, 0)),
    ]
    args = [P3, P3, P3, lw["conv_w"], lw["conv_b"], lw["wg"][d], lw["bg"][d], lw["lam"][d]]
    if has_init:
        in_specs.append(pl.BlockSpec((SUBLANES, CB), lambda g, j, k: (g, j)))
        args.append(h0[:, d])
    if reverse:
        in_specs.append(pl.BlockSpec((SUBLANES, Tt, CB), lambda g, j, k: (g, tmap(k), j)))
        in_specs.append(pl.BlockSpec((SUBLANES, Tt, CB), lambda g, j, k: (g, tmap(k), lycol + j)))
        args += [hf, P3]
    out, fin = pl.pallas_call(
        functools.partial(_lru_kernel, Tt=Tt, CB=CB, reverse=reverse, has_init=has_init),
        grid=(B // SUBLANES, W // CB, nk),
        in_specs=in_specs,
        out_specs=[pl.BlockSpec((SUBLANES, Tt, CB), lambda g, j, k: (g, tmap(k), j)),
                   pl.BlockSpec((SUBLANES, CB), lambda g, j, k: (g, j))],
        out_shape=[jax.ShapeDtypeStruct((B, T, W), F32), jax.ShapeDtypeStruct((B, W), F32)],
        scratch_shapes=[pltpu.VMEM((CB // LRU_BLOCK, (Tt + LRU_CONV) * SUBLANES, LRU_BLOCK), F32)]
        + [pltpu.VMEM((CB // LRU_BLOCK, Tt * SUBLANES, LRU_BLOCK), F32) for _ in range(3)]
        + [pltpu.VMEM((SUBLANES, CB), F32)],
        compiler_params=_cparams("parallel", "parallel", "arbitrary"),
        name="lru_bwd" if reverse else "lru_fwd",
    )(*args)
    return out, fin


def _merge_kernel(x_ref, mod_ref, o_ref, r_ref, ys_ref, yl_ref, gg_ref, gs_ref, gl_ref,
                  gn_ref, wglu_ref, wbg_ref, wbs_ref, wbl_ref, wo_ref, lg_ref, lb_ref,
                  out_ref, *, D, alpha):
    DV = D // GLA_HEADS
    gn = gn_ref[...]
    parts = []
    for h in range(GLA_HEADS):
        sl = slice(h * DV, (h + 1) * DV)
        o = o_ref[:, sl]
        mu = jnp.mean(o, axis=-1, keepdims=True)
        oc = o - mu
        var = jnp.mean(oc * oc, axis=-1, keepdims=True)
        parts.append(oc * lax.rsqrt(var + LN_EPS) * gn[:, sl] * _silu(r_ref[:, sl]))
    y_gla = jnp.concatenate(parts, axis=1)
    ys = jax.nn.gelu(ys_ref[...])
    y_s5 = ys * _sigmoid(_dot(ys, wglu_ref[...]))
    merged = (_sigmoid(gg_ref[...]) * _dot(y_gla, wbg_ref[...])
              + _sigmoid(gs_ref[...]) * _dot(y_s5, wbs_ref[...])
              + _sigmoid(gl_ref[...]) * _dot(yl_ref[...], wbl_ref[...]))
    mix = _dot(merged, wo_ref[...])
    gate1 = mod_ref[0, :, 2 * D:3 * D]
    out_ref[...] = _layer_norm(alpha * x_ref[...] + gate1 * mix, lg_ref[...], lb_ref[...])


def _merge(x2, mod, o_gla, P, ys, ylru, lw, T, alpha):
    N, D = x2.shape
    W = 3 * D // 2
    Bm = mod.shape[0]
    rcol, gcol = 2, 6
    tm = _row_tile(N, T, Bm > 1, 256)
    per_b = T // tm
    mod_map = (lambda i: (i // per_b, 0, 0)) if Bm > 1 else (lambda i: (0, 0, 0))
    row = lambda width, cb=0: pl.BlockSpec((tm, width), lambda i: (i, cb))
    full = lambda a: pl.BlockSpec(a.shape, lambda i: (0,) * a.ndim)
    weights = [lw["gnorm"], lw["w_glu"], lw["w_br_gla"], lw["w_br_s5"], lw["w_br_lru"],
               lw["w_out"], lw["ln1_g"], lw["ln1_b"]]
    return pl.pallas_call(
        functools.partial(_merge_kernel, D=D, alpha=alpha),
        grid=(N // tm,),
        in_specs=[row(D), pl.BlockSpec((1, 1, 6 * D), mod_map), row(D), row(D, rcol), row(D),
                  row(W), row(D, gcol), row(D, gcol + 1), row(D, gcol + 2)]
        + [full(w) for w in weights],
        out_specs=row(D),
        out_shape=jax.ShapeDtypeStruct((N, D), F32),
        compiler_params=_cparams("parallel"),
        name="merge",
    )(x2, mod, o_gla, P, ys, ylru, P, P, P, *weights)


def _mlp_kernel(x_ref, mod_ref, w1_ref, w2_ref, lg_ref, lb_ref, out_ref, h_ref, acc_ref, *, D, alpha):
    j = pl.program_id(1)

    @pl.when(j == 0)
    def _():
        shift = mod_ref[0, :, 3 * D:4 * D]
        scale = mod_ref[0, :, 4 * D:5 * D]
        h_ref[...] = (x_ref[...] * (1.0 + scale) + shift).astype(BF16)
        acc_ref[...] = jnp.zeros_like(acc_ref)

    hid = jnp.dot(h_ref[...], w1_ref[...], preferred_element_type=F32)
    hid = jnp.square(jnp.maximum(hid, 0.0))
    acc_ref[...] += _dot(hid, w2_ref[...])

    @pl.when(j == pl.num_programs(1) - 1)
    def _():
        gate2 = mod_ref[0, :, 5 * D:6 * D]
        out_ref[...] = _layer_norm(alpha * x_ref[...] + gate2 * acc_ref[...], lg_ref[...], lb_ref[...])


def _mlp(x2, mod, lw, T, alpha):
    N, D = x2.shape
    HID = lw["w_mlp_in"].shape[1]
    Bm = mod.shape[0]
    tm = _row_tile(N, T, Bm > 1, 1024)
    th = min(HID, 1024)
    per_b = T // tm
    mod_map = (lambda i, j: (i // per_b, 0, 0)) if Bm > 1 else (lambda i, j: (0, 0, 0))
    return pl.pallas_call(
        functools.partial(_mlp_kernel, D=D, alpha=alpha),
        grid=(N // tm, HID // th),
        in_specs=[pl.BlockSpec((tm, D), lambda i, j: (i, 0)),
                  pl.BlockSpec((1, 1, 6 * D), mod_map),
                  pl.BlockSpec((D, th), lambda i, j: (0, j)),
                  pl.BlockSpec((th, D), lambda i, j: (j, 0)),
                  pl.BlockSpec((1, D), lambda i, j: (0, 0)),
                  pl.BlockSpec((1, D), lambda i, j: (0, 0))],
        out_specs=pl.BlockSpec((tm, D), lambda i, j: (i, 0)),
        out_shape=jax.ShapeDtypeStruct((N, D), F32),
        scratch_shapes=[pltpu.VMEM((tm, D), BF16), pltpu.VMEM((tm, D), F32)],
        compiler_params=_cparams("parallel", "arbitrary"),
        name="mlp",
    )(x2, mod, lw["w_mlp_in"], lw["w_mlp_out"], lw["ln2_g"], lw["ln2_b"])


def _pack_layer_weights(l, D, w_in, gla_w_gate, gla_b_gate, gla_norm_g, s5_d, s5_w_glu,
                        lru_conv_w, lru_conv_b, lru_w_a, lru_b_a, lru_w_i, lru_b_i, lru_lam,
                        w_br_gla, w_br_s5, w_br_lru, w_out, ln1_g, ln1_b, ln2_g, ln2_b,
                        w_mlp_in, w_mlp_out):
    H = GLA_HEADS
    KEY = D // 2
    DK = KEY // H
    W = 3 * D // 2
    CB = 2 * LRU_BLOCK
    NB = W // LRU_BLOCK
    widths = (KEY, KEY, D, D, 2 * GATE_RANK, D, W, W, 3 * D)
    offs = [0]
    for wd in widths:
        offs.append(offs[-1] + wd)
    wl = w_in[l]
    piece = lambda i: wl[:, offs[i]:offs[i + 1]]
    w_pack = jnp.concatenate([piece(0), piece(1), piece(2), piece(3), piece(6), piece(7),
                              piece(8)], axis=1).astype(BF16)
    w_glr = jnp.concatenate([piece(4), jnp.zeros((D, LANES - 2 * GATE_RANK), wl.dtype)], axis=1).astype(BF16)
    w_ut = piece(5).T.astype(BF16)
    wgate = gla_w_gate[l]
    wg = jnp.zeros((H, LANES, 2 * DK), F32)
    wg = wg.at[:, 0:GATE_RANK, 0:DK].set(wgate[0].reshape(GATE_RANK, H, DK).transpose(1, 0, 2))
    wg = wg.at[:, GATE_RANK:2 * GATE_RANK, DK:].set(wgate[1].reshape(GATE_RANK, H, DK).transpose(1, 0, 2))
    bgate = gla_b_gate[l].reshape(2, H, DK).transpose(1, 0, 2).reshape(H, 1, 2 * DK)
    lru_wg = 0.5 * jnp.concatenate([lru_w_a[l], lru_w_i[l]], axis=-1)
    lru_wg = lru_wg.reshape(2, W // CB, CB // LRU_BLOCK, LRU_BLOCK, 2 * LRU_BLOCK).astype(BF16)
    lru_bg = 0.5 * jnp.concatenate([lru_b_a[l].reshape(2, W // CB, 1, CB), lru_b_i[l].reshape(2, W // CB, 1, CB)],
                                   axis=-1)
    return {
        "w_pack": w_pack, "w_glr": w_glr, "w_ut": w_ut, "gla_wg": wg.astype(BF16), "gla_bg": bgate,
        "lru": {"conv_w": lru_conv_w[l].reshape(LRU_CONV, W // CB, CB).transpose(1, 0, 2),
                "conv_b": lru_conv_b[l].reshape(W // CB, 1, CB),
                "wg": lru_wg, "bg": lru_bg, "lam": lru_lam[l].reshape(2, W // CB, 1, CB)},
        "gnorm": gla_norm_g[l].reshape(1, D), "s5_dcol": jnp.tile(s5_d[l].reshape(D // S5_GROUP, 1, S5_GROUP), (1, S5_SUB, 1)).reshape(
            D // S5_GROUP, S5_SUB * S5_GROUP, 1),
        "w_glu": s5_w_glu[l].astype(BF16), "w_br_gla": w_br_gla[l].astype(BF16),
        "w_br_s5": w_br_s5[l].astype(BF16), "w_br_lru": w_br_lru[l].astype(BF16),
        "w_out": w_out[l].astype(BF16),
        "ln1_g": ln1_g[l].reshape(1, D), "ln1_b": ln1_b[l].reshape(1, D),
        "ln2_g": ln2_g[l].reshape(1, D), "ln2_b": ln2_b[l].reshape(1, D),
        "w_mlp_in": w_mlp_in[l].astype(BF16), "w_mlp_out": w_mlp_out[l].astype(BF16),
    }


def _trunk_layer(x2, mod, lw, s5ops, init, B, T, D, alpha):
    N = B * T
    G = D // S5_GROUP
    S = S5_SUB
    R = T // S
    Pn = S5_STATE
    P, glr = _in_proj(x2, mod, lw["w_pack"], lw["w_glr"], T)
    gla0 = s5re0 = s5im0 = lru0 = None
    if init is not None:
        gla0, s5re0, s5im0, lru0 = init
        s5re0 = s5re0.transpose(2, 0, 1, 3).reshape(G, B, 2 * Pn)
        s5im0 = s5im0.transpose(2, 0, 1, 3).reshape(G, B, 2 * Pn)
    o_gla, gla_fin = _gla_core(P, glr, lw["gla_wg"], lw["gla_bg"], gla0, B, T, D)
    if s5re0 is None:
        s5re0 = jnp.zeros((G, B, 2 * Pn), F32)
        s5im0 = jnp.zeros((G, B, 2 * Pn), F32)
    xt = _s5_proj_t(x2, mod, lw["w_ut"], B, T)
    yt, fre, fim = _s5_core_t(xt, lw["s5_dcol"], s5ops, s5re0, s5im0, R, B)
    ys = _s5_rows(yt, B, T, D)
    s5re_fin = fre.reshape(G, B, 2, Pn).transpose(1, 2, 0, 3)
    s5im_fin = fim.reshape(G, B, 2, Pn).transpose(1, 2, 0, 3)
    P3 = P.reshape(B, T, P.shape[1])
    hf, lf = _lru_sweep(P3, lw["lru"], lru0, None, False, B, T, D)
    ylru, lb = _lru_sweep(P3, lw["lru"], lru0, hf, True, B, T, D)
    lru_fin = jnp.stack([lf, lb], axis=1)
    x1 = _merge(x2, mod, o_gla, P, ys, ylru.reshape(N, -1), lw, T, alpha)
    x3 = _mlp(x1, mod, lw, T, alpha)
    return x3, (gla_fin, s5re_fin, s5im_fin, lru_fin)


def kernel(x_prompt, x_sample, state_gla, state_s5_re, state_s5_im, state_lru, c, c_ctx, w_ada, b_ada, w_in, gla_w_gate, gla_b_gate, gla_norm_g, s5_lam_re, s5_lam_im, s5_log_step, s5_b_re, s5_b_im, s5_c_re, s5_c_im, s5_d, s5_w_glu, lru_conv_w, lru_conv_b, lru_w_a, lru_b_a, lru_w_i, lru_b_i, lru_lam, w_br_gla, w_br_s5, w_br_lru, w_out, ln1_g, ln1_b, ln2_g, ln2_b, w_mlp_in, w_mlp_out):
    Bp, Tp, D = x_prompt.shape
    Bs, Ts, _ = x_sample.shape
    L = w_in.shape[0]
    alpha = (2.0 * L) ** 0.25

    n_rows = -(-(Bs + 1) // SUBLANES) * SUBLANES
    cc = jnp.concatenate([c, c_ctx[None], jnp.zeros((n_rows - Bs - 1, D), F32)], axis=0)
    mod = _ada_mod(cc, w_ada, b_ada)
    s5ops_all = _s5_operators(s5_lam_re, s5_lam_im, s5_log_step, s5_b_re, s5_b_im, s5_c_re, s5_c_im)

    xp = x_prompt.reshape(Bp * Tp, D)
    xs = _add_pos(x_sample, _grid_pos_table(Ts, D)).reshape(Bs * Ts, D)
    fins = []
    for l in range(L):
        lw = _pack_layer_weights(l, D, w_in, gla_w_gate, gla_b_gate, gla_norm_g, s5_d, s5_w_glu,
                                 lru_conv_w, lru_conv_b, lru_w_a, lru_b_a, lru_w_i, lru_b_i, lru_lam,
                                 w_br_gla, w_br_s5, w_br_lru, w_out, ln1_g, ln1_b, ln2_g, ln2_b,
                                 w_mlp_in, w_mlp_out)
        s5ops = tuple(a[l] for a in s5ops_all)
        mod_ctx = mod[l, Bs:Bs + 1].reshape(1, 1, 6 * D)
        mod_lat = mod[l, :Bs].reshape(Bs, 1, 6 * D)
        xp, fin = _trunk_layer(xp, mod_ctx, lw, s5ops, None, Bp, Tp, D, alpha)
        fins.append(fin)
        cache = (state_gla[:, l], state_s5_re[:, l], state_s5_im[:, l], state_lru[:, l])
        xs, _ = _trunk_layer(xs, mod_lat, lw, s5ops, cache, Bs, Ts, D, alpha)
    sdt = x_prompt.dtype
    new_states = tuple(jnp.stack([f[i] for f in fins], axis=1).astype(sdt) for i in range(4))
    return (xp.reshape(Bp, Tp, D), xs.reshape(Bs, Ts, D)) + new_states
```

```python
import functools
import math

import jax
import jax.numpy as jnp
from jax import lax
from jax.experimental import pallas as pl
from jax.experimental.pallas import tpu as pltpu

F32 = jnp.float32
BF16 = jnp.bfloat16
HIGHEST = lax.Precision.HIGHEST

LANES = 128
SUBLANES = 8
VMEM_LIMIT = 56 * 1024 * 1024

GRID_W = 64
GLA_HEADS = 4
GATE_RANK = 16
GLA_TAU = 16.0
GLA_CHUNK = 64
GLA_BLOCK = 256
S5_GROUP = 16
S5_STATE = 64
S5_SUB = 16
S5_PROJ_ROWS = 1024
LRU_BLOCK = 128
LRU_CONV = 4
LRU_C = 8.0
LN_EPS = 1e-5
F32_TINY = 1.1754944e-38
SCAN_UNROLL = 8


def _cparams(*sem):
    return pltpu.CompilerParams(dimension_semantics=sem, vmem_limit_bytes=VMEM_LIMIT)


def _dot(a, b):
    return jnp.dot(a.astype(BF16), b.astype(BF16), preferred_element_type=F32)


def _dot_nt(a, b):
    return lax.dot_general(a.astype(BF16), b.astype(BF16), (((1,), (1,)), ((), ())),
                           preferred_element_type=F32)


def _dot_tn(a, b):
    return lax.dot_general(a.astype(BF16), b.astype(BF16), (((0,), (0,)), ((), ())),
                           preferred_element_type=F32)


def _dot_f32(a, b):
    return jnp.dot(a, b, precision=HIGHEST, preferred_element_type=F32)


def _layer_norm(z, g, b):
    mu = jnp.mean(z, axis=-1, keepdims=True)
    zc = z - mu
    var = jnp.mean(zc * zc, axis=-1, keepdims=True)
    return zc * lax.rsqrt(var + LN_EPS) * g + b


def _sigmoid(x):
    return 0.5 * jnp.tanh(0.5 * x) + 0.5


def _silu(x):
    return x * _sigmoid(x)


def _ada_kernel(cc_ref, w_ref, b_ref, o_ref):
    s = _silu(cc_ref[...])
    o_ref[0] = _dot_f32(s, w_ref[0]) + b_ref[0]


def _ada_mod(cc, w_ada, b_ada):
    L, D, D6 = w_ada.shape
    R = cc.shape[0]
    tn = D6 // 4
    return pl.pallas_call(
        _ada_kernel,
        grid=(L, D6 // tn),
        in_specs=[pl.BlockSpec((R, D), lambda l, j: (0, 0)),
                  pl.BlockSpec((1, D, tn), lambda l, j: (l, 0, j)),
                  pl.BlockSpec((1, 1, tn), lambda l, j: (l, 0, j))],
        out_specs=pl.BlockSpec((1, R, tn), lambda l, j: (l, 0, j)),
        out_shape=jax.ShapeDtypeStruct((L, R, D6), F32),
        compiler_params=_cparams("parallel", "parallel"),
        name="ada_mod",
    )(cc, w_ada, b_ada.reshape(L, 1, D6))


def _addpos_kernel(x_ref, p_ref, o_ref):
    o_ref[0] = x_ref[0] + p_ref[...]


def _add_pos(x, pos):
    B, T, D = x.shape
    tt = min(T, 512)
    return pl.pallas_call(
        _addpos_kernel,
        grid=(T // tt, B),
        in_specs=[pl.BlockSpec((1, tt, D), lambda t, b: (b, t, 0)),
                  pl.BlockSpec((tt, D), lambda t, b: (t, 0))],
        out_specs=pl.BlockSpec((1, tt, D), lambda t, b: (b, t, 0)),
        out_shape=jax.ShapeDtypeStruct((B, T, D), F32),
        compiler_params=_cparams("parallel", "parallel"),
        name="add_pos",
    )(x, pos)


def _grid_pos_table(n_tokens, dim):
    rows = n_tokens // GRID_W
    quarter = dim // 4
    omega = 1.0 / (10000.0 ** (jnp.arange(quarter, dtype=F32) / quarter))
    r = jnp.arange(rows, dtype=F32)[:, None, None] * omega
    cl = jnp.arange(GRID_W, dtype=F32)[None, :, None] * omega
    shape = (rows, GRID_W, quarter)
    emb = jnp.concatenate([jnp.broadcast_to(jnp.sin(r), shape), jnp.broadcast_to(jnp.cos(r), shape),
                           jnp.broadcast_to(jnp.sin(cl), shape), jnp.broadcast_to(jnp.cos(cl), shape)], axis=-1)
    return emb.reshape(rows * GRID_W, dim)


def _inproj_kernel(x_ref, mod_ref, w_ref, wglr_ref, o_ref, glr_ref, h_ref, *, D):
    @pl.when(pl.program_id(1) == 0)
    def _():
        shift = mod_ref[0, :, 0:D]
        scale = mod_ref[0, :, D:2 * D]
        h = (x_ref[...] * (1.0 + scale) + shift).astype(BF16)
        h_ref[...] = h
        glr_ref[...] = jnp.dot(h, wglr_ref[...], preferred_element_type=F32)

    o_ref[...] = jnp.dot(h_ref[...], w_ref[...], preferred_element_type=F32)


def _row_tile(N, T, per_batch_mod, want):
    tm = min(want, T if per_batch_mod else N)
    while N % tm or (per_batch_mod and T % tm):
        tm //= 2
    return tm


def _in_proj(x2, mod, w_pack, w_glr, T):
    N, D = x2.shape
    NC = w_pack.shape[1]
    Bm = mod.shape[0]
    tm = _row_tile(N, T, Bm > 1, 1024)
    tn = D
    per_b = T // tm
    mod_map = (lambda i, j: (i // per_b, 0, 0)) if Bm > 1 else (lambda i, j: (0, 0, 0))
    return pl.pallas_call(
        functools.partial(_inproj_kernel, D=D),
        grid=(N // tm, NC // tn),
        in_specs=[pl.BlockSpec((tm, D), lambda i, j: (i, 0)),
                  pl.BlockSpec((1, 1, 6 * D), mod_map),
                  pl.BlockSpec((D, tn), lambda i, j: (0, j)),
                  pl.BlockSpec((D, LANES), lambda i, j: (0, 0))],
        out_specs=[pl.BlockSpec((tm, tn), lambda i, j: (i, j)),
                   pl.BlockSpec((tm, LANES), lambda i, j: (i, 0))],
        out_shape=[jax.ShapeDtypeStruct((N, NC), F32), jax.ShapeDtypeStruct((N, LANES), F32)],
        scratch_shapes=[pltpu.VMEM((tm, D), BF16)],
        compiler_params=_cparams("parallel", "arbitrary"),
        name="in_proj",
    )(x2, mod, w_pack, w_glr)


def _gla_kernel(*refs, T, DK, DV, has_init):
    if has_init:
        (q_ref, k_ref, v_ref, glr_ref, wg_ref, bg_ref, s0_ref,
         o_ref, sfin_ref, qb_ref, keb_ref, db_ref, sf_ref, sb_ref) = refs
    else:
        (q_ref, k_ref, v_ref, glr_ref, wg_ref, bg_ref,
         o_ref, sfin_ref, qb_ref, keb_ref, db_ref, sf_ref, sb_ref) = refs
    C = GLA_CHUNK
    BLK = GLA_BLOCK
    NS = BLK // C
    nblk = T // BLK
    qscale = DK ** -0.5
    shift = C.bit_length() - 1
    row = lax.broadcasted_iota(jnp.int32, (BLK, BLK), 0)
    col = lax.broadcasted_iota(jnp.int32, (BLK, BLK), 1)
    same = lax.shift_right_logical(row, shift) == lax.shift_right_logical(col, shift)
    lower = jnp.logical_and(same, row >= col)
    upper = jnp.logical_and(same, row <= col)
    tri = jnp.where(lower, 1.0, 0.0).astype(BF16)
    chunk_of_row = lax.shift_right_logical(lax.broadcasted_iota(jnp.int32, (BLK, DK), 0), shift)
    wg = wg_ref[0]
    bg = bg_ref[0]

    def chunk_cols(ke):
        return jnp.concatenate([jnp.where(chunk_of_row == c, ke, 0.0) for c in range(NS)], axis=1).astype(BF16)

    def per_chunk_last(x, r):
        return jnp.concatenate([jnp.broadcast_to(x[c * C + r:c * C + r + 1, :], (C, x.shape[1]))
                                for c in range(NS)], axis=0)

    if has_init:
        sf_ref[...] = s0_ref[0, 0, 0].T
        sb_ref[...] = s0_ref[0, 1, 0].T
    else:
        sf_ref[...] = jnp.zeros((DV, DK), F32)
        sb_ref[...] = jnp.zeros((DV, DK), F32)

    def fwd_body(i, carry):
        r0 = pl.multiple_of(i * BLK, BLK)
        rows = pl.ds(r0, BLK)
        qc = q_ref[rows, :] * qscale
        kc = k_ref[rows, :]
        vc = v_ref[rows, :]
        logits = _dot(glr_ref[rows, :], wg) + bg
        la = (jnp.minimum(logits, 0.0) - jnp.log(1.0 + jnp.exp(-jnp.abs(logits)))) * (1.0 / GLA_TAU)
        p1 = la.astype(BF16)
        p2 = (la - p1.astype(F32)).astype(BF16)
        pre = jnp.dot(tri, p1, preferred_element_type=F32) + jnp.dot(tri, p2, preferred_element_type=F32)
        tot = per_chunk_last(pre, C - 1)
        cum_f = pre[:, :DK]
        last_f = tot[:, :DK]
        cum_b = tot[:, DK:] - pre[:, DK:] + la[:, DK:]
        last_b = tot[:, DK:]
        q_f = qc * jnp.exp(cum_f)
        k_f = kc * jnp.exp(-cum_f)
        ke_f = kc * jnp.exp(last_f - cum_f)
        q_b = qc * jnp.exp(cum_b)
        k_b = kc * jnp.exp(-cum_b)
        ke_b = kc * jnp.exp(last_b - cum_b)
        sc = jnp.where(lower, _dot_nt(q_f, k_f), 0.0) + jnp.where(upper, _dot_nt(q_b, k_b), 0.0)
        o_blk = _dot(sc, vc)
        u_f = jnp.dot(vc.T.astype(BF16), chunk_cols(ke_f), preferred_element_type=F32)
        d_f = jnp.exp(last_f)
        s = sf_ref[...]
        for c in range(NS):
            rc = slice(c * C, (c + 1) * C)
            o_ref[pl.ds(r0 + c * C, C), :] = o_blk[rc] + _dot_nt(q_f[rc], s)
            s = s * d_f[c * C:c * C + 1, :] + u_f[:, c * DK:(c + 1) * DK]
        sf_ref[...] = s
        qb_ref[rows, :] = q_b
        keb_ref[rows, :] = ke_b
        d_b = jnp.exp(last_b)
        db_ref[pl.ds(i, 1), :] = jnp.concatenate([d_b[c * C:c * C + 1, :] for c in range(NS)], axis=1)
        return carry

    lax.fori_loop(0, nblk, fwd_body, 0, unroll=min(8, nblk))

    def bwd_body(j, carry):
        i = nblk - 1 - j
        r0 = pl.multiple_of(i * BLK, BLK)
        rows = pl.ds(r0, BLK)
        q_b = qb_ref[rows, :]
        u_b = jnp.dot(v_ref[rows, :].T.astype(BF16), chunk_cols(keb_ref[rows, :]), preferred_element_type=F32)
        d_b = db_ref[pl.ds(i, 1), :]
        s = sb_ref[...]
        for c in reversed(range(NS)):
            rc = pl.ds(r0 + c * C, C)
            o_ref[rc, :] = o_ref[rc, :] + _dot_nt(q_b[c * C:(c + 1) * C], s)
            s = s * d_b[:, c * DK:(c + 1) * DK] + u_b[:, c * DK:(c + 1) * DK]
        sb_ref[...] = s
        return carry

    lax.fori_loop(0, nblk, bwd_body, 0, unroll=min(8, nblk))

    sfin_ref[0, 0, 0] = sf_ref[...].T
    sfin_ref[0, 1, 0] = sb_ref[...].T


def _gla_core(P, glr, wg, bg, s0, B, T, D):
    H = GLA_HEADS
    DK = D // 2 // H
    DV = D // H
    N = B * T
    has_init = s0 is not None
    kcol = (D // 2) // DK
    vcol = D // DV
    in_specs = [pl.BlockSpec((T, DK), lambda b, h: (b, h)),
                pl.BlockSpec((T, DK), lambda b, h: (b, kcol + h)),
                pl.BlockSpec((T, DV), lambda b, h: (b, vcol + h)),
                pl.BlockSpec((T, LANES), lambda b, h: (b, 0)),
                pl.BlockSpec((1, LANES, 2 * DK), lambda b, h: (h, 0, 0)),
                pl.BlockSpec((1, 1, 2 * DK), lambda b, h: (h, 0, 0))]
    args = [P, P, P, glr, wg, bg]
    if has_init:
        in_specs.append(pl.BlockSpec((1, 2, 1, DK, DV), lambda b, h: (b, 0, h, 0, 0)))
        args.append(s0)
    o, sfin = pl.pallas_call(
        functools.partial(_gla_kernel, T=T, DK=DK, DV=DV, has_init=has_init),
        grid=(B, H),
        in_specs=in_specs,
        out_specs=[pl.BlockSpec((T, DV), lambda b, h: (b, h)),
                   pl.BlockSpec((1, 2, 1, DK, DV), lambda b, h: (b, 0, h, 0, 0))],
        out_shape=[jax.ShapeDtypeStruct((N, D), F32),
                   jax.ShapeDtypeStruct((B, 2, H, DK, DV), F32)],
        scratch_shapes=[pltpu.VMEM((T, DK), F32), pltpu.VMEM((T, DK), F32),
                        pltpu.VMEM((T // GLA_BLOCK, (GLA_BLOCK // GLA_CHUNK) * DK), F32),
                        pltpu.VMEM((DV, DK), F32), pltpu.VMEM((DV, DK), F32)],
        compiler_params=_cparams("parallel", "parallel"),
        name="gla_core",
    )(*args)
    return o, sfin


def _s5prep_kernel(lr_ref, li_ref, ls_ref, btr_ref, bti_ref, cr_ref, ci_ref,
                   kt_ref, wre_ref, wim_ref, cpre_ref, cpimn_ref, pre_ref, pim_ref):
    I = S5_GROUP
    lr = lr_ref[0, 0]
    li = li_ref[0, 0]
    step = jnp.exp(ls_ref[0, 0])
    mag = jnp.exp(lr * step)
    ang = li * step
    a_re = mag * jnp.cos(ang)
    a_im = mag * jnp.sin(ang)
    den = lr * lr + li * li
    nr = a_re - 1.0
    f_re = (nr * lr + a_im * li) / den
    f_im = (a_im * lr - nr * li) / den
    btr = btr_ref[0]
    bti = bti_ref[0]
    bb_re = f_re[:, None, :] * btr - f_im[:, None, :] * bti
    bb_im = f_re[:, None, :] * bti + f_im[:, None, :] * btr
    c_re = cr_ref[0]
    c_im = ci_ref[0]
    p_re = jnp.ones_like(lr)
    p_im = jnp.zeros_like(lr)
    batched_nt = (((2,), (2,)), ((0,), (0,)))
    for e in range(S5_SUB + 1):
        pr = p_re[:, None, :]
        pi = p_im[:, None, :]
        cp_re = c_re * pr - c_im * pi
        cp_im = c_re * pi + c_im * pr
        cpre_ref[0, 0, :, e * I:(e + 1) * I, :] = cp_re
        cpimn_ref[0, 0, :, e * I:(e + 1) * I, :] = -cp_im
        if e < S5_SUB:
            wre_ref[0, 0, :, e * I:(e + 1) * I, :] = pr * bb_re - pi * bb_im
            wim_ref[0, 0, :, e * I:(e + 1) * I, :] = pr * bb_im + pi * bb_re
            kt_ref[0, 0, :, e * I:(e + 1) * I, :] = (
                lax.dot_general(cp_re, bb_re, batched_nt, precision=HIGHEST, preferred_element_type=F32)
                - lax.dot_general(cp_im, bb_im, batched_nt, precision=HIGHEST, preferred_element_type=F32))
            p_re, p_im = p_re * a_re - p_im * a_im, p_re * a_im + p_im * a_re
    pre_ref[0, 0] = p_re
    pim_ref[0, 0] = p_im


def _s5_operators(lam_re, lam_im, log_step, b_re, b_im, c_re, c_im):
    L, _, G, Pn = lam_re.shape
    I = S5_GROUP
    S = S5_SUB
    ls = jnp.broadcast_to(log_step[..., None], lam_re.shape)
    btr = jnp.swapaxes(b_re, -1, -2)
    bti = jnp.swapaxes(b_im, -1, -2)
    Gb = SUBLANES
    lam_spec = pl.BlockSpec((1, 1, Gb, Pn), lambda l, d, g: (l, d, g, 0))
    gip_spec = pl.BlockSpec((1, Gb, I, Pn), lambda l, d, g: (l, g, 0, 0))

    def out_spec(rows, last):
        return pl.BlockSpec((1, 1, Gb, rows, last), lambda l, d, g: (l, d, g, 0, 0))

    kt, wre, wim, cpre, cpimn, pre, pim = pl.pallas_call(
        _s5prep_kernel,
        grid=(L, 2, G // Gb),
        in_specs=[lam_spec, lam_spec, lam_spec, gip_spec, gip_spec, gip_spec, gip_spec],
        out_specs=[out_spec(S * I, I), out_spec(S * I, Pn), out_spec(S * I, Pn),
                   out_spec((S + 1) * I, Pn), out_spec((S + 1) * I, Pn), lam_spec, lam_spec],
        out_shape=[jax.ShapeDtypeStruct((L, 2, G, S * I, I), F32),
                   jax.ShapeDtypeStruct((L, 2, G, S * I, Pn), F32),
                   jax.ShapeDtypeStruct((L, 2, G, S * I, Pn), F32),
                   jax.ShapeDtypeStruct((L, 2, G, (S + 1) * I, Pn), F32),
                   jax.ShapeDtypeStruct((L, 2, G, (S + 1) * I, Pn), F32),
                   jax.ShapeDtypeStruct((L, 2, G, Pn), F32),
                   jax.ShapeDtypeStruct((L, 2, G, Pn), F32)],
        compiler_params=_cparams("parallel", "parallel", "parallel"),
        name="s5_prep",
    )(lam_re, lam_im, ls, btr, bti, c_re, c_im)

    kt = kt.reshape(L, 2, G, S, I, I)
    s_idx = jnp.arange(S)[:, None]
    t_idx = jnp.arange(S)[None, :]

    def toeplitz(k, lag, valid):
        m = k[:, :, jnp.clip(lag, 0, S - 1)]
        m = jnp.where(valid[None, None, :, :, None, None], m, 0.0)
        return m.transpose(0, 1, 3, 4, 2, 5).reshape(L, G, S * I, S * I)

    mf = toeplitz(kt[:, 0], t_idx - s_idx, t_idx >= s_idx)
    mb = toeplitz(kt[:, 1], s_idx - t_idx, s_idx >= t_idx)

    def by_pos(w, reverse):
        w = w.reshape(L, G, S, I, Pn)
        if reverse:
            w = w[:, :, ::-1]
        return w.reshape(L, G, S * I, Pn).swapaxes(-1, -2)

    wall = jnp.concatenate([by_pos(wre[:, 0], True), by_pos(wre[:, 1], False),
                            by_pos(wim[:, 0], True), by_pos(wim[:, 1], False)], axis=2)

    def readout(cp):
        cp = cp.reshape(L, 2, G, S + 1, I, Pn)
        f = cp[:, 0, :, 1:]
        b = cp[:, 1, :, 1:][:, :, ::-1]
        return jnp.concatenate([f.reshape(L, G, S * I, Pn), b.reshape(L, G, S * I, Pn)], axis=-1)

    vre = readout(cpre)
    vim = readout(cpimn)
    are = jnp.concatenate([pre[:, 0], pre[:, 1]], axis=-1)[:, :, None, :]
    aim = jnp.concatenate([pim[:, 0], pim[:, 1]], axis=-1)[:, :, None, :]
    return mf, mb, wall, vre, vim, are, aim


def _s5projt_kernel(x_ref, mod_ref, wt_ref, o_ref, *, D, R):
    Bm = mod_ref.shape[0]
    span = x_ref.shape[0] // Bm
    parts = []
    for b in range(Bm):
        shift = mod_ref[b, :, 0:D]
        scale = mod_ref[b, :, D:2 * D]
        parts.append((x_ref[b * span:(b + 1) * span, :] * (1.0 + scale) + shift).astype(BF16))
    h = parts[0] if Bm == 1 else jnp.concatenate(parts, axis=0)
    ut = lax.dot_general(wt_ref[...], h, (((1,), (1,)), ((), ())), preferred_element_type=F32)
    o_ref[...] = ut.reshape(o_ref.shape).astype(o_ref.dtype)


def _s5_proj_t(x2, mod, w_ut, B, T):
    N, D = x2.shape
    S = S5_SUB
    R = T // S
    G = D // S5_GROUP
    Bm = mod.shape[0]
    xv = x2.reshape(B * R, S * D)
    rows = B * R
    while rows > S5_PROJ_ROWS and rows % 2 == 0 and (rows // 2) % R == 0 and (rows // 2) % LANES == 0:
        rows //= 2
    nrb = B * R // rows
    mods = max(1, Bm * rows // (B * R))
    mod_spec = pl.BlockSpec((mods, 1, 6 * D), lambda i, s: (i if Bm > 1 else 0, 0, 0))
    return pl.pallas_call(
        functools.partial(_s5projt_kernel, D=D, R=R),
        grid=(nrb, S),
        in_specs=[pl.BlockSpec((rows, D), lambda i, s: (i, s)),
                  mod_spec,
                  pl.BlockSpec((D, D), lambda i, s: (0, 0))],
        out_specs=pl.BlockSpec((G, S5_GROUP, rows), lambda i, s: (0, s, i)),
        out_shape=jax.ShapeDtypeStruct((G, S * S5_GROUP, B * R), F32),
        compiler_params=_cparams("parallel", "parallel"),
        name="s5_proj_t",
    )(xv, mod, w_ut)


def _s5t_kernel(xt_ref, dcol_ref, mft_ref, mbt_ref, wt_ref, vret_ref, vimt_ref, are_ref, aim_ref,
                s0re_ref, s0im_ref, yt_ref, fre_ref, fim_ref,
                zt_ref, zre_ref, zim_ref, hfre_ref, hfim_ref, hbre_ref, hbim_ref, *, R, B, chunk_major):
    Pn = S5_STATE
    xt32 = xt_ref[0]
    xt = xt32.astype(BF16)
    mt = (mft_ref[0] + mbt_ref[0]).astype(BF16)
    yt_ref[0] = jnp.dot(mt, xt, preferred_element_type=F32) + dcol_ref[0] * xt32
    zt_ref[...] = jnp.dot(wt_ref[0].astype(BF16), xt, preferred_element_type=F32)
    if chunk_major:
        for b in range(B):
            zb = zt_ref[:, b * R:(b + 1) * R].T
            zre_ref[pl.ds(b, R, stride=B), :] = zb[:, :2 * Pn]
            zim_ref[pl.ds(b, R, stride=B), :] = zb[:, 2 * Pn:]
        step_rows = lambda r: pl.ds(pl.multiple_of(r * B, B), B)
    else:
        z = zt_ref[...].T
        zre_ref[...] = z[:, :2 * Pn]
        zim_ref[...] = z[:, 2 * Pn:]
        step_rows = lambda r: pl.ds(r, B, stride=R)
    a_re = are_ref[0]
    a_im = aim_ref[0]
    h0 = (s0re_ref[0], s0im_ref[0])

    def advance(h, rows):
        hr, hi = h
        return (hr * a_re - hi * a_im + zre_ref[rows, :], hr * a_im + hi * a_re + zim_ref[rows, :])

    def step(r, carry):
        hf, hb = carry
        rows_f = step_rows(r)
        rows_b = step_rows(R - 1 - r)
        hfre_ref[rows_f, :] = hf[0]
        hfim_ref[rows_f, :] = hf[1]
        hbre_ref[rows_b, :] = hb[0]
        hbim_ref[rows_b, :] = hb[1]
        return advance(hf, rows_f), advance(hb, rows_b)

    hf, hb = lax.fori_loop(0, R, step, (h0, h0), unroll=SCAN_UNROLL)

    def own_half(f, b):
        return jnp.where(lax.broadcasted_iota(jnp.int32, f.shape, 1) < Pn, f, b)

    fre_ref[0] = own_half(hf[0], hb[0])
    fim_ref[0] = own_half(hf[1], hb[1])
    vret = vret_ref[0]
    vimt = vimt_ref[0]
    if chunk_major:
        for b in range(B):
            sel = pl.ds(b, R, stride=B)
            hre_b = own_half(hfre_ref[sel, :], hbre_ref[sel, :])
            him_b = own_half(hfim_ref[sel, :], hbim_ref[sel, :])
            cols = slice(b * R, (b + 1) * R)
            yt_ref[0, :, cols] = yt_ref[0, :, cols] + _dot_nt(vret, hre_b) + _dot_nt(vimt, him_b)
    else:
        hre = own_half(hfre_ref[...], hbre_ref[...])
        him = own_half(hfim_ref[...], hbim_ref[...])
        yt_ref[0] = yt_ref[0] + _dot_nt(vret, hre) + _dot_nt(vimt, him)


def _s5_core_t(xt, dcol, ops, s0re, s0im, R, B):
    G, K, BR = xt.shape
    mft, mbt, wt, vret, vimt, are, aim = ops
    P2 = 2 * S5_STATE
    chunk_major = B == SUBLANES and R % LANES == 0
    gspec = lambda shape: pl.BlockSpec((1,) + shape, lambda g: (g, 0, 0))
    return pl.pallas_call(
        functools.partial(_s5t_kernel, R=R, B=B, chunk_major=chunk_major),
        grid=(G,),
        in_specs=[gspec((K, BR)), gspec((K, 1)), gspec((K, K)), gspec((K, K)), gspec((K, K)),
                  gspec((K, P2)), gspec((K, P2)), gspec((1, P2)), gspec((1, P2)),
                  gspec((B, P2)), gspec((B, P2))],
        out_specs=[gspec((K, BR)), gspec((B, P2)), gspec((B, P2))],
        out_shape=[jax.ShapeDtypeStruct((G, K, BR), F32),
                   jax.ShapeDtypeStruct((G, B, P2), F32),
                   jax.ShapeDtypeStruct((G, B, P2), F32)],
        scratch_shapes=[pltpu.VMEM((K, BR), F32)] + [pltpu.VMEM((BR, P2), F32) for _ in range(6)],
        compiler_params=_cparams("parallel"),
        name="s5_core_t",
    )(xt, dcol, mft, mbt, wt, vret, vimt, are, aim, s0re, s0im)


def _s5rows_kernel(yt_ref, o_ref):
    I = S5_GROUP
    nc = yt_ref.shape[-1]
    for t in range(S5_SUB):
        blk = yt_ref[:, t * I:(t + 1) * I, :]
        o_ref[pl.ds(t, nc, stride=S5_SUB), :] = blk.reshape(LANES, nc).T


def _s5_rows(yt, B, T, D):
    G = D // S5_GROUP
    BR = B * T // S5_SUB
    gpt = LANES // S5_GROUP
    nc = 2 * LANES if BR % (2 * LANES) == 0 else LANES
    return pl.pallas_call(
        _s5rows_kernel,
        grid=(BR // nc, G // gpt),
        in_specs=[pl.BlockSpec((gpt, S5_SUB * S5_GROUP, nc), lambda i, j: (j, 0, i))],
        out_specs=pl.BlockSpec((nc * S5_SUB, LANES), lambda i, j: (i, j)),
        out_shape=jax.ShapeDtypeStruct((B * T, D), F32),
        compiler_params=_cparams("parallel", "parallel"),
        name="s5_rows",
    )(yt)


def _lru_kernel(*refs, Tt, CB, reverse, has_init):
    refs = list(refs)
    lx_ref, prev_ref, next_ref, cw_ref, cb_ref, wg_ref, bgate_ref, lam_ref = refs[:8]
    pos = 8
    h0_ref = None
    if has_init:
        h0_ref = refs[pos]
        pos += 1
    if reverse:
        hf_ref, ly_ref = refs[pos:pos + 2]
        pos += 2
    out_ref, fin_ref, x_scr, a_scr, b_scr, h_scr, carry_ref = refs[pos:pos + 7]
    NBk = CB // LRU_BLOCK
    k = pl.program_id(2)
    nk = pl.num_programs(2)
    first_tile = k == 0

    @pl.when(first_tile)
    def _():
        if has_init:
            carry_ref[...] = h0_ref[...]
        else:
            carry_ref[...] = jnp.zeros((SUBLANES, CB), F32)

    tpos = (nk - 1 - k) if reverse else k
    has_prev = tpos > 0
    has_next = tpos < nk - 1
    cw = cw_ref[0]
    cbias = cb_ref[0]
    bgate = bgate_ref[0]
    lam = lam_ref[0]
    half_cfac = (-0.5 * LRU_C) * (jnp.maximum(-lam, 0.0) + jnp.log1p(jnp.exp(-jnp.abs(lam))))

    left = LRU_CONV // 2
    for b in range(SUBLANES):
        cur = lx_ref[b]
        pv = jnp.where(has_prev, prev_ref[b], 0.0)
        nx = jnp.where(has_next, next_ref[b], 0.0)
        for c in range(NBk):
            sl = slice(c * LRU_BLOCK, (c + 1) * LRU_BLOCK)
            x_scr[c, pl.ds(left * SUBLANES + b, Tt, stride=SUBLANES), :] = cur[:, sl]
            for q in range(left):
                x_scr[c, pl.ds(q * SUBLANES + b, 1), :] = pv[SUBLANES - left + q:SUBLANES - left + q + 1, sl]
            for q in range(LRU_CONV - 1 - left):
                x_scr[c, pl.ds((Tt + left + q) * SUBLANES + b, 1), :] = nx[q:q + 1, sl]

    RC = min(Tt * SUBLANES, 512)
    for c in range(NBk):
        sl = slice(c * LRU_BLOCK, (c + 1) * LRU_BLOCK)
        bias_c = jnp.concatenate([bgate[:, sl], bgate[:, CB + c * LRU_BLOCK: CB + (c + 1) * LRU_BLOCK]], axis=1)
        for r0 in range(0, Tt * SUBLANES, RC):
            xc = cbias[:, sl] + sum(cw[j:j + 1, sl] * x_scr[c, pl.ds(r0 + j * SUBLANES, RC), :]
                                    for j in range(LRU_CONV))
            th = jnp.tanh(_dot(xc, wg_ref[0, c]) + bias_c)
            log_a = half_cfac[:, sl] * th[:, :LRU_BLOCK] + half_cfac[:, sl]
            a = jnp.exp(log_a)
            om = -jnp.tanh(log_a) * (1.0 + a * a)
            root = om * lax.rsqrt(jnp.maximum(om, F32_TINY))
            bt = root * ((0.5 * th[:, LRU_BLOCK:] + 0.5) * xc)
            a_scr[c, pl.ds(r0, RC), :] = a
            b_scr[c, pl.ds(r0, RC), :] = bt

    def step(s, h):
        t = (Tt - 1 - s) if reverse else s
        rows = pl.ds(pl.multiple_of(t * SUBLANES, SUBLANES), SUBLANES)
        new = []
        for c in range(NBk):
            hc = a_scr[c, rows, :] * h[c] + b_scr[c, rows, :]
            h_scr[c, rows, :] = hc
            new.append(hc)
        return tuple(new)

    h_init = tuple(carry_ref[:, c * LRU_BLOCK:(c + 1) * LRU_BLOCK] for c in range(NBk))
    h_last = lax.fori_loop(0, Tt, step, h_init, unroll=SCAN_UNROLL)
    for c in range(NBk):
        carry_ref[:, c * LRU_BLOCK:(c + 1) * LRU_BLOCK] = h_last[c]
        fin_ref[:, c * LRU_BLOCK:(c + 1) * LRU_BLOCK] = h_last[c]

    for b in range(SUBLANES):
        hb = jnp.concatenate([h_scr[c, pl.ds(b, Tt, stride=SUBLANES), :] for c in range(NBk)], axis=1)
        if reverse:
            out_ref[b] = (hf_ref[b] + hb) * jax.nn.gelu(ly_ref[b])
        else:
            out_ref[b] = hb


def _lru_sweep(P3, lw, h0, hf, reverse, B, T, D):
    W = 3 * D // 2
    CB = 2 * LRU_BLOCK
    Tt = min(T, 256)
    nk = T // Tt
    nb8 = Tt // SUBLANES
    lxcol = (D // 2 + D // 2 + D + D) // CB
    lycol = lxcol + W // CB
    d = 1 if reverse else 0
    has_init = h0 is not None

    def tmap(k):
        return (nk - 1 - k) if reverse else k

    in_specs = [
        pl.BlockSpec((SUBLANES, Tt, CB), lambda g, j, k: (g, tmap(k), lxcol + j)),
        pl.BlockSpec((SUBLANES, SUBLANES, CB),
                     lambda g, j, k: (g, jnp.maximum(tmap(k) * nb8 - 1, 0), lxcol + j)),
        pl.BlockSpec((SUBLANES, SUBLANES, CB),
                     lambda g, j, k: (g, jnp.minimum((tmap(k) + 1) * nb8, T // SUBLANES - 1), lxcol + j)),
        pl.BlockSpec((1, LRU_CONV, CB), lambda g, j, k: (j, 0, 0)),
        pl.BlockSpec((1, 1, CB), lambda g, j, k: (j, 0, 0)),
        pl.BlockSpec((1, CB // LRU_BLOCK, LRU_BLOCK, 2 * LRU_BLOCK), lambda g, j, k: (j, 0, 0, 0)),
        pl.BlockSpec((1, 1, 2 * CB), lambda g, j, k: (j, 0, 0)),
        pl.BlockSpec((1, 1, CB), lambda g, j, k: (j, 0, 0)),
    ]
    args = [P3, P3, P3, lw["conv_w"], lw["conv_b"], lw["wg"][d], lw["bg"][d], lw["lam"][d]]
    if has_init:
        in_specs.append(pl.BlockSpec((SUBLANES, CB), lambda g, j, k: (g, j)))
        args.append(h0[:, d])
    if reverse:
        in_specs.append(pl.BlockSpec((SUBLANES, Tt, CB), lambda g, j, k: (g, tmap(k), j)))
        in_specs.append(pl.BlockSpec((SUBLANES, Tt, CB), lambda g, j, k: (g, tmap(k), lycol + j)))
        args += [hf, P3]
    out, fin = pl.pallas_call(
        functools.partial(_lru_kernel, Tt=Tt, CB=CB, reverse=reverse, has_init=has_init),
        grid=(B // SUBLANES, W // CB, nk),
        in_specs=in_specs,
        out_specs=[pl.BlockSpec((SUBLANES, Tt, CB), lambda g, j, k: (g, tmap(k), j)),
                   pl.BlockSpec((SUBLANES, CB), lambda g, j, k: (g, j))],
        out_shape=[jax.ShapeDtypeStruct((B, T, W), F32), jax.ShapeDtypeStruct((B, W), F32)],
        scratch_shapes=[pltpu.VMEM((CB // LRU_BLOCK, (Tt + LRU_CONV) * SUBLANES, LRU_BLOCK), F32)]
        + [pltpu.VMEM((CB // LRU_BLOCK, Tt * SUBLANES, LRU_BLOCK), F32) for _ in range(3)]
        + [pltpu.VMEM((SUBLANES, CB), F32)],
        compiler_params=_cparams("parallel", "parallel", "arbitrary"),
        name="lru_bwd" if reverse else "lru_fwd",
    )(*args)
    return out, fin


def _merge_kernel(x_ref, mod_ref, o_ref, r_ref, ys_ref, yl_ref, gg_ref, gs_ref, gl_ref,
                  gn_ref, wglu_ref, wbg_ref, wbs_ref, wbl_ref, wo_ref, lg_ref, lb_ref,
                  out_ref, *, D, alpha):
    DV = D // GLA_HEADS
    gn = gn_ref[...]
    parts = []
    for h in range(GLA_HEADS):
        sl = slice(h * DV, (h + 1) * DV)
        o = o_ref[:, sl]
        mu = jnp.mean(o, axis=-1, keepdims=True)
        oc = o - mu
        var = jnp.mean(oc * oc, axis=-1, keepdims=True)
        parts.append(oc * lax.rsqrt(var + LN_EPS) * gn[:, sl] * _silu(r_ref[:, sl]))
    y_gla = jnp.concatenate(parts, axis=1)
    ys = jax.nn.gelu(ys_ref[...])
    y_s5 = ys * _sigmoid(_dot(ys, wglu_ref[...]))
    merged = (_sigmoid(gg_ref[...]) * _dot(y_gla, wbg_ref[...])
              + _sigmoid(gs_ref[...]) * _dot(y_s5, wbs_ref[...])
              + _sigmoid(gl_ref[...]) * _dot(yl_ref[...], wbl_ref[...]))
    mix = _dot(merged, wo_ref[...])
    gate1 = mod_ref[0, :, 2 * D:3 * D]
    out_ref[...] = _layer_norm(alpha * x_ref[...] + gate1 * mix, lg_ref[...], lb_ref[...])


def _merge(x2, mod, o_gla, P, ys, ylru, lw, T, alpha):
    N, D = x2.shape
    W = 3 * D // 2
    Bm = mod.shape[0]
    rcol, gcol = 2, 6
    tm = _row_tile(N, T, Bm > 1, 256)
    per_b = T // tm
    mod_map = (lambda i: (i // per_b, 0, 0)) if Bm > 1 else (lambda i: (0, 0, 0))
    row = lambda width, cb=0: pl.BlockSpec((tm, width), lambda i: (i, cb))
    full = lambda a: pl.BlockSpec(a.shape, lambda i: (0,) * a.ndim)
    weights = [lw["gnorm"], lw["w_glu"], lw["w_br_gla"], lw["w_br_s5"], lw["w_br_lru"],
               lw["w_out"], lw["ln1_g"], lw["ln1_b"]]
    return pl.pallas_call(
        functools.partial(_merge_kernel, D=D, alpha=alpha),
        grid=(N // tm,),
        in_specs=[row(D), pl.BlockSpec((1, 1, 6 * D), mod_map), row(D), row(D, rcol), row(D),
                  row(W), row(D, gcol), row(D, gcol + 1), row(D, gcol + 2)]
        + [full(w) for w in weights],
        out_specs=row(D),
        out_shape=jax.ShapeDtypeStruct((N, D), F32),
        compiler_params=_cparams("parallel"),
        name="merge",
    )(x2, mod, o_gla, P, ys, ylru, P, P, P, *weights)


def _mlp_kernel(x_ref, mod_ref, w1_ref, w2_ref, lg_ref, lb_ref, out_ref, h_ref, acc_ref, *, D, alpha):
    j = pl.program_id(1)

    @pl.when(j == 0)
    def _():
        shift = mod_ref[0, :, 3 * D:4 * D]
        scale = mod_ref[0, :, 4 * D:5 * D]
        h_ref[...] = (x_ref[...] * (1.0 + scale) + shift).astype(BF16)

    hid = jnp.dot(h_ref[...], w1_ref[...], preferred_element_type=F32)
    hid = jnp.square(jnp.maximum(hid, 0.0))
    part = _dot(hid, w2_ref[...])

    @pl.when(j == 0)
    def _():
        acc_ref[...] = part

    @pl.when(j > 0)
    def _():
        acc_ref[...] += part

    @pl.when(j == pl.num_programs(1) - 1)
    def _():
        gate2 = mod_ref[0, :, 5 * D:6 * D]
        out_ref[...] = _layer_norm(alpha * x_ref[...] + gate2 * acc_ref[...], lg_ref[...], lb_ref[...])


def _mlp(x2, mod, lw, T, alpha):
    N, D = x2.shape
    HID = lw["w_mlp_in"].shape[1]
    Bm = mod.shape[0]
    tm = _row_tile(N, T, Bm > 1, 1024)
    th = min(HID, 1024)
    per_b = T // tm
    mod_map = (lambda i, j: (i // per_b, 0, 0)) if Bm > 1 else (lambda i, j: (0, 0, 0))
    return pl.pallas_call(
        functools.partial(_mlp_kernel, D=D, alpha=alpha),
        grid=(N // tm, HID // th),
        in_specs=[pl.BlockSpec((tm, D), lambda i, j: (i, 0)),
                  pl.BlockSpec((1, 1, 6 * D), mod_map),
                  pl.BlockSpec((D, th), lambda i, j: (0, j)),
                  pl.BlockSpec((th, D), lambda i, j: (j, 0)),
                  pl.BlockSpec((1, D), lambda i, j: (0, 0)),
                  pl.BlockSpec((1, D), lambda i, j: (0, 0))],
        out_specs=pl.BlockSpec((tm, D), lambda i, j: (i, 0)),
        out_shape=jax.ShapeDtypeStruct((N, D), F32),
        scratch_shapes=[pltpu.VMEM((tm, D), BF16), pltpu.VMEM((tm, D), F32)],
        compiler_params=_cparams("parallel", "arbitrary"),
        name="mlp",
    )(x2, mod, lw["w_mlp_in"], lw["w_mlp_out"], lw["ln2_g"], lw["ln2_b"])


def _pack_layer_weights(l, D, w_in, gla_w_gate, gla_b_gate, gla_norm_g, s5_d, s5_w_glu,
                        lru_conv_w, lru_conv_b, lru_w_a, lru_b_a, lru_w_i, lru_b_i, lru_lam,
                        w_br_gla, w_br_s5, w_br_lru, w_out, ln1_g, ln1_b, ln2_g, ln2_b,
                        w_mlp_in, w_mlp_out):
    H = GLA_HEADS
    KEY = D // 2
    DK = KEY // H
    W = 3 * D // 2
    CB = 2 * LRU_BLOCK
    NB = W // LRU_BLOCK
    widths = (KEY, KEY, D, D, 2 * GATE_RANK, D, W, W, 3 * D)
    offs = [0]
    for wd in widths:
        offs.append(offs[-1] + wd)
    wl = w_in[l]
    piece = lambda i: wl[:, offs[i]:offs[i + 1]]
    w_pack = jnp.concatenate([piece(0), piece(1), piece(2), piece(3), piece(6), piece(7),
                              piece(8)], axis=1).astype(BF16)
    w_glr = jnp.concatenate([piece(4), jnp.zeros((D, LANES - 2 * GATE_RANK), wl.dtype)], axis=1).astype(BF16)
    w_ut = piece(5).T.astype(BF16)
    wgate = gla_w_gate[l]
    wg = jnp.zeros((H, LANES, 2 * DK), F32)
    wg = wg.at[:, 0:GATE_RANK, 0:DK].set(wgate[0].reshape(GATE_RANK, H, DK).transpose(1, 0, 2))
    wg = wg.at[:, GATE_RANK:2 * GATE_RANK, DK:].set(wgate[1].reshape(GATE_RANK, H, DK).transpose(1, 0, 2))
    bgate = gla_b_gate[l].reshape(2, H, DK).transpose(1, 0, 2).reshape(H, 1, 2 * DK)
    lru_wg = 0.5 * jnp.concatenate([lru_w_a[l], lru_w_i[l]], axis=-1)
    lru_wg = lru_wg.reshape(2, W // CB, CB // LRU_BLOCK, LRU_BLOCK, 2 * LRU_BLOCK).astype(BF16)
    lru_bg = 0.5 * jnp.concatenate([lru_b_a[l].reshape(2, W // CB, 1, CB), lru_b_i[l].reshape(2, W // CB, 1, CB)],
                                   axis=-1)
    return {
        "w_pack": w_pack, "w_glr": w_glr, "w_ut": w_ut, "gla_wg": wg.astype(BF16), "gla_bg": bgate,
        "lru": {"conv_w": lru_conv_w[l].reshape(LRU_CONV, W // CB, CB).transpose(1, 0, 2),
                "conv_b": lru_conv_b[l].reshape(W // CB, 1, CB),
                "wg": lru_wg, "bg": lru_bg, "lam": lru_lam[l].reshape(2, W // CB, 1, CB)},
        "gnorm": gla_norm_g[l].reshape(1, D), "s5_dcol": jnp.tile(s5_d[l].reshape(D // S5_GROUP, 1, S5_GROUP), (1, S5_SUB, 1)).reshape(
            D // S5_GROUP, S5_SUB * S5_GROUP, 1),
        "w_glu": s5_w_glu[l].astype(BF16), "w_br_gla": w_br_gla[l].astype(BF16),
        "w_br_s5": w_br_s5[l].astype(BF16), "w_br_lru": w_br_lru[l].astype(BF16),
        "w_out": w_out[l].astype(BF16),
        "ln1_g": ln1_g[l].reshape(1, D), "ln1_b": ln1_b[l].reshape(1, D),
        "ln2_g": ln2_g[l].reshape(1, D), "ln2_b": ln2_b[l].reshape(1, D),
        "w_mlp_in": w_mlp_in[l].astype(BF16), "w_mlp_out": w_mlp_out[l].astype(BF16),
    }


def _trunk_layer(x2, mod, lw, s5ops, init, B, T, D, alpha):
    N = B * T
    G = D // S5_GROUP
    S = S5_SUB
    R = T // S
    Pn = S5_STATE
    P, glr = _in_proj(x2, mod, lw["w_pack"], lw["w_glr"], T)
    gla0 = s5re0 = s5im0 = lru0 = None
    if init is not None:
        gla0, s5re0, s5im0, lru0 = init
        s5re0 = s5re0.transpose(2, 0, 1, 3).reshape(G, B, 2 * Pn)
        s5im0 = s5im0.transpose(2, 0, 1, 3).reshape(G, B, 2 * Pn)
    o_gla, gla_fin = _gla_core(P, glr, lw["gla_wg"], lw["gla_bg"], gla0, B, T, D)
    if s5re0 is None:
        s5re0 = jnp.zeros((G, B, 2 * Pn), F32)
        s5im0 = jnp.zeros((G, B, 2 * Pn), F32)
    xt = _s5_proj_t(x2, mod, lw["w_ut"], B, T)
    yt, fre, fim = _s5_core_t(xt, lw["s5_dcol"], s5ops, s5re0, s5im0, R, B)
    ys = _s5_rows(yt, B, T, D)
    s5re_fin = fre.reshape(G, B, 2, Pn).transpose(1, 2, 0, 3)
    s5im_fin = fim.reshape(G, B, 2, Pn).transpose(1, 2, 0, 3)
    P3 = P.reshape(B, T, P.shape[1])
    hf, lf = _lru_sweep(P3, lw["lru"], lru0, None, False, B, T, D)
    ylru, lb = _lru_sweep(P3, lw["lru"], lru0, hf, True, B, T, D)
    lru_fin = jnp.stack([lf, lb], axis=1)
    x1 = _merge(x2, mod, o_gla, P, ys, ylru.reshape(N, -1), lw, T, alpha)
    x3 = _mlp(x1, mod, lw, T, alpha)
    return x3, (gla_fin, s5re_fin, s5im_fin, lru_fin)


def kernel(x_prompt, x_sample, state_gla, state_s5_re, state_s5_im, state_lru, c, c_ctx, w_ada, b_ada, w_in, gla_w_gate, gla_b_gate, gla_norm_g, s5_lam_re, s5_lam_im, s5_log_step, s5_b_re, s5_b_im, s5_c_re, s5_c_im, s5_d, s5_w_glu, lru_conv_w, lru_conv_b, lru_w_a, lru_b_a, lru_w_i, lru_b_i, lru_lam, w_br_gla, w_br_s5, w_br_lru, w_out, ln1_g, ln1_b, ln2_g, ln2_b, w_mlp_in, w_mlp_out):
    Bp, Tp, D = x_prompt.shape
    Bs, Ts, _ = x_sample.shape
    L = w_in.shape[0]
    alpha = (2.0 * L) ** 0.25

    n_rows = -(-(Bs + 1) // SUBLANES) * SUBLANES
    cc = jnp.concatenate([c, c_ctx[None], jnp.zeros((n_rows - Bs - 1, D), F32)], axis=0)
    mod = _ada_mod(cc, w_ada, b_ada)
    s5ops_all = _s5_operators(s5_lam_re, s5_lam_im, s5_log_step, s5_b_re, s5_b_im, s5_c_re, s5_c_im)

    xp = x_prompt.reshape(Bp * Tp, D)
    xs = _add_pos(x_sample, _grid_pos_table(Ts, D)).reshape(Bs * Ts, D)
    fins = []
    for l in range(L):
        lw = _pack_layer_weights(l, D, w_in, gla_w_gate, gla_b_gate, gla_norm_g, s5_d, s5_w_glu,
                                 lru_conv_w, lru_conv_b, lru_w_a, lru_b_a, lru_w_i, lru_b_i, lru_lam,
                                 w_br_gla, w_br_s5, w_br_lru, w_out, ln1_g, ln1_b, ln2_g, ln2_b,
                                 w_mlp_in, w_mlp_out)
        s5ops = tuple(a[l] for a in s5ops_all)
        mod_ctx = mod[l, Bs:Bs + 1].reshape(1, 1, 6 * D)
        mod_lat = mod[l, :Bs].reshape(Bs, 1, 6 * D)
        xp, fin = _trunk_layer(xp, mod_ctx, lw, s5ops, None, Bp, Tp, D, alpha)
        fins.append(fin)
        cache = (state_gla[:, l], state_s5_re[:, l], state_s5_im[:, l], state_lru[:, l])
        xs, _ = _trunk_layer(xs, mod_lat, lw, s5ops, cache, Bs, Ts, D, alpha)
    sdt = x_prompt.dtype
    new_states = tuple(jnp.stack([f[i] for f in fins], axis=1).astype(sdt) for i in range(4))
    return (xp.reshape(Bp, Tp, D), xs.reshape(Bs, Ts, D)) + new_states
```

```python
import functools
import math

import jax
import jax.numpy as jnp
from jax import lax
from jax.experimental import pallas as pl
from jax.experimental.pallas import tpu as pltpu

F32 = jnp.float32
BF16 = jnp.bfloat16
HIGHEST = lax.Precision.HIGHEST

LANES = 128
SUBLANES = 8
VMEM_LIMIT = 56 * 1024 * 1024

GRID_W = 64
GLA_HEADS = 4
GATE_RANK = 16
GLA_TAU = 16.0
GLA_CHUNK = 64
GLA_BLOCK = 256
S5_GROUP = 16
S5_STATE = 64
S5_SUB = 16
S5_PROJ_ROWS = 1024
LRU_BLOCK = 128
LRU_CONV = 4
LRU_C = 8.0
LN_EPS = 1e-5
F32_TINY = 1.1754944e-38
SCAN_UNROLL = 8


def _cparams(*sem):
    return pltpu.CompilerParams(dimension_semantics=sem, vmem_limit_bytes=VMEM_LIMIT)


def _dot(a, b):
    return jnp.dot(a.astype(BF16), b.astype(BF16), preferred_element_type=F32)


def _dot_nt(a, b):
    return lax.dot_general(a.astype(BF16), b.astype(BF16), (((1,), (1,)), ((), ())),
                           preferred_element_type=F32)


def _dot_tn(a, b):
    return lax.dot_general(a.astype(BF16), b.astype(BF16), (((0,), (0,)), ((), ())),
                           preferred_element_type=F32)


def _dot_f32(a, b):
    return jnp.dot(a, b, precision=HIGHEST, preferred_element_type=F32)


def _layer_norm(z, g, b):
    mu = jnp.mean(z, axis=-1, keepdims=True)
    zc = z - mu
    var = jnp.mean(zc * zc, axis=-1, keepdims=True)
    return zc * lax.rsqrt(var + LN_EPS) * g + b


def _sigmoid(x):
    return 0.5 * jnp.tanh(0.5 * x) + 0.5


def _silu(x):
    return x * _sigmoid(x)


def _ada_kernel(cc_ref, w_ref, b_ref, o_ref):
    s = _silu(cc_ref[...])
    o_ref[0] = _dot_f32(s, w_ref[0]) + b_ref[0]


def _ada_mod(cc, w_ada, b_ada):
    L, D, D6 = w_ada.shape
    R = cc.shape[0]
    tn = D6 // 4
    return pl.pallas_call(
        _ada_kernel,
        grid=(L, D6 // tn),
        in_specs=[pl.BlockSpec((R, D), lambda l, j: (0, 0)),
                  pl.BlockSpec((1, D, tn), lambda l, j: (l, 0, j)),
                  pl.BlockSpec((1, 1, tn), lambda l, j: (l, 0, j))],
        out_specs=pl.BlockSpec((1, R, tn), lambda l, j: (l, 0, j)),
        out_shape=jax.ShapeDtypeStruct((L, R, D6), F32),
        compiler_params=_cparams("parallel", "parallel"),
        name="ada_mod",
    )(cc, w_ada, b_ada.reshape(L, 1, D6))


def _addpos_kernel(x_ref, p_ref, o_ref):
    o_ref[0] = x_ref[0] + p_ref[...]


def _add_pos(x, pos):
    B, T, D = x.shape
    tt = min(T, 512)
    return pl.pallas_call(
        _addpos_kernel,
        grid=(T // tt, B),
        in_specs=[pl.BlockSpec((1, tt, D), lambda t, b: (b, t, 0)),
                  pl.BlockSpec((tt, D), lambda t, b: (t, 0))],
        out_specs=pl.BlockSpec((1, tt, D), lambda t, b: (b, t, 0)),
        out_shape=jax.ShapeDtypeStruct((B, T, D), F32),
        compiler_params=_cparams("parallel", "parallel"),
        name="add_pos",
    )(x, pos)


def _grid_pos_table(n_tokens, dim):
    rows = n_tokens // GRID_W
    quarter = dim // 4
    omega = 1.0 / (10000.0 ** (jnp.arange(quarter, dtype=F32) / quarter))
    r = jnp.arange(rows, dtype=F32)[:, None, None] * omega
    cl = jnp.arange(GRID_W, dtype=F32)[None, :, None] * omega
    shape = (rows, GRID_W, quarter)
    emb = jnp.concatenate([jnp.broadcast_to(jnp.sin(r), shape), jnp.broadcast_to(jnp.cos(r), shape),
                           jnp.broadcast_to(jnp.sin(cl), shape), jnp.broadcast_to(jnp.cos(cl), shape)], axis=-1)
    return emb.reshape(rows * GRID_W, dim)


def _inproj_kernel(x_ref, mod_ref, w_ref, wglr_ref, o_ref, glr_ref, h_ref, *, D):
    @pl.when(pl.program_id(1) == 0)
    def _():
        shift = mod_ref[0, :, 0:D]
        scale = mod_ref[0, :, D:2 * D]
        h = (x_ref[...] * (1.0 + scale) + shift).astype(BF16)
        h_ref[...] = h
        glr_ref[...] = jnp.dot(h, wglr_ref[...], preferred_element_type=F32)

    o_ref[...] = jnp.dot(h_ref[...], w_ref[...], preferred_element_type=F32)


def _row_tile(N, T, per_batch_mod, want):
    tm = min(want, T if per_batch_mod else N)
    while N % tm or (per_batch_mod and T % tm):
        tm //= 2
    return tm


def _in_proj(x2, mod, w_pack, w_glr, T):
    N, D = x2.shape
    NC = w_pack.shape[1]
    Bm = mod.shape[0]
    tm = _row_tile(N, T, Bm > 1, 1024)
    tn = D
    per_b = T // tm
    mod_map = (lambda i, j: (i // per_b, 0, 0)) if Bm > 1 else (lambda i, j: (0, 0, 0))
    return pl.pallas_call(
        functools.partial(_inproj_kernel, D=D),
        grid=(N // tm, NC // tn),
        in_specs=[pl.BlockSpec((tm, D), lambda i, j: (i, 0)),
                  pl.BlockSpec((1, 1, 6 * D), mod_map),
                  pl.BlockSpec((D, tn), lambda i, j: (0, j)),
                  pl.BlockSpec((D, LANES), lambda i, j: (0, 0))],
        out_specs=[pl.BlockSpec((tm, tn), lambda i, j: (i, j)),
                   pl.BlockSpec((tm, LANES), lambda i, j: (i, 0))],
        out_shape=[jax.ShapeDtypeStruct((N, NC), F32), jax.ShapeDtypeStruct((N, LANES), F32)],
        scratch_shapes=[pltpu.VMEM((tm, D), BF16)],
        compiler_params=_cparams("parallel", "arbitrary"),
        name="in_proj",
    )(x2, mod, w_pack, w_glr)


def _gla_kernel(*refs, T, DK, DV, has_init):
    if has_init:
        (q_ref, k_ref, v_ref, glr_ref, wg_ref, bg_ref, s0_ref,
         o_ref, sfin_ref, qb_ref, keb_ref, db_ref, sf_ref, sb_ref) = refs
    else:
        (q_ref, k_ref, v_ref, glr_ref, wg_ref, bg_ref,
         o_ref, sfin_ref, qb_ref, keb_ref, db_ref, sf_ref, sb_ref) = refs
    C = GLA_CHUNK
    BLK = GLA_BLOCK
    NS = BLK // C
    nblk = T // BLK
    qscale = DK ** -0.5
    shift = C.bit_length() - 1
    row = lax.broadcasted_iota(jnp.int32, (BLK, BLK), 0)
    col = lax.broadcasted_iota(jnp.int32, (BLK, BLK), 1)
    same = lax.shift_right_logical(row, shift) == lax.shift_right_logical(col, shift)
    lower = jnp.logical_and(same, row >= col)
    upper = jnp.logical_and(same, row <= col)
    tri = jnp.where(lower, 1.0, 0.0).astype(BF16)
    chunk_of_row = lax.shift_right_logical(lax.broadcasted_iota(jnp.int32, (BLK, DK), 0), shift)
    wg = wg_ref[0]
    bg = bg_ref[0]

    def chunk_cols(ke):
        return jnp.concatenate([jnp.where(chunk_of_row == c, ke, 0.0) for c in range(NS)], axis=1).astype(BF16)

    def per_chunk_last(x, r):
        return jnp.concatenate([jnp.broadcast_to(x[c * C + r:c * C + r + 1, :], (C, x.shape[1]))
                                for c in range(NS)], axis=0)

    if has_init:
        sf_ref[...] = s0_ref[0, 0, 0].T
        sb_ref[...] = s0_ref[0, 1, 0].T
    else:
        sf_ref[...] = jnp.zeros((DV, DK), F32)
        sb_ref[...] = jnp.zeros((DV, DK), F32)

    def fwd_body(i, carry):
        r0 = pl.multiple_of(i * BLK, BLK)
        rows = pl.ds(r0, BLK)
        qc = q_ref[rows, :] * qscale
        kc = k_ref[rows, :]
        vc = v_ref[rows, :]
        logits = _dot(glr_ref[rows, :], wg) + bg
        la = (jnp.minimum(logits, 0.0) - jnp.log(1.0 + jnp.exp(-jnp.abs(logits)))) * (1.0 / GLA_TAU)
        p1 = la.astype(BF16)
        p2 = (la - p1.astype(F32)).astype(BF16)
        pre = jnp.dot(tri, p1, preferred_element_type=F32) + jnp.dot(tri, p2, preferred_element_type=F32)
        tot = per_chunk_last(pre, C - 1)
        cum_f = pre[:, :DK]
        last_f = tot[:, :DK]
        cum_b = tot[:, DK:] - pre[:, DK:] + la[:, DK:]
        last_b = tot[:, DK:]
        q_f = qc * jnp.exp(cum_f)
        k_f = kc * jnp.exp(-cum_f)
        ke_f = kc * jnp.exp(last_f - cum_f)
        q_b = qc * jnp.exp(cum_b)
        k_b = kc * jnp.exp(-cum_b)
        ke_b = kc * jnp.exp(last_b - cum_b)
        sc = jnp.where(lower, _dot_nt(q_f, k_f), 0.0) + jnp.where(upper, _dot_nt(q_b, k_b), 0.0)
        o_blk = _dot(sc, vc)
        u_f = jnp.dot(vc.T.astype(BF16), chunk_cols(ke_f), preferred_element_type=F32)
        d_f = jnp.exp(last_f)
        s = sf_ref[...]
        for c in range(NS):
            rc = slice(c * C, (c + 1) * C)
            o_ref[pl.ds(r0 + c * C, C), :] = o_blk[rc] + _dot_nt(q_f[rc], s)
            s = s * d_f[c * C:c * C + 1, :] + u_f[:, c * DK:(c + 1) * DK]
        sf_ref[...] = s
        qb_ref[rows, :] = q_b
        keb_ref[rows, :] = ke_b
        d_b = jnp.exp(last_b)
        db_ref[pl.ds(i, 1), :] = jnp.concatenate([d_b[c * C:c * C + 1, :] for c in range(NS)], axis=1)
        return carry

    lax.fori_loop(0, nblk, fwd_body, 0, unroll=min(8, nblk))

    def bwd_body(j, carry):
        i = nblk - 1 - j
        r0 = pl.multiple_of(i * BLK, BLK)
        rows = pl.ds(r0, BLK)
        q_b = qb_ref[rows, :]
        u_b = jnp.dot(v_ref[rows, :].T.astype(BF16), chunk_cols(keb_ref[rows, :]), preferred_element_type=F32)
        d_b = db_ref[pl.ds(i, 1), :]
        s = sb_ref[...]
        for c in reversed(range(NS)):
            rc = pl.ds(r0 + c * C, C)
            o_ref[rc, :] = o_ref[rc, :] + _dot_nt(q_b[c * C:(c + 1) * C], s)
            s = s * d_b[:, c * DK:(c + 1) * DK] + u_b[:, c * DK:(c + 1) * DK]
        sb_ref[...] = s
        return carry

    lax.fori_loop(0, nblk, bwd_body, 0, unroll=min(8, nblk))

    sfin_ref[0, 0, 0] = sf_ref[...].T
    sfin_ref[0, 1, 0] = sb_ref[...].T


def _gla_core(P, glr, wg, bg, s0, B, T, D):
    H = GLA_HEADS
    DK = D // 2 // H
    DV = D // H
    N = B * T
    has_init = s0 is not None
    kcol = (D // 2) // DK
    vcol = D // DV
    in_specs = [pl.BlockSpec((T, DK), lambda b, h: (b, h)),
                pl.BlockSpec((T, DK), lambda b, h: (b, kcol + h)),
                pl.BlockSpec((T, DV), lambda b, h: (b, vcol + h)),
                pl.BlockSpec((T, LANES), lambda b, h: (b, 0)),
                pl.BlockSpec((1, LANES, 2 * DK), lambda b, h: (h, 0, 0)),
                pl.BlockSpec((1, 1, 2 * DK), lambda b, h: (h, 0, 0))]
    args = [P, P, P, glr, wg, bg]
    if has_init:
        in_specs.append(pl.BlockSpec((1, 2, 1, DK, DV), lambda b, h: (b, 0, h, 0, 0)))
        args.append(s0)
    o, sfin = pl.pallas_call(
        functools.partial(_gla_kernel, T=T, DK=DK, DV=DV, has_init=has_init),
        grid=(B, H),
        in_specs=in_specs,
        out_specs=[pl.BlockSpec((T, DV), lambda b, h: (b, h)),
                   pl.BlockSpec((1, 2, 1, DK, DV), lambda b, h: (b, 0, h, 0, 0))],
        out_shape=[jax.ShapeDtypeStruct((N, D), F32),
                   jax.ShapeDtypeStruct((B, 2, H, DK, DV), F32)],
        scratch_shapes=[pltpu.VMEM((T, DK), F32), pltpu.VMEM((T, DK), F32),
                        pltpu.VMEM((T // GLA_BLOCK, (GLA_BLOCK // GLA_CHUNK) * DK), F32),
                        pltpu.VMEM((DV, DK), F32), pltpu.VMEM((DV, DK), F32)],
        compiler_params=_cparams("parallel", "parallel"),
        name="gla_core",
    )(*args)
    return o, sfin


def _s5prep_kernel(lr_ref, li_ref, ls_ref, btr_ref, bti_ref, cr_ref, ci_ref,
                   kt_ref, k0_ref, wre_ref, wim_ref, cpre_ref, cpimn_ref, pre_ref, pim_ref):
    I = S5_GROUP
    btr = btr_ref[0]
    bti = bti_ref[0]
    c_re = cr_ref[0]
    c_im = ci_ref[0]
    batched_nt = (((2,), (2,)), ((0,), (0,)))
    for d in range(2):
        lr = lr_ref[0, d]
        li = li_ref[0, d]
        step = jnp.exp(ls_ref[0, d])
        mag = jnp.exp(lr * step)
        ang = li * step
        a_re = mag * jnp.cos(ang)
        a_im = mag * jnp.sin(ang)
        den = lr * lr + li * li
        nr = a_re - 1.0
        f_re = (nr * lr + a_im * li) / den
        f_im = (a_im * lr - nr * li) / den
        bb_re = f_re[:, None, :] * btr - f_im[:, None, :] * bti
        bb_im = f_re[:, None, :] * bti + f_im[:, None, :] * btr
        p_re = jnp.ones_like(lr)
        p_im = jnp.zeros_like(lr)
        for e in range(S5_SUB + 1):
            pr = p_re[:, None, :]
            pi = p_im[:, None, :]
            cp_re = c_re * pr - c_im * pi
            cp_im = c_re * pi + c_im * pr
            cpre_ref[0, d, :, e * I:(e + 1) * I, :] = cp_re
            cpimn_ref[0, d, :, e * I:(e + 1) * I, :] = -cp_im
            if e < S5_SUB:
                wre_ref[0, d, :, e * I:(e + 1) * I, :] = pr * bb_re - pi * bb_im
                wim_ref[0, d, :, e * I:(e + 1) * I, :] = pr * bb_im + pi * bb_re
                lag = (lax.dot_general(cp_re, bb_re, batched_nt, precision=HIGHEST, preferred_element_type=F32)
                       - lax.dot_general(cp_im, bb_im, batched_nt, precision=HIGHEST, preferred_element_type=F32))
                kt_ref[0, d, :, e * I:(e + 1) * I, :] = lag
                if e == 0:
                    k0_ref[0] = lag if d == 0 else k0_ref[0] + lag
                p_re, p_im = p_re * a_re - p_im * a_im, p_re * a_im + p_im * a_re
        pre_ref[0, d] = p_re
        pim_ref[0, d] = p_im


def _s5_operators(lam_re, lam_im, log_step, b_re, b_im, c_re, c_im):
    L, _, G, Pn = lam_re.shape
    I = S5_GROUP
    S = S5_SUB
    ls = jnp.broadcast_to(log_step[..., None], lam_re.shape)
    btr = jnp.swapaxes(b_re, -1, -2)
    bti = jnp.swapaxes(b_im, -1, -2)
    Gb = SUBLANES
    lam_spec = pl.BlockSpec((1, 2, Gb, Pn), lambda l, g: (l, 0, g, 0))
    gip_spec = pl.BlockSpec((1, Gb, I, Pn), lambda l, g: (l, g, 0, 0))

    def out_spec(rows, last):
        return pl.BlockSpec((1, 2, Gb, rows, last), lambda l, g: (l, 0, g, 0, 0))

    kt, k0, wre, wim, cpre, cpimn, pre, pim = pl.pallas_call(
        _s5prep_kernel,
        grid=(L, G // Gb),
        in_specs=[lam_spec, lam_spec, lam_spec, gip_spec, gip_spec, gip_spec, gip_spec],
        out_specs=[out_spec(S * I, I), pl.BlockSpec((1, Gb, I, I), lambda l, g: (l, g, 0, 0)),
                   out_spec(S * I, Pn), out_spec(S * I, Pn),
                   out_spec((S + 1) * I, Pn), out_spec((S + 1) * I, Pn), lam_spec, lam_spec],
        out_shape=[jax.ShapeDtypeStruct((L, 2, G, S * I, I), F32),
                   jax.ShapeDtypeStruct((L, G, I, I), F32),
                   jax.ShapeDtypeStruct((L, 2, G, S * I, Pn), F32),
                   jax.ShapeDtypeStruct((L, 2, G, S * I, Pn), F32),
                   jax.ShapeDtypeStruct((L, 2, G, (S + 1) * I, Pn), F32),
                   jax.ShapeDtypeStruct((L, 2, G, (S + 1) * I, Pn), F32),
                   jax.ShapeDtypeStruct((L, 2, G, Pn), F32),
                   jax.ShapeDtypeStruct((L, 2, G, Pn), F32)],
        compiler_params=_cparams("parallel", "parallel"),
        name="s5_prep",
    )(lam_re, lam_im, ls, btr, bti, c_re, c_im)

    kt = kt.reshape(L, 2, G, S, I, I)
    table = jnp.concatenate([kt[:, 0, :, 1:], kt[:, 1, :, 1:], k0[:, :, None]], axis=2)
    s_idx = jnp.arange(S)[:, None]
    t_idx = jnp.arange(S)[None, :]
    pick = jnp.where(t_idx > s_idx, t_idx - s_idx - 1,
                     jnp.where(t_idx < s_idx, (S - 1) + s_idx - t_idx - 1, 2 * S - 2))
    m = table[:, :, pick]
    m = m.transpose(0, 1, 3, 4, 2, 5).reshape(L, G, S * I, S * I)

    def by_pos(w, reverse):
        w = w.reshape(L, G, S, I, Pn)
        if reverse:
            w = w[:, :, ::-1]
        return w.reshape(L, G, S * I, Pn).swapaxes(-1, -2)

    wall = jnp.concatenate([by_pos(wre[:, 0], True), by_pos(wre[:, 1], False),
                            by_pos(wim[:, 0], True), by_pos(wim[:, 1], False)], axis=2)

    def readout(cp):
        cp = cp.reshape(L, 2, G, S + 1, I, Pn)
        f = cp[:, 0, :, 1:]
        b = cp[:, 1, :, 1:][:, :, ::-1]
        return jnp.concatenate([f.reshape(L, G, S * I, Pn), b.reshape(L, G, S * I, Pn)], axis=-1)

    vre = readout(cpre)
    vim = readout(cpimn)
    are = jnp.concatenate([pre[:, 0], pre[:, 1]], axis=-1)[:, :, None, :]
    aim = jnp.concatenate([pim[:, 0], pim[:, 1]], axis=-1)[:, :, None, :]
    return m, wall, vre, vim, are, aim


def _s5projt_kernel(x_ref, mod_ref, wt_ref, o_ref, *, D, R):
    Bm = mod_ref.shape[0]
    span = x_ref.shape[0] // Bm
    parts = []
    for b in range(Bm):
        shift = mod_ref[b, :, 0:D]
        scale = mod_ref[b, :, D:2 * D]
        parts.append((x_ref[b * span:(b + 1) * span, :] * (1.0 + scale) + shift).astype(BF16))
    h = parts[0] if Bm == 1 else jnp.concatenate(parts, axis=0)
    ut = lax.dot_general(wt_ref[...], h, (((1,), (1,)), ((), ())), preferred_element_type=F32)
    o_ref[...] = ut.reshape(o_ref.shape).astype(o_ref.dtype)


def _s5_proj_t(x2, mod, w_ut, B, T):
    N, D = x2.shape
    S = S5_SUB
    R = T // S
    G = D // S5_GROUP
    Bm = mod.shape[0]
    xv = x2.reshape(B * R, S * D)
    rows = B * R
    while rows > S5_PROJ_ROWS and rows % 2 == 0 and (rows // 2) % R == 0 and (rows // 2) % LANES == 0:
        rows //= 2
    nrb = B * R // rows
    mods = max(1, Bm * rows // (B * R))
    mod_spec = pl.BlockSpec((mods, 1, 6 * D), lambda i, s: (i if Bm > 1 else 0, 0, 0))
    return pl.pallas_call(
        functools.partial(_s5projt_kernel, D=D, R=R),
        grid=(nrb, S),
        in_specs=[pl.BlockSpec((rows, D), lambda i, s: (i, s)),
                  mod_spec,
                  pl.BlockSpec((D, D), lambda i, s: (0, 0))],
        out_specs=pl.BlockSpec((G, S5_GROUP, rows), lambda i, s: (0, s, i)),
        out_shape=jax.ShapeDtypeStruct((G, S * S5_GROUP, B * R), F32),
        compiler_params=_cparams("parallel", "parallel"),
        name="s5_proj_t",
    )(xv, mod, w_ut)


def _s5t_kernel(xt_ref, dcol_ref, mt_ref, wt_ref, vret_ref, vimt_ref, are_ref, aim_ref,
                s0re_ref, s0im_ref, yt_ref, fre_ref, fim_ref,
                zt_ref, zre_ref, zim_ref, hfre_ref, hfim_ref, hbre_ref, hbim_ref, *, R, B, chunk_major):
    Pn = S5_STATE
    xt32 = xt_ref[0]
    xt = xt32.astype(BF16)
    yt_ref[0] = _dot(mt_ref[0], xt) + dcol_ref[0] * xt32
    zt_ref[...] = jnp.dot(wt_ref[0].astype(BF16), xt, preferred_element_type=F32)
    if chunk_major:
        for b in range(B):
            zb = zt_ref[:, b * R:(b + 1) * R].T
            zre_ref[pl.ds(b, R, stride=B), :] = zb[:, :2 * Pn]
            zim_ref[pl.ds(b, R, stride=B), :] = zb[:, 2 * Pn:]
        step_rows = lambda r: pl.ds(pl.multiple_of(r * B, B), B)
    else:
        z = zt_ref[...].T
        zre_ref[...] = z[:, :2 * Pn]
        zim_ref[...] = z[:, 2 * Pn:]
        step_rows = lambda r: pl.ds(r, B, stride=R)
    a_re = are_ref[0]
    a_im = aim_ref[0]
    h0 = (s0re_ref[0], s0im_ref[0])

    def advance(h, rows):
        hr, hi = h
        return (hr * a_re - hi * a_im + zre_ref[rows, :], hr * a_im + hi * a_re + zim_ref[rows, :])

    def step(r, carry):
        hf, hb = carry
        rows_f = step_rows(r)
        rows_b = step_rows(R - 1 - r)
        hfre_ref[rows_f, :] = hf[0]
        hfim_ref[rows_f, :] = hf[1]
        hbre_ref[rows_b, :] = hb[0]
        hbim_ref[rows_b, :] = hb[1]
        return advance(hf, rows_f), advance(hb, rows_b)

    hf, hb = lax.fori_loop(0, R, step, (h0, h0), unroll=SCAN_UNROLL)

    def own_half(f, b):
        return jnp.where(lax.broadcasted_iota(jnp.int32, f.shape, 1) < Pn, f, b)

    fre_ref[0] = own_half(hf[0], hb[0])
    fim_ref[0] = own_half(hf[1], hb[1])
    vret = vret_ref[0]
    vimt = vimt_ref[0]
    if chunk_major:
        for b in range(B):
            sel = pl.ds(b, R, stride=B)
            hre_b = own_half(hfre_ref[sel, :], hbre_ref[sel, :])
            him_b = own_half(hfim_ref[sel, :], hbim_ref[sel, :])
            cols = slice(b * R, (b + 1) * R)
            yt_ref[0, :, cols] = yt_ref[0, :, cols] + _dot_nt(vret, hre_b) + _dot_nt(vimt, him_b)
    else:
        hre = own_half(hfre_ref[...], hbre_ref[...])
        him = own_half(hfim_ref[...], hbim_ref[...])
        yt_ref[0] = yt_ref[0] + _dot_nt(vret, hre) + _dot_nt(vimt, him)


def _s5_core_t(xt, dcol, ops, s0re, s0im, R, B):
    G, K, BR = xt.shape
    mt, wt, vret, vimt, are, aim = ops
    P2 = 2 * S5_STATE
    chunk_major = B == SUBLANES and R % LANES == 0
    gspec = lambda shape: pl.BlockSpec((1,) + shape, lambda g: (g, 0, 0))
    return pl.pallas_call(
        functools.partial(_s5t_kernel, R=R, B=B, chunk_major=chunk_major),
        grid=(G,),
        in_specs=[gspec((K, BR)), gspec((K, 1)), gspec((K, K)), gspec((K, K)),
                  gspec((K, P2)), gspec((K, P2)), gspec((1, P2)), gspec((1, P2)),
                  gspec((B, P2)), gspec((B, P2))],
        out_specs=[gspec((K, BR)), gspec((B, P2)), gspec((B, P2))],
        out_shape=[jax.ShapeDtypeStruct((G, K, BR), F32),
                   jax.ShapeDtypeStruct((G, B, P2), F32),
                   jax.ShapeDtypeStruct((G, B, P2), F32)],
        scratch_shapes=[pltpu.VMEM((K, BR), F32)] + [pltpu.VMEM((BR, P2), F32) for _ in range(6)],
        compiler_params=_cparams("parallel"),
        name="s5_core_t",
    )(xt, dcol, mt, wt, vret, vimt, are, aim, s0re, s0im)


def _s5rows_kernel(yt_ref, o_ref):
    I = S5_GROUP
    nc = yt_ref.shape[-1]
    for t in range(S5_SUB):
        blk = yt_ref[:, t * I:(t + 1) * I, :]
        o_ref[pl.ds(t, nc, stride=S5_SUB), :] = blk.reshape(LANES, nc).T


def _s5_rows(yt, B, T, D):
    G = D // S5_GROUP
    BR = B * T // S5_SUB
    gpt = LANES // S5_GROUP
    nc = 2 * LANES if BR % (2 * LANES) == 0 else LANES
    return pl.pallas_call(
        _s5rows_kernel,
        grid=(BR // nc, G // gpt),
        in_specs=[pl.BlockSpec((gpt, S5_SUB * S5_GROUP, nc), lambda i, j: (j, 0, i))],
        out_specs=pl.BlockSpec((nc * S5_SUB, LANES), lambda i, j: (i, j)),
        out_shape=jax.ShapeDtypeStruct((B * T, D), F32),
        compiler_params=_cparams("parallel", "parallel"),
        name="s5_rows",
    )(yt)


def _lru_kernel(*refs, Tt, CB, reverse, has_init):
    refs = list(refs)
    lx_ref, prev_ref, next_ref, cw_ref, cb_ref, wg_ref, bgate_ref, lam_ref = refs[:8]
    pos = 8
    h0_ref = None
    if has_init:
        h0_ref = refs[pos]
        pos += 1
    if reverse:
        hf_ref, ly_ref = refs[pos:pos + 2]
        pos += 2
    out_ref, fin_ref, x_scr, a_scr, b_scr, h_scr, carry_ref = refs[pos:pos + 7]
    NBk = CB // LRU_BLOCK
    k = pl.program_id(2)
    nk = pl.num_programs(2)
    first_tile = k == 0

    @pl.when(first_tile)
    def _():
        if has_init:
            carry_ref[...] = h0_ref[...]
        else:
            carry_ref[...] = jnp.zeros((SUBLANES, CB), F32)

    tpos = (nk - 1 - k) if reverse else k
    has_prev = tpos > 0
    has_next = tpos < nk - 1
    cw = cw_ref[0]
    cbias = cb_ref[0]
    bgate = bgate_ref[0]
    lam = lam_ref[0]
    half_cfac = (-0.5 * LRU_C) * (jnp.maximum(-lam, 0.0) + jnp.log1p(jnp.exp(-jnp.abs(lam))))

    left = LRU_CONV // 2
    for b in range(SUBLANES):
        cur = lx_ref[b]
        pv = jnp.where(has_prev, prev_ref[b], 0.0)
        nx = jnp.where(has_next, next_ref[b], 0.0)
        for c in range(NBk):
            sl = slice(c * LRU_BLOCK, (c + 1) * LRU_BLOCK)
            x_scr[c, pl.ds(left * SUBLANES + b, Tt, stride=SUBLANES), :] = cur[:, sl]
            for q in range(left):
                x_scr[c, pl.ds(q * SUBLANES + b, 1), :] = pv[SUBLANES - left + q:SUBLANES - left + q + 1, sl]
            for q in range(LRU_CONV - 1 - left):
                x_scr[c, pl.ds((Tt + left + q) * SUBLANES + b, 1), :] = nx[q:q + 1, sl]

    RC = min(Tt * SUBLANES, 512)
    for c in range(NBk):
        sl = slice(c * LRU_BLOCK, (c + 1) * LRU_BLOCK)
        bias_c = jnp.concatenate([bgate[:, sl], bgate[:, CB + c * LRU_BLOCK: CB + (c + 1) * LRU_BLOCK]], axis=1)
        for r0 in range(0, Tt * SUBLANES, RC):
            xc = cbias[:, sl] + sum(cw[j:j + 1, sl] * x_scr[c, pl.ds(r0 + j * SUBLANES, RC), :]
                                    for j in range(LRU_CONV))
            th = jnp.tanh(_dot(xc, wg_ref[0, c]) + bias_c)
            log_a = half_cfac[:, sl] * th[:, :LRU_BLOCK] + half_cfac[:, sl]
            a = jnp.exp(log_a)
            om = -jnp.tanh(log_a) * (1.0 + a * a)
            root = om * lax.rsqrt(jnp.maximum(om, F32_TINY))
            bt = root * ((0.5 * th[:, LRU_BLOCK:] + 0.5) * xc)
            a_scr[c, pl.ds(r0, RC), :] = a
            b_scr[c, pl.ds(r0, RC), :] = bt

    def step(s, h):
        t = (Tt - 1 - s) if reverse else s
        rows = pl.ds(pl.multiple_of(t * SUBLANES, SUBLANES), SUBLANES)
        new = []
        for c in range(NBk):
            hc = a_scr[c, rows, :] * h[c] + b_scr[c, rows, :]
            h_scr[c, rows, :] = hc
            new.append(hc)
        return tuple(new)

    h_init = tuple(carry_ref[:, c * LRU_BLOCK:(c + 1) * LRU_BLOCK] for c in range(NBk))
    h_last = lax.fori_loop(0, Tt, step, h_init, unroll=SCAN_UNROLL)
    for c in range(NBk):
        carry_ref[:, c * LRU_BLOCK:(c + 1) * LRU_BLOCK] = h_last[c]
        fin_ref[:, c * LRU_BLOCK:(c + 1) * LRU_BLOCK] = h_last[c]

    for b in range(SUBLANES):
        hb = jnp.concatenate([h_scr[c, pl.ds(b, Tt, stride=SUBLANES), :] for c in range(NBk)], axis=1)
        if reverse:
            out_ref[b] = (hf_ref[b] + hb) * jax.nn.gelu(ly_ref[b])
        else:
            out_ref[b] = hb


def _lru_sweep(P3, lw, h0, hf, reverse, B, T, D):
    W = 3 * D // 2
    CB = 2 * LRU_BLOCK
    Tt = min(T, 256)
    nk = T // Tt
    nb8 = Tt // SUBLANES
    lxcol = (D // 2 + D // 2 + D + D) // CB
    lycol = lxcol + W // CB
    d = 1 if reverse else 0
    has_init = h0 is not None

    def tmap(k):
        return (nk - 1 - k) if reverse else k

    in_specs = [
        pl.BlockSpec((SUBLANES, Tt, CB), lambda g, j, k: (g, tmap(k), lxcol + j)),
        pl.BlockSpec((SUBLANES, SUBLANES, CB),
                     lambda g, j, k: (g, jnp.maximum(tmap(k) * nb8 - 1, 0), lxcol + j)),
        pl.BlockSpec((SUBLANES, SUBLANES, CB),
                     lambda g, j, k: (g, jnp.minimum((tmap(k) + 1) * nb8, T // SUBLANES - 1), lxcol + j)),
        pl.BlockSpec((1, LRU_CONV, CB), lambda g, j, k: (j, 0, 0)),
        pl.BlockSpec((1, 1, CB), lambda g, j, k: (j, 0, 0)),
        pl.BlockSpec((1, CB // LRU_BLOCK, LRU_BLOCK, 2 * LRU_BLOCK), lambda g, j, k: (j, 0, 0, 0)),
        pl.BlockSpec((1, 1, 2 * CB), lambda g, j, k: (j, 0, 0)),
        pl.BlockSpec((1, 1, CB), lambda g, j, k: (j, 0, 0)),
    ]
    args = [P3, P3, P3, lw["conv_w"], lw["conv_b"], lw["wg"][d], lw["bg"][d], lw["lam"][d]]
    if has_init:
        in_specs.append(pl.BlockSpec((SUBLANES, CB), lambda g, j, k: (g, j)))
        args.append(h0[:, d])
    if reverse:
        in_specs.append(pl.BlockSpec((SUBLANES, Tt, CB), lambda g, j, k: (g, tmap(k), j)))
        in_specs.append(pl.BlockSpec((SUBLANES, Tt, CB), lambda g, j, k: (g, tmap(k), lycol + j)))
        args += [hf, P3]
    out, fin = pl.pallas_call(
        functools.partial(_lru_kernel, Tt=Tt, CB=CB, reverse=reverse, has_init=has_init),
        grid=(B // SUBLANES, W // CB, nk),
        in_specs=in_specs,
        out_specs=[pl.BlockSpec((SUBLANES, Tt, CB), lambda g, j, k: (g, tmap(k), j)),
                   pl.BlockSpec((SUBLANES, CB), lambda g, j, k: (g, j))],
        out_shape=[jax.ShapeDtypeStruct((B, T, W), F32), jax.ShapeDtypeStruct((B, W), F32)],
        scratch_shapes=[pltpu.VMEM((CB // LRU_BLOCK, (Tt + LRU_CONV) * SUBLANES, LRU_BLOCK), F32)]
        + [pltpu.VMEM((CB // LRU_BLOCK, Tt * SUBLANES, LRU_BLOCK), F32) for _ in range(3)]
        + [pltpu.VMEM((SUBLANES, CB), F32)],
        compiler_params=_cparams("parallel", "parallel", "arbitrary"),
        name="lru_bwd" if reverse else "lru_fwd",
    )(*args)
    return out, fin


def _merge_kernel(x_ref, mod_ref, o_ref, r_ref, ys_ref, yl_ref, gg_ref, gs_ref, gl_ref,
                  gn_ref, wglu_ref, wbg_ref, wbs_ref, wbl_ref, wo_ref, lg_ref, lb_ref,
                  out_ref, *, D, alpha):
    DV = D // GLA_HEADS
    gn = gn_ref[...]
    parts = []
    for h in range(GLA_HEADS):
        sl = slice(h * DV, (h + 1) * DV)
        o = o_ref[:, sl]
        mu = jnp.mean(o, axis=-1, keepdims=True)
        oc = o - mu
        var = jnp.mean(oc * oc, axis=-1, keepdims=True)
        parts.append(oc * lax.rsqrt(var + LN_EPS) * gn[:, sl] * _silu(r_ref[:, sl]))
    y_gla = jnp.concatenate(parts, axis=1)
    ys = jax.nn.gelu(ys_ref[...])
    y_s5 = ys * _sigmoid(_dot(ys, wglu_ref[...]))
    merged = (_sigmoid(gg_ref[...]) * _dot(y_gla, wbg_ref[...])
              + _sigmoid(gs_ref[...]) * _dot(y_s5, wbs_ref[...])
              + _sigmoid(gl_ref[...]) * _dot(yl_ref[...], wbl_ref[...]))
    mix = _dot(merged, wo_ref[...])
    gate1 = mod_ref[0, :, 2 * D:3 * D]
    out_ref[...] = _layer_norm(alpha * x_ref[...] + gate1 * mix, lg_ref[...], lb_ref[...])


def _merge(x2, mod, o_gla, P, ys, ylru, lw, T, alpha):
    N, D = x2.shape
    W = 3 * D // 2
    Bm = mod.shape[0]
    rcol, gcol = 2, 6
    tm = _row_tile(N, T, Bm > 1, 256)
    per_b = T // tm
    mod_map = (lambda i: (i // per_b, 0, 0)) if Bm > 1 else (lambda i: (0, 0, 0))
    row = lambda width, cb=0: pl.BlockSpec((tm, width), lambda i: (i, cb))
    full = lambda a: pl.BlockSpec(a.shape, lambda i: (0,) * a.ndim)
    weights = [lw["gnorm"], lw["w_glu"], lw["w_br_gla"], lw["w_br_s5"], lw["w_br_lru"],
               lw["w_out"], lw["ln1_g"], lw["ln1_b"]]
    return pl.pallas_call(
        functools.partial(_merge_kernel, D=D, alpha=alpha),
        grid=(N // tm,),
        in_specs=[row(D), pl.BlockSpec((1, 1, 6 * D), mod_map), row(D), row(D, rcol), row(D),
                  row(W), row(D, gcol), row(D, gcol + 1), row(D, gcol + 2)]
        + [full(w) for w in weights],
        out_specs=row(D),
        out_shape=jax.ShapeDtypeStruct((N, D), F32),
        compiler_params=_cparams("parallel"),
        name="merge",
    )(x2, mod, o_gla, P, ys, ylru, P, P, P, *weights)


def _mlp_kernel(x_ref, mod_ref, w1_ref, w2_ref, lg_ref, lb_ref, out_ref, h_ref, acc_ref, *, D, alpha):
    j = pl.program_id(1)

    @pl.when(j == 0)
    def _():
        shift = mod_ref[0, :, 3 * D:4 * D]
        scale = mod_ref[0, :, 4 * D:5 * D]
        h_ref[...] = (x_ref[...] * (1.0 + scale) + shift).astype(BF16)
        acc_ref[...] = jnp.zeros_like(acc_ref)

    hid = jnp.dot(h_ref[...], w1_ref[...], preferred_element_type=F32)
    hid = jnp.square(jnp.maximum(hid, 0.0))
    acc_ref[...] += _dot(hid, w2_ref[...])

    @pl.when(j == pl.num_programs(1) - 1)
    def _():
        gate2 = mod_ref[0, :, 5 * D:6 * D]
        out_ref[...] = _layer_norm(alpha * x_ref[...] + gate2 * acc_ref[...], lg_ref[...], lb_ref[...])


def _mlp(x2, mod, lw, T, alpha):
    N, D = x2.shape
    HID = lw["w_mlp_in"].shape[1]
    Bm = mod.shape[0]
    tm = _row_tile(N, T, Bm > 1, 1024)
    th = min(HID, 1024)
    per_b = T // tm
    mod_map = (lambda i, j: (i // per_b, 0, 0)) if Bm > 1 else (lambda i, j: (0, 0, 0))
    return pl.pallas_call(
        functools.partial(_mlp_kernel, D=D, alpha=alpha),
        grid=(N // tm, HID // th),
        in_specs=[pl.BlockSpec((tm, D), lambda i, j: (i, 0)),
                  pl.BlockSpec((1, 1, 6 * D), mod_map),
                  pl.BlockSpec((D, th), lambda i, j: (0, j)),
                  pl.BlockSpec((th, D), lambda i, j: (j, 0)),
                  pl.BlockSpec((1, D), lambda i, j: (0, 0)),
                  pl.BlockSpec((1, D), lambda i, j: (0, 0))],
        out_specs=pl.BlockSpec((tm, D), lambda i, j: (i, 0)),
        out_shape=jax.ShapeDtypeStruct((N, D), F32),
        scratch_shapes=[pltpu.VMEM((tm, D), BF16), pltpu.VMEM((tm, D), F32)],
        compiler_params=_cparams("parallel", "arbitrary"),
        name="mlp",
    )(x2, mod, lw["w_mlp_in"], lw["w_mlp_out"], lw["ln2_g"], lw["ln2_b"])


def _pack_layer_weights(l, D, w_in, gla_w_gate, gla_b_gate, gla_norm_g, s5_d, s5_w_glu,
                        lru_conv_w, lru_conv_b, lru_w_a, lru_b_a, lru_w_i, lru_b_i, lru_lam,
                        w_br_gla, w_br_s5, w_br_lru, w_out, ln1_g, ln1_b, ln2_g, ln2_b,
                        w_mlp_in, w_mlp_out):
    H = GLA_HEADS
    KEY = D // 2
    DK = KEY // H
    W = 3 * D // 2
    CB = 2 * LRU_BLOCK
    NB = W // LRU_BLOCK
    widths = (KEY, KEY, D, D, 2 * GATE_RANK, D, W, W, 3 * D)
    offs = [0]
    for wd in widths:
        offs.append(offs[-1] + wd)
    wl = w_in[l]
    piece = lambda i: wl[:, offs[i]:offs[i + 1]]
    w_pack = jnp.concatenate([piece(0), piece(1), piece(2), piece(3), piece(6), piece(7),
                              piece(8)], axis=1).astype(BF16)
    w_glr = jnp.concatenate([piece(4), jnp.zeros((D, LANES - 2 * GATE_RANK), wl.dtype)], axis=1).astype(BF16)
    w_ut = piece(5).T.astype(BF16)
    wgate = gla_w_gate[l]
    wg = jnp.zeros((H, LANES, 2 * DK), F32)
    wg = wg.at[:, 0:GATE_RANK, 0:DK].set(wgate[0].reshape(GATE_RANK, H, DK).transpose(1, 0, 2))
    wg = wg.at[:, GATE_RANK:2 * GATE_RANK, DK:].set(wgate[1].reshape(GATE_RANK, H, DK).transpose(1, 0, 2))
    bgate = gla_b_gate[l].reshape(2, H, DK).transpose(1, 0, 2).reshape(H, 1, 2 * DK)
    lru_wg = 0.5 * jnp.concatenate([lru_w_a[l], lru_w_i[l]], axis=-1)
    lru_wg = lru_wg.reshape(2, W // CB, CB // LRU_BLOCK, LRU_BLOCK, 2 * LRU_BLOCK).astype(BF16)
    lru_bg = 0.5 * jnp.concatenate([lru_b_a[l].reshape(2, W // CB, 1, CB), lru_b_i[l].reshape(2, W // CB, 1, CB)],
                                   axis=-1)
    return {
        "w_pack": w_pack, "w_glr": w_glr, "w_ut": w_ut, "gla_wg": wg.astype(BF16), "gla_bg": bgate,
        "lru": {"conv_w": lru_conv_w[l].reshape(LRU_CONV, W // CB, CB).transpose(1, 0, 2),
                "conv_b": lru_conv_b[l].reshape(W // CB, 1, CB),
                "wg": lru_wg, "bg": lru_bg, "lam": lru_lam[l].reshape(2, W // CB, 1, CB)},
        "gnorm": gla_norm_g[l].reshape(1, D), "s5_dcol": jnp.tile(s5_d[l].reshape(D // S5_GROUP, 1, S5_GROUP), (1, S5_SUB, 1)).reshape(
            D // S5_GROUP, S5_SUB * S5_GROUP, 1),
        "w_glu": s5_w_glu[l].astype(BF16), "w_br_gla": w_br_gla[l].astype(BF16),
        "w_br_s5": w_br_s5[l].astype(BF16), "w_br_lru": w_br_lru[l].astype(BF16),
        "w_out": w_out[l].astype(BF16),
        "ln1_g": ln1_g[l].reshape(1, D), "ln1_b": ln1_b[l].reshape(1, D),
        "ln2_g": ln2_g[l].reshape(1, D), "ln2_b": ln2_b[l].reshape(1, D),
        "w_mlp_in": w_mlp_in[l].astype(BF16), "w_mlp_out": w_mlp_out[l].astype(BF16),
    }


def _trunk_layer(x2, mod, lw, s5ops, init, B, T, D, alpha):
    N = B * T
    G = D // S5_GROUP
    S = S5_SUB
    R = T // S
    Pn = S5_STATE
    P, glr = _in_proj(x2, mod, lw["w_pack"], lw["w_glr"], T)
    gla0 = s5re0 = s5im0 = lru0 = None
    if init is not None:
        gla0, s5re0, s5im0, lru0 = init
        s5re0 = s5re0.transpose(2, 0, 1, 3).reshape(G, B, 2 * Pn)
        s5im0 = s5im0.transpose(2, 0, 1, 3).reshape(G, B, 2 * Pn)
    o_gla, gla_fin = _gla_core(P, glr, lw["gla_wg"], lw["gla_bg"], gla0, B, T, D)
    if s5re0 is None:
        s5re0 = jnp.zeros((G, B, 2 * Pn), F32)
        s5im0 = jnp.zeros((G, B, 2 * Pn), F32)
    xt = _s5_proj_t(x2, mod, lw["w_ut"], B, T)
    yt, fre, fim = _s5_core_t(xt, lw["s5_dcol"], s5ops, s5re0, s5im0, R, B)
    ys = _s5_rows(yt, B, T, D)
    s5re_fin = fre.reshape(G, B, 2, Pn).transpose(1, 2, 0, 3)
    s5im_fin = fim.reshape(G, B, 2, Pn).transpose(1, 2, 0, 3)
    P3 = P.reshape(B, T, P.shape[1])
    hf, lf = _lru_sweep(P3, lw["lru"], lru0, None, False, B, T, D)
    ylru, lb = _lru_sweep(P3, lw["lru"], lru0, hf, True, B, T, D)
    lru_fin = jnp.stack([lf, lb], axis=1)
    x1 = _merge(x2, mod, o_gla, P, ys, ylru.reshape(N, -1), lw, T, alpha)
    x3 = _mlp(x1, mod, lw, T, alpha)
    return x3, (gla_fin, s5re_fin, s5im_fin, lru_fin)


def kernel(x_prompt, x_sample, state_gla, state_s5_re, state_s5_im, state_lru, c, c_ctx, w_ada, b_ada, w_in, gla_w_gate, gla_b_gate, gla_norm_g, s5_lam_re, s5_lam_im, s5_log_step, s5_b_re, s5_b_im, s5_c_re, s5_c_im, s5_d, s5_w_glu, lru_conv_w, lru_conv_b, lru_w_a, lru_b_a, lru_w_i, lru_b_i, lru_lam, w_br_gla, w_br_s5, w_br_lru, w_out, ln1_g, ln1_b, ln2_g, ln2_b, w_mlp_in, w_mlp_out):
    Bp, Tp, D = x_prompt.shape
    Bs, Ts, _ = x_sample.shape
    L = w_in.shape[0]
    alpha = (2.0 * L) ** 0.25

    n_rows = -(-(Bs + 1) // SUBLANES) * SUBLANES
    cc = jnp.concatenate([c, c_ctx[None], jnp.zeros((n_rows - Bs - 1, D), F32)], axis=0)
    mod = _ada_mod(cc, w_ada, b_ada)
    s5ops_all = _s5_operators(s5_lam_re, s5_lam_im, s5_log_step, s5_b_re, s5_b_im, s5_c_re, s5_c_im)

    xp = x_prompt.reshape(Bp * Tp, D)
    xs = _add_pos(x_sample, _grid_pos_table(Ts, D)).reshape(Bs * Ts, D)
    fins = []
    for l in range(L):
        lw = _pack_layer_weights(l, D, w_in, gla_w_gate, gla_b_gate, gla_norm_g, s5_d, s5_w_glu,
                                 lru_conv_w, lru_conv_b, lru_w_a, lru_b_a, lru_w_i, lru_b_i, lru_lam,
                                 w_br_gla, w_br_s5, w_br_lru, w_out, ln1_g, ln1_b, ln2_g, ln2_b,
                                 w_mlp_in, w_mlp_out)
        s5ops = tuple(a[l] for a in s5ops_all)
        mod_ctx = mod[l, Bs:Bs + 1].reshape(1, 1, 6 * D)
        mod_lat = mod[l, :Bs].reshape(Bs, 1, 6 * D)
        xp, fin = _trunk_layer(xp, mod_ctx, lw, s5ops, None, Bp, Tp, D, alpha)
        fins.append(fin)
        cache = (state_gla[:, l], state_s5_re[:, l], state_s5_im[:, l], state_lru[:, l])
        xs, _ = _trunk_layer(xs, mod_lat, lw, s5ops, cache, Bs, Ts, D, alpha)
    sdt = x_prompt.dtype
    new_states = tuple(jnp.stack([f[i] for f in fins], axis=1).astype(sdt) for i in range(4))
    return (xp.reshape(Bp, Tp, D), xs.reshape(Bs, Ts, D)) + new_states
```

```python
import functools
import math

import jax
import jax.numpy as jnp
from jax import lax
from jax.experimental import pallas as pl
from jax.experimental.pallas import tpu as pltpu

F32 = jnp.float32
BF16 = jnp.bfloat16
HIGHEST = lax.Precision.HIGHEST

LANES = 128
SUBLANES = 8
VMEM_LIMIT = 56 * 1024 * 1024

GRID_W = 64
GLA_HEADS = 4
GATE_RANK = 16
GLA_TAU = 16.0
GLA_CHUNK = 64
GLA_BLOCK = 256
S5_GROUP = 16
S5_STATE = 64
S5_SUB = 16
S5_PROJ_ROWS = 1024
LRU_BLOCK = 128
LRU_CONV = 4
LRU_C = 8.0
LN_EPS = 1e-5
F32_TINY = 1.1754944e-38
SCAN_UNROLL = 8


def _cparams(*sem):
    return pltpu.CompilerParams(dimension_semantics=sem, vmem_limit_bytes=VMEM_LIMIT)


def _dot(a, b):
    return jnp.dot(a.astype(BF16), b.astype(BF16), preferred_element_type=F32)


def _dot_nt(a, b):
    return lax.dot_general(a.astype(BF16), b.astype(BF16), (((1,), (1,)), ((), ())),
                           preferred_element_type=F32)


def _dot_tn(a, b):
    return lax.dot_general(a.astype(BF16), b.astype(BF16), (((0,), (0,)), ((), ())),
                           preferred_element_type=F32)


def _dot_f32(a, b):
    return jnp.dot(a, b, precision=HIGHEST, preferred_element_type=F32)


def _layer_norm(z, g, b):
    mu = jnp.mean(z, axis=-1, keepdims=True)
    zc = z - mu
    var = jnp.mean(zc * zc, axis=-1, keepdims=True)
    return zc * lax.rsqrt(var + LN_EPS) * g + b


def _sigmoid(x):
    return 0.5 * jnp.tanh(0.5 * x) + 0.5


def _silu(x):
    return x * _sigmoid(x)


def _ada_kernel(cc_ref, w_ref, b_ref, o_ref):
    s = _silu(cc_ref[...])
    o_ref[0] = _dot_f32(s, w_ref[0]) + b_ref[0]


def _ada_mod(cc, w_ada, b_ada):
    L, D, D6 = w_ada.shape
    R = cc.shape[0]
    tn = D6 // 4
    return pl.pallas_call(
        _ada_kernel,
        grid=(L, D6 // tn),
        in_specs=[pl.BlockSpec((R, D), lambda l, j: (0, 0)),
                  pl.BlockSpec((1, D, tn), lambda l, j: (l, 0, j)),
                  pl.BlockSpec((1, 1, tn), lambda l, j: (l, 0, j))],
        out_specs=pl.BlockSpec((1, R, tn), lambda l, j: (l, 0, j)),
        out_shape=jax.ShapeDtypeStruct((L, R, D6), F32),
        compiler_params=_cparams("parallel", "parallel"),
        name="ada_mod",
    )(cc, w_ada, b_ada.reshape(L, 1, D6))


def _addpos_kernel(x_ref, p_ref, o_ref):
    o_ref[0] = x_ref[0] + p_ref[...]


def _add_pos(x, pos):
    B, T, D = x.shape
    tt = min(T, 512)
    return pl.pallas_call(
        _addpos_kernel,
        grid=(T // tt, B),
        in_specs=[pl.BlockSpec((1, tt, D), lambda t, b: (b, t, 0)),
                  pl.BlockSpec((tt, D), lambda t, b: (t, 0))],
        out_specs=pl.BlockSpec((1, tt, D), lambda t, b: (b, t, 0)),
        out_shape=jax.ShapeDtypeStruct((B, T, D), F32),
        compiler_params=_cparams("parallel", "parallel"),
        name="add_pos",
    )(x, pos)


def _grid_pos_table(n_tokens, dim):
    rows = n_tokens // GRID_W
    quarter = dim // 4
    omega = 1.0 / (10000.0 ** (jnp.arange(quarter, dtype=F32) / quarter))
    r = jnp.arange(rows, dtype=F32)[:, None, None] * omega
    cl = jnp.arange(GRID_W, dtype=F32)[None, :, None] * omega
    shape = (rows, GRID_W, quarter)
    emb = jnp.concatenate([jnp.broadcast_to(jnp.sin(r), shape), jnp.broadcast_to(jnp.cos(r), shape),
                           jnp.broadcast_to(jnp.sin(cl), shape), jnp.broadcast_to(jnp.cos(cl), shape)], axis=-1)
    return emb.reshape(rows * GRID_W, dim)


def _inproj_kernel(x_ref, mod_ref, w_ref, wglr_ref, o_ref, glr_ref, h_ref, *, D):
    @pl.when(pl.program_id(1) == 0)
    def _():
        shift = mod_ref[0, :, 0:D]
        scale = mod_ref[0, :, D:2 * D]
        h = (x_ref[...] * (1.0 + scale) + shift).astype(BF16)
        h_ref[...] = h
        glr_ref[...] = jnp.dot(h, wglr_ref[...], preferred_element_type=F32)

    o_ref[...] = jnp.dot(h_ref[...], w_ref[...], preferred_element_type=F32)


def _row_tile(N, T, per_batch_mod, want):
    tm = min(want, T if per_batch_mod else N)
    while N % tm or (per_batch_mod and T % tm):
        tm //= 2
    return tm


def _in_proj(x2, mod, w_pack, w_glr, T):
    N, D = x2.shape
    NC = w_pack.shape[1]
    Bm = mod.shape[0]
    tm = _row_tile(N, T, Bm > 1, 1024)
    tn = D
    per_b = T // tm
    mod_map = (lambda i, j: (i // per_b, 0, 0)) if Bm > 1 else (lambda i, j: (0, 0, 0))
    return pl.pallas_call(
        functools.partial(_inproj_kernel, D=D),
        grid=(N // tm, NC // tn),
        in_specs=[pl.BlockSpec((tm, D), lambda i, j: (i, 0)),
                  pl.BlockSpec((1, 1, 6 * D), mod_map),
                  pl.BlockSpec((D, tn), lambda i, j: (0, j)),
                  pl.BlockSpec((D, LANES), lambda i, j: (0, 0))],
        out_specs=[pl.BlockSpec((tm, tn), lambda i, j: (i, j)),
                   pl.BlockSpec((tm, LANES), lambda i, j: (i, 0))],
        out_shape=[jax.ShapeDtypeStruct((N, NC), F32), jax.ShapeDtypeStruct((N, LANES), F32)],
        scratch_shapes=[pltpu.VMEM((tm, D), BF16)],
        compiler_params=_cparams("parallel", "arbitrary"),
        name="in_proj",
    )(x2, mod, w_pack, w_glr)


def _gla_kernel(*refs, T, DK, DV, has_init):
    if has_init:
        (q_ref, k_ref, v_ref, glr_ref, wg_ref, bg_ref, s0_ref,
         o_ref, sfin_ref, qb_ref, keb_ref, db_ref, sf_ref, sb_ref) = refs
    else:
        (q_ref, k_ref, v_ref, glr_ref, wg_ref, bg_ref,
         o_ref, sfin_ref, qb_ref, keb_ref, db_ref, sf_ref, sb_ref) = refs
    C = GLA_CHUNK
    BLK = GLA_BLOCK
    NS = BLK // C
    nblk = T // BLK
    qscale = DK ** -0.5
    shift = C.bit_length() - 1
    row = lax.broadcasted_iota(jnp.int32, (BLK, BLK), 0)
    col = lax.broadcasted_iota(jnp.int32, (BLK, BLK), 1)
    same = lax.shift_right_logical(row, shift) == lax.shift_right_logical(col, shift)
    lower = jnp.logical_and(same, row >= col)
    upper = jnp.logical_and(same, row <= col)
    tri = jnp.where(lower, 1.0, 0.0).astype(BF16)
    chunk_of_row = lax.shift_right_logical(lax.broadcasted_iota(jnp.int32, (BLK, DK), 0), shift)
    wg = wg_ref[0]
    bg = bg_ref[0]

    def chunk_cols(ke):
        return jnp.concatenate([jnp.where(chunk_of_row == c, ke, 0.0) for c in range(NS)], axis=1).astype(BF16)

    def per_chunk_last(x, r):
        return jnp.concatenate([jnp.broadcast_to(x[c * C + r:c * C + r + 1, :], (C, x.shape[1]))
                                for c in range(NS)], axis=0)

    if has_init:
        sf_ref[...] = s0_ref[0, 0, 0].T
        sb_ref[...] = s0_ref[0, 1, 0].T
    else:
        sf_ref[...] = jnp.zeros((DV, DK), F32)
        sb_ref[...] = jnp.zeros((DV, DK), F32)

    def fwd_body(i, carry):
        r0 = pl.multiple_of(i * BLK, BLK)
        rows = pl.ds(r0, BLK)
        qc = q_ref[rows, :] * qscale
        kc = k_ref[rows, :]
        vc = v_ref[rows, :]
        logits = _dot(glr_ref[rows, :], wg) + bg
        la = (jnp.minimum(logits, 0.0) - jnp.log(1.0 + jnp.exp(-jnp.abs(logits)))) * (1.0 / GLA_TAU)
        p1 = la.astype(BF16)
        p2 = (la - p1.astype(F32)).astype(BF16)
        pre = jnp.dot(tri, p1, preferred_element_type=F32) + jnp.dot(tri, p2, preferred_element_type=F32)
        tot = per_chunk_last(pre, C - 1)
        cum_f = pre[:, :DK]
        last_f = tot[:, :DK]
        cum_b = tot[:, DK:] - pre[:, DK:] + la[:, DK:]
        last_b = tot[:, DK:]
        q_f = qc * jnp.exp(cum_f)
        k_f = kc * jnp.exp(-cum_f)
        ke_f = kc * jnp.exp(last_f - cum_f)
        q_b = qc * jnp.exp(cum_b)
        k_b = kc * jnp.exp(-cum_b)
        ke_b = kc * jnp.exp(last_b - cum_b)
        sc = jnp.where(lower, _dot_nt(q_f, k_f), 0.0) + jnp.where(upper, _dot_nt(q_b, k_b), 0.0)
        o_blk = _dot(sc, vc)
        u_f = jnp.dot(vc.T.astype(BF16), chunk_cols(ke_f), preferred_element_type=F32)
        d_f = jnp.exp(last_f)
        s = sf_ref[...]
        for c in range(NS):
            rc = slice(c * C, (c + 1) * C)
            o_ref[pl.ds(r0 + c * C, C), :] = o_blk[rc] + _dot_nt(q_f[rc], s)
            s = s * d_f[c * C:c * C + 1, :] + u_f[:, c * DK:(c + 1) * DK]
        sf_ref[...] = s
        qb_ref[rows, :] = q_b
        keb_ref[rows, :] = ke_b
        d_b = jnp.exp(last_b)
        db_ref[pl.ds(i, 1), :] = jnp.concatenate([d_b[c * C:c * C + 1, :] for c in range(NS)], axis=1)
        return carry

    lax.fori_loop(0, nblk, fwd_body, 0, unroll=min(8, nblk))

    def bwd_body(j, carry):
        i = nblk - 1 - j
        r0 = pl.multiple_of(i * BLK, BLK)
        rows = pl.ds(r0, BLK)
        q_b = qb_ref[rows, :]
        u_b = jnp.dot(v_ref[rows, :].T.astype(BF16), chunk_cols(keb_ref[rows, :]), preferred_element_type=F32)
        d_b = db_ref[pl.ds(i, 1), :]
        s = sb_ref[...]
        for c in reversed(range(NS)):
            rc = pl.ds(r0 + c * C, C)
            o_ref[rc, :] = o_ref[rc, :] + _dot_nt(q_b[c * C:(c + 1) * C], s)
            s = s * d_b[:, c * DK:(c + 1) * DK] + u_b[:, c * DK:(c + 1) * DK]
        sb_ref[...] = s
        return carry

    lax.fori_loop(0, nblk, bwd_body, 0, unroll=min(8, nblk))

    sfin_ref[0, 0, 0] = sf_ref[...].T
    sfin_ref[0, 1, 0] = sb_ref[...].T


def _gla_core(P, glr, wg, bg, s0, B, T, D):
    H = GLA_HEADS
    DK = D // 2 // H
    DV = D // H
    N = B * T
    has_init = s0 is not None
    kcol = (D // 2) // DK
    vcol = D // DV
    in_specs = [pl.BlockSpec((T, DK), lambda b, h: (b, h)),
                pl.BlockSpec((T, DK), lambda b, h: (b, kcol + h)),
                pl.BlockSpec((T, DV), lambda b, h: (b, vcol + h)),
                pl.BlockSpec((T, LANES), lambda b, h: (b, 0)),
                pl.BlockSpec((1, LANES, 2 * DK), lambda b, h: (h, 0, 0)),
                pl.BlockSpec((1, 1, 2 * DK), lambda b, h: (h, 0, 0))]
    args = [P, P, P, glr, wg, bg]
    if has_init:
        in_specs.append(pl.BlockSpec((1, 2, 1, DK, DV), lambda b, h: (b, 0, h, 0, 0)))
        args.append(s0)
    o, sfin = pl.pallas_call(
        functools.partial(_gla_kernel, T=T, DK=DK, DV=DV, has_init=has_init),
        grid=(B, H),
        in_specs=in_specs,
        out_specs=[pl.BlockSpec((T, DV), lambda b, h: (b, h)),
                   pl.BlockSpec((1, 2, 1, DK, DV), lambda b, h: (b, 0, h, 0, 0))],
        out_shape=[jax.ShapeDtypeStruct((N, D), F32),
                   jax.ShapeDtypeStruct((B, 2, H, DK, DV), F32)],
        scratch_shapes=[pltpu.VMEM((T, DK), F32), pltpu.VMEM((T, DK), F32),
                        pltpu.VMEM((T // GLA_BLOCK, (GLA_BLOCK // GLA_CHUNK) * DK), F32),
                        pltpu.VMEM((DV, DK), F32), pltpu.VMEM((DV, DK), F32)],
        compiler_params=_cparams("parallel", "parallel"),
        name="gla_core",
    )(*args)
    return o, sfin


def _s5prep_kernel(lr_ref, li_ref, ls_ref, btr_ref, bti_ref, cr_ref, ci_ref,
                   kt_ref, k0_ref, wre_ref, wim_ref, cpre_ref, cpimn_ref, pre_ref, pim_ref):
    I = S5_GROUP
    btr = btr_ref[0]
    bti = bti_ref[0]
    c_re = cr_ref[0]
    c_im = ci_ref[0]
    batched_nt = (((2,), (2,)), ((0,), (0,)))
    for d in range(2):
        lr = lr_ref[0, d]
        li = li_ref[0, d]
        step = jnp.exp(ls_ref[0, d])
        mag = jnp.exp(lr * step)
        ang = li * step
        a_re = mag * jnp.cos(ang)
        a_im = mag * jnp.sin(ang)
        den = lr * lr + li * li
        nr = a_re - 1.0
        f_re = (nr * lr + a_im * li) / den
        f_im = (a_im * lr - nr * li) / den
        bb_re = f_re[:, None, :] * btr - f_im[:, None, :] * bti
        bb_im = f_re[:, None, :] * bti + f_im[:, None, :] * btr
        p_re = jnp.ones_like(lr)
        p_im = jnp.zeros_like(lr)
        for e in range(S5_SUB + 1):
            pr = p_re[:, None, :]
            pi = p_im[:, None, :]
            cp_re = c_re * pr - c_im * pi
            cp_im = c_re * pi + c_im * pr
            cpre_ref[0, d, :, e * I:(e + 1) * I, :] = cp_re
            cpimn_ref[0, d, :, e * I:(e + 1) * I, :] = -cp_im
            if e < S5_SUB:
                wre_ref[0, d, :, e * I:(e + 1) * I, :] = pr * bb_re - pi * bb_im
                wim_ref[0, d, :, e * I:(e + 1) * I, :] = pr * bb_im + pi * bb_re
                lag = (lax.dot_general(cp_re, bb_re, batched_nt, precision=HIGHEST, preferred_element_type=F32)
                       - lax.dot_general(cp_im, bb_im, batched_nt, precision=HIGHEST, preferred_element_type=F32))
                kt_ref[0, d, :, e * I:(e + 1) * I, :] = lag
                if e == 0:
                    k0_ref[0] = lag if d == 0 else k0_ref[0] + lag
                p_re, p_im = p_re * a_re - p_im * a_im, p_re * a_im + p_im * a_re
        pre_ref[0, d] = p_re
        pim_ref[0, d] = p_im


def _s5_operators(lam_re, lam_im, log_step, b_re, b_im, c_re, c_im):
    L, _, G, Pn = lam_re.shape
    I = S5_GROUP
    S = S5_SUB
    ls = jnp.broadcast_to(log_step[..., None], lam_re.shape)
    btr = jnp.swapaxes(b_re, -1, -2)
    bti = jnp.swapaxes(b_im, -1, -2)
    Gb = SUBLANES
    lam_spec = pl.BlockSpec((1, 2, Gb, Pn), lambda l, g: (l, 0, g, 0))
    gip_spec = pl.BlockSpec((1, Gb, I, Pn), lambda l, g: (l, g, 0, 0))

    def out_spec(rows, last):
        return pl.BlockSpec((1, 2, Gb, rows, last), lambda l, g: (l, 0, g, 0, 0))

    kt, k0, wre, wim, cpre, cpimn, pre, pim = pl.pallas_call(
        _s5prep_kernel,
        grid=(L, G // Gb),
        in_specs=[lam_spec, lam_spec, lam_spec, gip_spec, gip_spec, gip_spec, gip_spec],
        out_specs=[out_spec(S * I, I), pl.BlockSpec((1, Gb, I, I), lambda l, g: (l, g, 0, 0)),
                   out_spec(S * I, Pn), out_spec(S * I, Pn),
                   out_spec((S + 1) * I, Pn), out_spec((S + 1) * I, Pn), lam_spec, lam_spec],
        out_shape=[jax.ShapeDtypeStruct((L, 2, G, S * I, I), F32),
                   jax.ShapeDtypeStruct((L, G, I, I), F32),
                   jax.ShapeDtypeStruct((L, 2, G, S * I, Pn), F32),
                   jax.ShapeDtypeStruct((L, 2, G, S * I, Pn), F32),
                   jax.ShapeDtypeStruct((L, 2, G, (S + 1) * I, Pn), F32),
                   jax.ShapeDtypeStruct((L, 2, G, (S + 1) * I, Pn), F32),
                   jax.ShapeDtypeStruct((L, 2, G, Pn), F32),
                   jax.ShapeDtypeStruct((L, 2, G, Pn), F32)],
        compiler_params=_cparams("parallel", "parallel"),
        name="s5_prep",
    )(lam_re, lam_im, ls, btr, bti, c_re, c_im)

    kt = kt.reshape(L, 2, G, S, I, I)
    table = jnp.concatenate([kt[:, 0, :, 1:], kt[:, 1, :, 1:], k0[:, :, None]], axis=2)
    s_idx = jnp.arange(S)[:, None]
    t_idx = jnp.arange(S)[None, :]
    pick = jnp.where(t_idx > s_idx, t_idx - s_idx - 1,
                     jnp.where(t_idx < s_idx, (S - 1) + s_idx - t_idx - 1, 2 * S - 2))
    m = table[:, :, pick]
    m = m.transpose(0, 1, 3, 4, 2, 5).reshape(L, G, S * I, S * I)

    def by_pos(w, reverse):
        w = w.reshape(L, G, S, I, Pn)
        if reverse:
            w = w[:, :, ::-1]
        return w.reshape(L, G, S * I, Pn).swapaxes(-1, -2)

    wall = jnp.concatenate([by_pos(wre[:, 0], True), by_pos(wre[:, 1], False),
                            by_pos(wim[:, 0], True), by_pos(wim[:, 1], False)], axis=2)

    def readout(cp):
        cp = cp.reshape(L, 2, G, S + 1, I, Pn)
        f = cp[:, 0, :, 1:]
        b = cp[:, 1, :, 1:][:, :, ::-1]
        return jnp.concatenate([f.reshape(L, G, S * I, Pn), b.reshape(L, G, S * I, Pn)], axis=-1)

    vre = readout(cpre)
    vim = readout(cpimn)
    are = jnp.concatenate([pre[:, 0], pre[:, 1]], axis=-1)[:, :, None, :]
    aim = jnp.concatenate([pim[:, 0], pim[:, 1]], axis=-1)[:, :, None, :]
    return m, wall, vre, vim, are, aim


def _s5projt_kernel(x_ref, mod_ref, wt_ref, o_ref, *, D, R):
    Bm = mod_ref.shape[0]
    span = x_ref.shape[0] // Bm
    parts = []
    for b in range(Bm):
        shift = mod_ref[b, :, 0:D]
        scale = mod_ref[b, :, D:2 * D]
        parts.append((x_ref[b * span:(b + 1) * span, :] * (1.0 + scale) + shift).astype(BF16))
    h = parts[0] if Bm == 1 else jnp.concatenate(parts, axis=0)
    ut = lax.dot_general(wt_ref[...], h, (((1,), (1,)), ((), ())), preferred_element_type=F32)
    o_ref[...] = ut.reshape(o_ref.shape).astype(o_ref.dtype)


def _s5_proj_t(x2, mod, w_ut, B, T):
    N, D = x2.shape
    S = S5_SUB
    R = T // S
    G = D // S5_GROUP
    Bm = mod.shape[0]
    xv = x2.reshape(B * R, S * D)
    rows = B * R
    while rows > S5_PROJ_ROWS and rows % 2 == 0 and (rows // 2) % R == 0 and (rows // 2) % LANES == 0:
        rows //= 2
    nrb = B * R // rows
    mods = max(1, Bm * rows // (B * R))
    mod_spec = pl.BlockSpec((mods, 1, 6 * D), lambda i, s: (i if Bm > 1 else 0, 0, 0))
    return pl.pallas_call(
        functools.partial(_s5projt_kernel, D=D, R=R),
        grid=(nrb, S),
        in_specs=[pl.BlockSpec((rows, D), lambda i, s: (i, s)),
                  mod_spec,
                  pl.BlockSpec((D, D), lambda i, s: (0, 0))],
        out_specs=pl.BlockSpec((G, S5_GROUP, rows), lambda i, s: (0, s, i)),
        out_shape=jax.ShapeDtypeStruct((G, S * S5_GROUP, B * R), F32),
        compiler_params=_cparams("parallel", "parallel"),
        name="s5_proj_t",
    )(xv, mod, w_ut)


def _s5t_kernel(xt_ref, dcol_ref, mt_ref, wt_ref, vret_ref, vimt_ref, are_ref, aim_ref,
                s0re_ref, s0im_ref, yt_ref, fre_ref, fim_ref,
                zt_ref, zre_ref, zim_ref, hfre_ref, hfim_ref, hbre_ref, hbim_ref, *, R, B, chunk_major):
    Pn = S5_STATE
    xt32 = xt_ref[0]
    xt = xt32.astype(BF16)
    yt_ref[0] = _dot(mt_ref[0], xt) + dcol_ref[0] * xt32
    zt_ref[...] = jnp.dot(wt_ref[0].astype(BF16), xt, preferred_element_type=F32)
    if chunk_major:
        for b in range(B):
            zb = zt_ref[:, b * R:(b + 1) * R].T
            zre_ref[pl.ds(b, R, stride=B), :] = zb[:, :2 * Pn]
            zim_ref[pl.ds(b, R, stride=B), :] = zb[:, 2 * Pn:]
        step_rows = lambda r: pl.ds(pl.multiple_of(r * B, B), B)
    else:
        z = zt_ref[...].T
        zre_ref[...] = z[:, :2 * Pn]
        zim_ref[...] = z[:, 2 * Pn:]
        step_rows = lambda r: pl.ds(r, B, stride=R)
    a_re = are_ref[0]
    a_im = aim_ref[0]
    h0 = (s0re_ref[0], s0im_ref[0])

    def advance(h, rows):
        hr, hi = h
        return (hr * a_re - hi * a_im + zre_ref[rows, :], hr * a_im + hi * a_re + zim_ref[rows, :])

    def step(r, carry):
        hf, hb = carry
        rows_f = step_rows(r)
        rows_b = step_rows(R - 1 - r)
        hfre_ref[rows_f, :] = hf[0]
        hfim_ref[rows_f, :] = hf[1]
        hbre_ref[rows_b, :] = hb[0]
        hbim_ref[rows_b, :] = hb[1]
        return advance(hf, rows_f), advance(hb, rows_b)

    hf, hb = lax.fori_loop(0, R, step, (h0, h0), unroll=SCAN_UNROLL)

    def own_half(f, b):
        return jnp.where(lax.broadcasted_iota(jnp.int32, f.shape, 1) < Pn, f, b)

    fre_ref[0] = own_half(hf[0], hb[0])
    fim_ref[0] = own_half(hf[1], hb[1])
    vret = vret_ref[0]
    vimt = vimt_ref[0]
    if chunk_major:
        for b in range(B):
            sel = pl.ds(b, R, stride=B)
            hre_b = own_half(hfre_ref[sel, :], hbre_ref[sel, :])
            him_b = own_half(hfim_ref[sel, :], hbim_ref[sel, :])
            cols = slice(b * R, (b + 1) * R)
            yt_ref[0, :, cols] = yt_ref[0, :, cols] + _dot_nt(vret, hre_b) + _dot_nt(vimt, him_b)
    else:
        hre = own_half(hfre_ref[...], hbre_ref[...])
        him = own_half(hfim_ref[...], hbim_ref[...])
        yt_ref[0] = yt_ref[0] + _dot_nt(vret, hre) + _dot_nt(vimt, him)


def _s5_core_t(xt, dcol, ops, s0re, s0im, R, B):
    G, K, BR = xt.shape
    mt, wt, vret, vimt, are, aim = ops
    P2 = 2 * S5_STATE
    chunk_major = B == SUBLANES and R % LANES == 0
    gspec = lambda shape: pl.BlockSpec((1,) + shape, lambda g: (g, 0, 0))
    return pl.pallas_call(
        functools.partial(_s5t_kernel, R=R, B=B, chunk_major=chunk_major),
        grid=(G,),
        in_specs=[gspec((K, BR)), gspec((K, 1)), gspec((K, K)), gspec((K, K)),
                  gspec((K, P2)), gspec((K, P2)), gspec((1, P2)), gspec((1, P2)),
                  gspec((B, P2)), gspec((B, P2))],
        out_specs=[gspec((K, BR)), gspec((B, P2)), gspec((B, P2))],
        out_shape=[jax.ShapeDtypeStruct((G, K, BR), F32),
                   jax.ShapeDtypeStruct((G, B, P2), F32),
                   jax.ShapeDtypeStruct((G, B, P2), F32)],
        scratch_shapes=[pltpu.VMEM((K, BR), F32)] + [pltpu.VMEM((BR, P2), F32) for _ in range(6)],
        compiler_params=_cparams("parallel"),
        name="s5_core_t",
    )(xt, dcol, mt, wt, vret, vimt, are, aim, s0re, s0im)


def _s5rows_kernel(yt_ref, o_ref):
    I = S5_GROUP
    nc = yt_ref.shape[-1]
    for t in range(S5_SUB):
        blk = yt_ref[:, t * I:(t + 1) * I, :]
        o_ref[pl.ds(t, nc, stride=S5_SUB), :] = blk.reshape(LANES, nc).T


def _s5_rows(yt, B, T, D):
    G = D // S5_GROUP
    BR = B * T // S5_SUB
    gpt = LANES // S5_GROUP
    nc = next(n for n in (4 * LANES, 2 * LANES, LANES) if BR % n == 0)
    return pl.pallas_call(
        _s5rows_kernel,
        grid=(BR // nc, G // gpt),
        in_specs=[pl.BlockSpec((gpt, S5_SUB * S5_GROUP, nc), lambda i, j: (j, 0, i))],
        out_specs=pl.BlockSpec((nc * S5_SUB, LANES), lambda i, j: (i, j)),
        out_shape=jax.ShapeDtypeStruct((B * T, D), F32),
        compiler_params=_cparams("parallel", "parallel"),
        name="s5_rows",
    )(yt)


def _lru_kernel(*refs, Tt, CB, reverse, has_init):
    refs = list(refs)
    lx_ref, prev_ref, next_ref, cw_ref, cb_ref, wg_ref, bgate_ref, lam_ref = refs[:8]
    pos = 8
    h0_ref = None
    if has_init:
        h0_ref = refs[pos]
        pos += 1
    out_ref, fin_ref, x_scr, a_scr, b_scr, h_scr, carry_ref = refs[pos:pos + 7]
    NBk = CB // LRU_BLOCK
    k = pl.program_id(2)
    nk = pl.num_programs(2)
    first_tile = k == 0

    @pl.when(first_tile)
    def _():
        if has_init:
            carry_ref[...] = h0_ref[...]
        else:
            carry_ref[...] = jnp.zeros((SUBLANES, CB), F32)

    tpos = (nk - 1 - k) if reverse else k
    has_prev = tpos > 0
    has_next = tpos < nk - 1
    cw = cw_ref[0]
    cbias = cb_ref[0]
    bgate = bgate_ref[0]
    lam = lam_ref[0]
    half_cfac = (-0.5 * LRU_C) * (jnp.maximum(-lam, 0.0) + jnp.log1p(jnp.exp(-jnp.abs(lam))))

    left = LRU_CONV // 2
    for b in range(SUBLANES):
        cur = lx_ref[b]
        pv = jnp.where(has_prev, prev_ref[b], 0.0)
        nx = jnp.where(has_next, next_ref[b], 0.0)
        for c in range(NBk):
            sl = slice(c * LRU_BLOCK, (c + 1) * LRU_BLOCK)
            x_scr[c, pl.ds(left * SUBLANES + b, Tt, stride=SUBLANES), :] = cur[:, sl]
            for q in range(left):
                x_scr[c, pl.ds(q * SUBLANES + b, 1), :] = pv[SUBLANES - left + q:SUBLANES - left + q + 1, sl]
            for q in range(LRU_CONV - 1 - left):
                x_scr[c, pl.ds((Tt + left + q) * SUBLANES + b, 1), :] = nx[q:q + 1, sl]

    RC = min(Tt * SUBLANES, 512)
    for c in range(NBk):
        sl = slice(c * LRU_BLOCK, (c + 1) * LRU_BLOCK)
        bias_c = jnp.concatenate([bgate[:, sl], bgate[:, CB + c * LRU_BLOCK: CB + (c + 1) * LRU_BLOCK]], axis=1)
        for r0 in range(0, Tt * SUBLANES, RC):
            xc = cbias[:, sl] + sum(cw[j:j + 1, sl] * x_scr[c, pl.ds(r0 + j * SUBLANES, RC), :]
                                    for j in range(LRU_CONV))
            th = jnp.tanh(_dot(xc, wg_ref[0, c]) + bias_c)
            log_a = half_cfac[:, sl] * th[:, :LRU_BLOCK] + half_cfac[:, sl]
            a = jnp.exp(log_a)
            om = -jnp.tanh(log_a) * (1.0 + a * a)
            root = om * lax.rsqrt(jnp.maximum(om, F32_TINY))
            bt = root * ((0.5 * th[:, LRU_BLOCK:] + 0.5) * xc)
            a_scr[c, pl.ds(r0, RC), :] = a
            b_scr[c, pl.ds(r0, RC), :] = bt

    def step(s, h):
        t = (Tt - 1 - s) if reverse else s
        rows = pl.ds(pl.multiple_of(t * SUBLANES, SUBLANES), SUBLANES)
        new = []
        for c in range(NBk):
            hc = a_scr[c, rows, :] * h[c] + b_scr[c, rows, :]
            h_scr[c, rows, :] = hc
            new.append(hc)
        return tuple(new)

    h_init = tuple(carry_ref[:, c * LRU_BLOCK:(c + 1) * LRU_BLOCK] for c in range(NBk))
    h_last = lax.fori_loop(0, Tt, step, h_init, unroll=SCAN_UNROLL)
    for c in range(NBk):
        carry_ref[:, c * LRU_BLOCK:(c + 1) * LRU_BLOCK] = h_last[c]
        fin_ref[:, c * LRU_BLOCK:(c + 1) * LRU_BLOCK] = h_last[c]

    for b in range(SUBLANES):
        out_ref[b] = jnp.concatenate([h_scr[c, pl.ds(b, Tt, stride=SUBLANES), :] for c in range(NBk)], axis=1)


def _lru_sweep(P3, lw, h0, reverse, B, T, D):
    W = 3 * D // 2
    CB = 2 * LRU_BLOCK
    Tt = min(T, 256)
    nk = T // Tt
    nb8 = Tt // SUBLANES
    lxcol = (D // 2 + D // 2 + D + D) // CB
    d = 1 if reverse else 0
    has_init = h0 is not None

    def tmap(k):
        return (nk - 1 - k) if reverse else k

    in_specs = [
        pl.BlockSpec((SUBLANES, Tt, CB), lambda g, j, k: (g, tmap(k), lxcol + j)),
        pl.BlockSpec((SUBLANES, SUBLANES, CB),
                     lambda g, j, k: (g, jnp.maximum(tmap(k) * nb8 - 1, 0), lxcol + j)),
        pl.BlockSpec((SUBLANES, SUBLANES, CB),
                     lambda g, j, k: (g, jnp.minimum((tmap(k) + 1) * nb8, T // SUBLANES - 1), lxcol + j)),
        pl.BlockSpec((1, LRU_CONV, CB), lambda g, j, k: (j, 0, 0)),
        pl.BlockSpec((1, 1, CB), lambda g, j, k: (j, 0, 0)),
        pl.BlockSpec((1, CB // LRU_BLOCK, LRU_BLOCK, 2 * LRU_BLOCK), lambda g, j, k: (j, 0, 0, 0)),
        pl.BlockSpec((1, 1, 2 * CB), lambda g, j, k: (j, 0, 0)),
        pl.BlockSpec((1, 1, CB), lambda g, j, k: (j, 0, 0)),
    ]
    args = [P3, P3, P3, lw["conv_w"], lw["conv_b"], lw["wg"][d], lw["bg"][d], lw["lam"][d]]
    if has_init:
        in_specs.append(pl.BlockSpec((SUBLANES, CB), lambda g, j, k: (g, j)))
        args.append(h0[:, d])
    out, fin = pl.pallas_call(
        functools.partial(_lru_kernel, Tt=Tt, CB=CB, reverse=reverse, has_init=has_init),
        grid=(B // SUBLANES, W // CB, nk),
        in_specs=in_specs,
        out_specs=[pl.BlockSpec((SUBLANES, Tt, CB), lambda g, j, k: (g, tmap(k), j)),
                   pl.BlockSpec((SUBLANES, CB), lambda g, j, k: (g, j))],
        out_shape=[jax.ShapeDtypeStruct((B, T, W), F32), jax.ShapeDtypeStruct((B, W), F32)],
        scratch_shapes=[pltpu.VMEM((CB // LRU_BLOCK, (Tt + LRU_CONV) * SUBLANES, LRU_BLOCK), F32)]
        + [pltpu.VMEM((CB // LRU_BLOCK, Tt * SUBLANES, LRU_BLOCK), F32) for _ in range(3)]
        + [pltpu.VMEM((SUBLANES, CB), F32)],
        compiler_params=_cparams("parallel", "parallel", "arbitrary"),
        name="lru_bwd" if reverse else "lru_fwd",
    )(*args)
    return out, fin


def _merge_kernel(x_ref, mod_ref, o_ref, r_ref, ys_ref, hf_ref, hb_ref, ly_ref, gg_ref, gs_ref, gl_ref,
                  gn_ref, wglu_ref, wbg_ref, wbs_ref, wbl_ref, wo_ref, lg_ref, lb_ref,
                  out_ref, *, D, alpha):
    DV = D // GLA_HEADS
    gn = gn_ref[...]
    parts = []
    for h in range(GLA_HEADS):
        sl = slice(h * DV, (h + 1) * DV)
        o = o_ref[:, sl]
        mu = jnp.mean(o, axis=-1, keepdims=True)
        oc = o - mu
        var = jnp.mean(oc * oc, axis=-1, keepdims=True)
        parts.append(oc * lax.rsqrt(var + LN_EPS) * gn[:, sl] * _silu(r_ref[:, sl]))
    y_gla = jnp.concatenate(parts, axis=1)
    ys = jax.nn.gelu(ys_ref[...])
    y_s5 = ys * _sigmoid(_dot(ys, wglu_ref[...]))
    p_lru = None
    for k0 in range(0, hf_ref.shape[1], 4 * LANES):
        ks = slice(k0, k0 + 4 * LANES)
        y_k = (hf_ref[:, ks] + hb_ref[:, ks]) * jax.nn.gelu(ly_ref[:, ks])
        part = _dot(y_k, wbl_ref[ks, :])
        p_lru = part if p_lru is None else p_lru + part
    merged = (_sigmoid(gg_ref[...]) * _dot(y_gla, wbg_ref[...])
              + _sigmoid(gs_ref[...]) * _dot(y_s5, wbs_ref[...])
              + _sigmoid(gl_ref[...]) * p_lru)
    mix = _dot(merged, wo_ref[...])
    gate1 = mod_ref[0, :, 2 * D:3 * D]
    out_ref[...] = _layer_norm(alpha * x_ref[...] + gate1 * mix, lg_ref[...], lb_ref[...])


def _merge(x2, mod, o_gla, P, ys, hf, hb, lw, T, alpha):
    N, D = x2.shape
    W = 3 * D // 2
    Bm = mod.shape[0]
    rcol, gcol = 2, 6
    lycol = (3 * D + W) // W
    tm = _row_tile(N, T, Bm > 1, 256)
    per_b = T // tm
    mod_map = (lambda i: (i // per_b, 0, 0)) if Bm > 1 else (lambda i: (0, 0, 0))
    row = lambda width, cb=0: pl.BlockSpec((tm, width), lambda i: (i, cb))
    full = lambda a: pl.BlockSpec(a.shape, lambda i: (0,) * a.ndim)
    weights = [lw["gnorm"], lw["w_glu"], lw["w_br_gla"], lw["w_br_s5"], lw["w_br_lru"],
               lw["w_out"], lw["ln1_g"], lw["ln1_b"]]
    return pl.pallas_call(
        functools.partial(_merge_kernel, D=D, alpha=alpha),
        grid=(N // tm,),
        in_specs=[row(D), pl.BlockSpec((1, 1, 6 * D), mod_map), row(D), row(D, rcol), row(D),
                  row(W), row(W), row(W, lycol), row(D, gcol), row(D, gcol + 1), row(D, gcol + 2)]
        + [full(w) for w in weights],
        out_specs=row(D),
        out_shape=jax.ShapeDtypeStruct((N, D), F32),
        compiler_params=_cparams("parallel"),
        name="merge",
    )(x2, mod, o_gla, P, ys, hf, hb, P, P, P, P, *weights)


def _mlp_kernel(x_ref, mod_ref, w1_ref, w2_ref, lg_ref, lb_ref, out_ref, h_ref, acc_ref, *, D, alpha):
    j = pl.program_id(1)

    @pl.when(j == 0)
    def _():
        shift = mod_ref[0, :, 3 * D:4 * D]
        scale = mod_ref[0, :, 4 * D:5 * D]
        h_ref[...] = (x_ref[...] * (1.0 + scale) + shift).astype(BF16)
        acc_ref[...] = jnp.zeros_like(acc_ref)

    hid = jnp.dot(h_ref[...], w1_ref[...], preferred_element_type=F32)
    hid = jnp.square(jnp.maximum(hid, 0.0))
    acc_ref[...] += _dot(hid, w2_ref[...])

    @pl.when(j == pl.num_programs(1) - 1)
    def _():
        gate2 = mod_ref[0, :, 5 * D:6 * D]
        out_ref[...] = _layer_norm(alpha * x_ref[...] + gate2 * acc_ref[...], lg_ref[...], lb_ref[...])


def _mlp(x2, mod, lw, T, alpha):
    N, D = x2.shape
    HID = lw["w_mlp_in"].shape[1]
    Bm = mod.shape[0]
    tm = _row_tile(N, T, Bm > 1, 1024)
    th = min(HID, 1024)
    per_b = T // tm
    mod_map = (lambda i, j: (i // per_b, 0, 0)) if Bm > 1 else (lambda i, j: (0, 0, 0))
    return pl.pallas_call(
        functools.partial(_mlp_kernel, D=D, alpha=alpha),
        grid=(N // tm, HID // th),
        in_specs=[pl.BlockSpec((tm, D), lambda i, j: (i, 0)),
                  pl.BlockSpec((1, 1, 6 * D), mod_map),
                  pl.BlockSpec((D, th), lambda i, j: (0, j)),
                  pl.BlockSpec((th, D), lambda i, j: (j, 0)),
                  pl.BlockSpec((1, D), lambda i, j: (0, 0)),
                  pl.BlockSpec((1, D), lambda i, j: (0, 0))],
        out_specs=pl.BlockSpec((tm, D), lambda i, j: (i, 0)),
        out_shape=jax.ShapeDtypeStruct((N, D), F32),
        scratch_shapes=[pltpu.VMEM((tm, D), BF16), pltpu.VMEM((tm, D), F32)],
        compiler_params=_cparams("parallel", "arbitrary"),
        name="mlp",
    )(x2, mod, lw["w_mlp_in"], lw["w_mlp_out"], lw["ln2_g"], lw["ln2_b"])


def _pack_layer_weights(l, D, w_in, gla_w_gate, gla_b_gate, gla_norm_g, s5_d, s5_w_glu,
                        lru_conv_w, lru_conv_b, lru_w_a, lru_b_a, lru_w_i, lru_b_i, lru_lam,
                        w_br_gla, w_br_s5, w_br_lru, w_out, ln1_g, ln1_b, ln2_g, ln2_b,
                        w_mlp_in, w_mlp_out):
    H = GLA_HEADS
    KEY = D // 2
    DK = KEY // H
    W = 3 * D // 2
    CB = 2 * LRU_BLOCK
    NB = W // LRU_BLOCK
    widths = (KEY, KEY, D, D, 2 * GATE_RANK, D, W, W, 3 * D)
    offs = [0]
    for wd in widths:
        offs.append(offs[-1] + wd)
    wl = w_in[l]
    piece = lambda i: wl[:, offs[i]:offs[i + 1]]
    w_pack = jnp.concatenate([piece(0), piece(1), piece(2), piece(3), piece(6), piece(7),
                              piece(8)], axis=1).astype(BF16)
    w_glr = jnp.concatenate([piece(4), jnp.zeros((D, LANES - 2 * GATE_RANK), wl.dtype)], axis=1).astype(BF16)
    w_ut = piece(5).T.astype(BF16)
    wgate = gla_w_gate[l]
    wg = jnp.zeros((H, LANES, 2 * DK), F32)
    wg = wg.at[:, 0:GATE_RANK, 0:DK].set(wgate[0].reshape(GATE_RANK, H, DK).transpose(1, 0, 2))
    wg = wg.at[:, GATE_RANK:2 * GATE_RANK, DK:].set(wgate[1].reshape(GATE_RANK, H, DK).transpose(1, 0, 2))
    bgate = gla_b_gate[l].reshape(2, H, DK).transpose(1, 0, 2).reshape(H, 1, 2 * DK)
    lru_wg = 0.5 * jnp.concatenate([lru_w_a[l], lru_w_i[l]], axis=-1)
    lru_wg = lru_wg.reshape(2, W // CB, CB // LRU_BLOCK, LRU_BLOCK, 2 * LRU_BLOCK).astype(BF16)
    lru_bg = 0.5 * jnp.concatenate([lru_b_a[l].reshape(2, W // CB, 1, CB), lru_b_i[l].reshape(2, W // CB, 1, CB)],
                                   axis=-1)
    return {
        "w_pack": w_pack, "w_glr": w_glr, "w_ut": w_ut, "gla_wg": wg.astype(BF16), "gla_bg": bgate,
        "lru": {"conv_w": lru_conv_w[l].reshape(LRU_CONV, W // CB, CB).transpose(1, 0, 2),
                "conv_b": lru_conv_b[l].reshape(W // CB, 1, CB),
                "wg": lru_wg, "bg": lru_bg, "lam": lru_lam[l].reshape(2, W // CB, 1, CB)},
        "gnorm": gla_norm_g[l].reshape(1, D), "s5_dcol": jnp.tile(s5_d[l].reshape(D // S5_GROUP, 1, S5_GROUP), (1, S5_SUB, 1)).reshape(
            D // S5_GROUP, S5_SUB * S5_GROUP, 1),
        "w_glu": s5_w_glu[l].astype(BF16), "w_br_gla": w_br_gla[l].astype(BF16),
        "w_br_s5": w_br_s5[l].astype(BF16), "w_br_lru": w_br_lru[l].astype(BF16),
        "w_out": w_out[l].astype(BF16),
        "ln1_g": ln1_g[l].reshape(1, D), "ln1_b": ln1_b[l].reshape(1, D),
        "ln2_g": ln2_g[l].reshape(1, D), "ln2_b": ln2_b[l].reshape(1, D),
        "w_mlp_in": w_mlp_in[l].astype(BF16), "w_mlp_out": w_mlp_out[l].astype(BF16),
    }


def _trunk_layer(x2, mod, lw, s5ops, init, B, T, D, alpha):
    N = B * T
    G = D // S5_GROUP
    S = S5_SUB
    R = T // S
    Pn = S5_STATE
    P, glr = _in_proj(x2, mod, lw["w_pack"], lw["w_glr"], T)
    gla0 = s5re0 = s5im0 = lru0 = None
    if init is not None:
        gla0, s5re0, s5im0, lru0 = init
        s5re0 = s5re0.transpose(2, 0, 1, 3).reshape(G, B, 2 * Pn)
        s5im0 = s5im0.transpose(2, 0, 1, 3).reshape(G, B, 2 * Pn)
    o_gla, gla_fin = _gla_core(P, glr, lw["gla_wg"], lw["gla_bg"], gla0, B, T, D)
    if s5re0 is None:
        s5re0 = jnp.zeros((G, B, 2 * Pn), F32)
        s5im0 = jnp.zeros((G, B, 2 * Pn), F32)
    xt = _s5_proj_t(x2, mod, lw["w_ut"], B, T)
    yt, fre, fim = _s5_core_t(xt, lw["s5_dcol"], s5ops, s5re0, s5im0, R, B)
    ys = _s5_rows(yt, B, T, D)
    s5re_fin = fre.reshape(G, B, 2, Pn).transpose(1, 2, 0, 3)
    s5im_fin = fim.reshape(G, B, 2, Pn).transpose(1, 2, 0, 3)
    P3 = P.reshape(B, T, P.shape[1])
    hf, lf = _lru_sweep(P3, lw["lru"], lru0, False, B, T, D)
    hb, lb = _lru_sweep(P3, lw["lru"], lru0, True, B, T, D)
    lru_fin = jnp.stack([lf, lb], axis=1)
    x1 = _merge(x2, mod, o_gla, P, ys, hf.reshape(N, -1), hb.reshape(N, -1), lw, T, alpha)
    x3 = _mlp(x1, mod, lw, T, alpha)
    return x3, (gla_fin, s5re_fin, s5im_fin, lru_fin)


def kernel(x_prompt, x_sample, state_gla, state_s5_re, state_s5_im, state_lru, c, c_ctx, w_ada, b_ada, w_in, gla_w_gate, gla_b_gate, gla_norm_g, s5_lam_re, s5_lam_im, s5_log_step, s5_b_re, s5_b_im, s5_c_re, s5_c_im, s5_d, s5_w_glu, lru_conv_w, lru_conv_b, lru_w_a, lru_b_a, lru_w_i, lru_b_i, lru_lam, w_br_gla, w_br_s5, w_br_lru, w_out, ln1_g, ln1_b, ln2_g, ln2_b, w_mlp_in, w_mlp_out):
    Bp, Tp, D = x_prompt.shape
    Bs, Ts, _ = x_sample.shape
    L = w_in.shape[0]
    alpha = (2.0 * L) ** 0.25

    n_rows = -(-(Bs + 1) // SUBLANES) * SUBLANES
    cc = jnp.concatenate([c, c_ctx[None], jnp.zeros((n_rows - Bs - 1, D), F32)], axis=0)
    mod = _ada_mod(cc, w_ada, b_ada)
    s5ops_all = _s5_operators(s5_lam_re, s5_lam_im, s5_log_step, s5_b_re, s5_b_im, s5_c_re, s5_c_im)

    xp = x_prompt.reshape(Bp * Tp, D)
    xs = _add_pos(x_sample, _grid_pos_table(Ts, D)).reshape(Bs * Ts, D)
    fins = []
    for l in range(L):
        lw = _pack_layer_weights(l, D, w_in, gla_w_gate, gla_b_gate, gla_norm_g, s5_d, s5_w_glu,
                                 lru_conv_w, lru_conv_b, lru_w_a, lru_b_a, lru_w_i, lru_b_i, lru_lam,
                                 w_br_gla, w_br_s5, w_br_lru, w_out, ln1_g, ln1_b, ln2_g, ln2_b,
                                 w_mlp_in, w_mlp_out)
        s5ops = tuple(a[l] for a in s5ops_all)
        mod_ctx = mod[l, Bs:Bs + 1].reshape(1, 1, 6 * D)
        mod_lat = mod[l, :Bs].reshape(Bs, 1, 6 * D)
        xp, fin = _trunk_layer(xp, mod_ctx, lw, s5ops, None, Bp, Tp, D, alpha)
        fins.append(fin)
        cache = (state_gla[:, l], state_s5_re[:, l], state_s5_im[:, l], state_lru[:, l])
        xs, _ = _trunk_layer(xs, mod_lat, lw, s5ops, cache, Bs, Ts, D, alpha)
    sdt = x_prompt.dtype
    new_states = tuple(jnp.stack([f[i] for f in fins], axis=1).astype(sdt) for i in range(4))
    return (xp.reshape(Bp, Tp, D), xs.reshape(Bs, Ts, D)) + new_states
```

```python
import functools
import math

import jax
import jax.numpy as jnp
from jax import lax
from jax.experimental import pallas as pl
from jax.experimental.pallas import tpu as pltpu

F32 = jnp.float32
BF16 = jnp.bfloat16
HIGHEST = lax.Precision.HIGHEST

LANES = 128
SUBLANES = 8
VMEM_LIMIT = 56 * 1024 * 1024

GRID_W = 64
GLA_HEADS = 4
GATE_RANK = 16
GLA_TAU = 16.0
GLA_CHUNK = 64
GLA_BLOCK = 256
S5_GROUP = 16
S5_STATE = 64
S5_SUB = 16
S5_PROJ_ROWS = 1024
LRU_BLOCK = 128
LRU_CONV = 4
LRU_C = 8.0
LN_EPS = 1e-5
F32_TINY = 1.1754944e-38
SCAN_UNROLL = 8


def _cparams(*sem):
    return pltpu.CompilerParams(dimension_semantics=sem, vmem_limit_bytes=VMEM_LIMIT)


def _dot(a, b):
    return jnp.dot(a.astype(BF16), b.astype(BF16), preferred_element_type=F32)


def _dot_nt(a, b):
    return lax.dot_general(a.astype(BF16), b.astype(BF16), (((1,), (1,)), ((), ())),
                           preferred_element_type=F32)


def _dot_tn(a, b):
    return lax.dot_general(a.astype(BF16), b.astype(BF16), (((0,), (0,)), ((), ())),
                           preferred_element_type=F32)


def _dot_f32(a, b):
    return jnp.dot(a, b, precision=HIGHEST, preferred_element_type=F32)


def _layer_norm(z, g, b):
    mu = jnp.mean(z, axis=-1, keepdims=True)
    zc = z - mu
    var = jnp.mean(zc * zc, axis=-1, keepdims=True)
    return zc * lax.rsqrt(var + LN_EPS) * g + b


def _sigmoid(x):
    return 0.5 * jnp.tanh(0.5 * x) + 0.5


def _silu(x):
    return x * _sigmoid(x)


def _ada_kernel(cc_ref, w_ref, b_ref, o_ref):
    s = _silu(cc_ref[...])
    o_ref[0] = _dot_f32(s, w_ref[0]) + b_ref[0]


def _ada_mod(cc, w_ada, b_ada):
    L, D, D6 = w_ada.shape
    R = cc.shape[0]
    tn = D6 // 4
    return pl.pallas_call(
        _ada_kernel,
        grid=(L, D6 // tn),
        in_specs=[pl.BlockSpec((R, D), lambda l, j: (0, 0)),
                  pl.BlockSpec((1, D, tn), lambda l, j: (l, 0, j)),
                  pl.BlockSpec((1, 1, tn), lambda l, j: (l, 0, j))],
        out_specs=pl.BlockSpec((1, R, tn), lambda l, j: (l, 0, j)),
        out_shape=jax.ShapeDtypeStruct((L, R, D6), F32),
        compiler_params=_cparams("parallel", "parallel"),
        name="ada_mod",
    )(cc, w_ada, b_ada.reshape(L, 1, D6))


def _addpos_kernel(x_ref, p_ref, o_ref):
    o_ref[0] = x_ref[0] + p_ref[...]


def _add_pos(x, pos):
    B, T, D = x.shape
    tt = min(T, 512)
    return pl.pallas_call(
        _addpos_kernel,
        grid=(T // tt, B),
        in_specs=[pl.BlockSpec((1, tt, D), lambda t, b: (b, t, 0)),
                  pl.BlockSpec((tt, D), lambda t, b: (t, 0))],
        out_specs=pl.BlockSpec((1, tt, D), lambda t, b: (b, t, 0)),
        out_shape=jax.ShapeDtypeStruct((B, T, D), F32),
        compiler_params=_cparams("parallel", "parallel"),
        name="add_pos",
    )(x, pos)


def _grid_pos_table(n_tokens, dim):
    rows = n_tokens // GRID_W
    quarter = dim // 4
    omega = 1.0 / (10000.0 ** (jnp.arange(quarter, dtype=F32) / quarter))
    r = jnp.arange(rows, dtype=F32)[:, None, None] * omega
    cl = jnp.arange(GRID_W, dtype=F32)[None, :, None] * omega
    shape = (rows, GRID_W, quarter)
    emb = jnp.concatenate([jnp.broadcast_to(jnp.sin(r), shape), jnp.broadcast_to(jnp.cos(r), shape),
                           jnp.broadcast_to(jnp.sin(cl), shape), jnp.broadcast_to(jnp.cos(cl), shape)], axis=-1)
    return emb.reshape(rows * GRID_W, dim)


def _inproj_kernel(x_ref, mod_ref, w_ref, wglr_ref, o_ref, glr_ref, h_ref, *, D):
    @pl.when(pl.program_id(1) == 0)
    def _():
        shift = mod_ref[0, :, 0:D]
        scale = mod_ref[0, :, D:2 * D]
        h = (x_ref[...] * (1.0 + scale) + shift).astype(BF16)
        h_ref[...] = h
        glr_ref[...] = jnp.dot(h, wglr_ref[...], preferred_element_type=F32)

    o_ref[...] = jnp.dot(h_ref[...], w_ref[...], preferred_element_type=F32)


def _row_tile(N, T, per_batch_mod, want):
    tm = min(want, T if per_batch_mod else N)
    while N % tm or (per_batch_mod and T % tm):
        tm //= 2
    return tm


def _in_proj(x2, mod, w_pack, w_glr, T):
    N, D = x2.shape
    NC = w_pack.shape[1]
    Bm = mod.shape[0]
    tm = _row_tile(N, T, Bm > 1, 1024)
    tn = D
    per_b = T // tm
    mod_map = (lambda i, j: (i // per_b, 0, 0)) if Bm > 1 else (lambda i, j: (0, 0, 0))
    return pl.pallas_call(
        functools.partial(_inproj_kernel, D=D),
        grid=(N // tm, NC // tn),
        in_specs=[pl.BlockSpec((tm, D), lambda i, j: (i, 0)),
                  pl.BlockSpec((1, 1, 6 * D), mod_map),
                  pl.BlockSpec((D, tn), lambda i, j: (0, j)),
                  pl.BlockSpec((D, LANES), lambda i, j: (0, 0))],
        out_specs=[pl.BlockSpec((tm, tn), lambda i, j: (i, j)),
                   pl.BlockSpec((tm, LANES), lambda i, j: (i, 0))],
        out_shape=[jax.ShapeDtypeStruct((N, NC), F32), jax.ShapeDtypeStruct((N, LANES), F32)],
        scratch_shapes=[pltpu.VMEM((tm, D), BF16)],
        compiler_params=_cparams("parallel", "arbitrary"),
        name="in_proj",
    )(x2, mod, w_pack, w_glr)


def _gla_kernel(*refs, T, DK, DV, has_init):
    if has_init:
        (q_ref, k_ref, v_ref, glr_ref, wg_ref, bg_ref, s0_ref,
         o_ref, sfin_ref, qb_ref, keb_ref, db_ref, sf_ref, sb_ref) = refs
    else:
        (q_ref, k_ref, v_ref, glr_ref, wg_ref, bg_ref,
         o_ref, sfin_ref, qb_ref, keb_ref, db_ref, sf_ref, sb_ref) = refs
    C = GLA_CHUNK
    BLK = GLA_BLOCK
    NS = BLK // C
    nblk = T // BLK
    qscale = DK ** -0.5
    shift = C.bit_length() - 1
    row = lax.broadcasted_iota(jnp.int32, (BLK, BLK), 0)
    col = lax.broadcasted_iota(jnp.int32, (BLK, BLK), 1)
    same = lax.shift_right_logical(row, shift) == lax.shift_right_logical(col, shift)
    lower = jnp.logical_and(same, row >= col)
    upper = jnp.logical_and(same, row <= col)
    tri = jnp.where(lower, 1.0, 0.0).astype(BF16)
    chunk_of_row = lax.shift_right_logical(lax.broadcasted_iota(jnp.int32, (BLK, DK), 0), shift)
    wg = wg_ref[0]
    bg = bg_ref[0]

    def chunk_cols(ke):
        return jnp.concatenate([jnp.where(chunk_of_row == c, ke, 0.0) for c in range(NS)], axis=1).astype(BF16)

    def per_chunk_last(x, r):
        return jnp.concatenate([jnp.broadcast_to(x[c * C + r:c * C + r + 1, :], (C, x.shape[1]))
                                for c in range(NS)], axis=0)

    if has_init:
        sf_ref[...] = s0_ref[0, 0, 0].T
        sb_ref[...] = s0_ref[0, 1, 0].T
    else:
        sf_ref[...] = jnp.zeros((DV, DK), F32)
        sb_ref[...] = jnp.zeros((DV, DK), F32)

    def fwd_body(i, carry):
        r0 = pl.multiple_of(i * BLK, BLK)
        rows = pl.ds(r0, BLK)
        qc = q_ref[rows, :] * qscale
        kc = k_ref[rows, :]
        vc = v_ref[rows, :]
        logits = _dot(glr_ref[rows, :], wg) + bg
        la = (jnp.minimum(logits, 0.0) - jnp.log(1.0 + jnp.exp(-jnp.abs(logits)))) * (1.0 / GLA_TAU)
        p1 = la.astype(BF16)
        p2 = (la - p1.astype(F32)).astype(BF16)
        pre = jnp.dot(tri, p1, preferred_element_type=F32) + jnp.dot(tri, p2, preferred_element_type=F32)
        tot = per_chunk_last(pre, C - 1)
        cum_f = pre[:, :DK]
        last_f = tot[:, :DK]
        cum_b = tot[:, DK:] - pre[:, DK:] + la[:, DK:]
        last_b = tot[:, DK:]
        q_f = qc * jnp.exp(cum_f)
        k_f = kc * jnp.exp(-cum_f)
        ke_f = kc * jnp.exp(last_f - cum_f)
        q_b = qc * jnp.exp(cum_b)
        k_b = kc * jnp.exp(-cum_b)
        ke_b = kc * jnp.exp(last_b - cum_b)
        sc = jnp.where(lower, _dot_nt(q_f, k_f), 0.0) + jnp.where(upper, _dot_nt(q_b, k_b), 0.0)
        o_blk = _dot(sc, vc)
        u_f = jnp.dot(vc.T.astype(BF16), chunk_cols(ke_f), preferred_element_type=F32)
        d_f = jnp.exp(last_f)
        s = sf_ref[...]
        for c in range(NS):
            rc = slice(c * C, (c + 1) * C)
            o_ref[pl.ds(r0 + c * C, C), :] = o_blk[rc] + _dot_nt(q_f[rc], s)
            s = s * d_f[c * C:c * C + 1, :] + u_f[:, c * DK:(c + 1) * DK]
        sf_ref[...] = s
        qb_ref[rows, :] = q_b
        keb_ref[rows, :] = ke_b
        d_b = jnp.exp(last_b)
        db_ref[pl.ds(i, 1), :] = jnp.concatenate([d_b[c * C:c * C + 1, :] for c in range(NS)], axis=1)
        return carry

    lax.fori_loop(0, nblk, fwd_body, 0, unroll=min(8, nblk))

    def bwd_body(j, carry):
        i = nblk - 1 - j
        r0 = pl.multiple_of(i * BLK, BLK)
        rows = pl.ds(r0, BLK)
        q_b = qb_ref[rows, :]
        u_b = jnp.dot(v_ref[rows, :].T.astype(BF16), chunk_cols(keb_ref[rows, :]), preferred_element_type=F32)
        d_b = db_ref[pl.ds(i, 1), :]
        s = sb_ref[...]
        for c in reversed(range(NS)):
            rc = pl.ds(r0 + c * C, C)
            o_ref[rc, :] = o_ref[rc, :] + _dot_nt(q_b[c * C:(c + 1) * C], s)
            s = s * d_b[:, c * DK:(c + 1) * DK] + u_b[:, c * DK:(c + 1) * DK]
        sb_ref[...] = s
        return carry

    lax.fori_loop(0, nblk, bwd_body, 0, unroll=min(8, nblk))

    sfin_ref[0, 0, 0] = sf_ref[...].T
    sfin_ref[0, 1, 0] = sb_ref[...].T


def _gla_core(P, glr, wg, bg, s0, B, T, D):
    H = GLA_HEADS
    DK = D // 2 // H
    DV = D // H
    N = B * T
    has_init = s0 is not None
    kcol = (D // 2) // DK
    vcol = D // DV
    in_specs = [pl.BlockSpec((T, DK), lambda b, h: (b, h)),
                pl.BlockSpec((T, DK), lambda b, h: (b, kcol + h)),
                pl.BlockSpec((T, DV), lambda b, h: (b, vcol + h)),
                pl.BlockSpec((T, LANES), lambda b, h: (b, 0)),
                pl.BlockSpec((1, LANES, 2 * DK), lambda b, h: (h, 0, 0)),
                pl.BlockSpec((1, 1, 2 * DK), lambda b, h: (h, 0, 0))]
    args = [P, P, P, glr, wg, bg]
    if has_init:
        in_specs.append(pl.BlockSpec((1, 2, 1, DK, DV), lambda b, h: (b, 0, h, 0, 0)))
        args.append(s0)
    o, sfin = pl.pallas_call(
        functools.partial(_gla_kernel, T=T, DK=DK, DV=DV, has_init=has_init),
        grid=(B, H),
        in_specs=in_specs,
        out_specs=[pl.BlockSpec((T, DV), lambda b, h: (b, h)),
                   pl.BlockSpec((1, 2, 1, DK, DV), lambda b, h: (b, 0, h, 0, 0))],
        out_shape=[jax.ShapeDtypeStruct((N, D), F32),
                   jax.ShapeDtypeStruct((B, 2, H, DK, DV), F32)],
        scratch_shapes=[pltpu.VMEM((T, DK), F32), pltpu.VMEM((T, DK), F32),
                        pltpu.VMEM((T // GLA_BLOCK, (GLA_BLOCK // GLA_CHUNK) * DK), F32),
                        pltpu.VMEM((DV, DK), F32), pltpu.VMEM((DV, DK), F32)],
        compiler_params=_cparams("parallel", "parallel"),
        name="gla_core",
    )(*args)
    return o, sfin


def _s5prep_kernel(lr_ref, li_ref, ls_ref, btr_ref, bti_ref, cr_ref, ci_ref,
                   kt_ref, k0_ref, wre_ref, wim_ref, cpre_ref, cpimn_ref, pre_ref, pim_ref):
    I = S5_GROUP
    btr = btr_ref[0]
    bti = bti_ref[0]
    c_re = cr_ref[0]
    c_im = ci_ref[0]
    batched_nt = (((2,), (2,)), ((0,), (0,)))
    for d in range(2):
        lr = lr_ref[0, d]
        li = li_ref[0, d]
        step = jnp.exp(ls_ref[0, d])
        mag = jnp.exp(lr * step)
        ang = li * step
        a_re = mag * jnp.cos(ang)
        a_im = mag * jnp.sin(ang)
        den = lr * lr + li * li
        nr = a_re - 1.0
        f_re = (nr * lr + a_im * li) / den
        f_im = (a_im * lr - nr * li) / den
        bb_re = f_re[:, None, :] * btr - f_im[:, None, :] * bti
        bb_im = f_re[:, None, :] * bti + f_im[:, None, :] * btr
        p_re = jnp.ones_like(lr)
        p_im = jnp.zeros_like(lr)
        for e in range(S5_SUB + 1):
            pr = p_re[:, None, :]
            pi = p_im[:, None, :]
            cp_re = c_re * pr - c_im * pi
            cp_im = c_re * pi + c_im * pr
            cpre_ref[0, d, :, e * I:(e + 1) * I, :] = cp_re
            cpimn_ref[0, d, :, e * I:(e + 1) * I, :] = -cp_im
            if e < S5_SUB:
                wre_ref[0, d, :, e * I:(e + 1) * I, :] = pr * bb_re - pi * bb_im
                wim_ref[0, d, :, e * I:(e + 1) * I, :] = pr * bb_im + pi * bb_re
                lag = (lax.dot_general(cp_re, bb_re, batched_nt, precision=HIGHEST, preferred_element_type=F32)
                       - lax.dot_general(cp_im, bb_im, batched_nt, precision=HIGHEST, preferred_element_type=F32))
                kt_ref[0, d, :, e * I:(e + 1) * I, :] = lag
                if e == 0:
                    k0_ref[0] = lag if d == 0 else k0_ref[0] + lag
                p_re, p_im = p_re * a_re - p_im * a_im, p_re * a_im + p_im * a_re
        pre_ref[0, d] = p_re
        pim_ref[0, d] = p_im


def _s5_operators(lam_re, lam_im, log_step, b_re, b_im, c_re, c_im):
    L, _, G, Pn = lam_re.shape
    I = S5_GROUP
    S = S5_SUB
    ls = jnp.broadcast_to(log_step[..., None], lam_re.shape)
    btr = jnp.swapaxes(b_re, -1, -2)
    bti = jnp.swapaxes(b_im, -1, -2)
    Gb = SUBLANES
    lam_spec = pl.BlockSpec((1, 2, Gb, Pn), lambda l, g: (l, 0, g, 0))
    gip_spec = pl.BlockSpec((1, Gb, I, Pn), lambda l, g: (l, g, 0, 0))

    def out_spec(rows, last):
        return pl.BlockSpec((1, 2, Gb, rows, last), lambda l, g: (l, 0, g, 0, 0))

    kt, k0, wre, wim, cpre, cpimn, pre, pim = pl.pallas_call(
        _s5prep_kernel,
        grid=(L, G // Gb),
        in_specs=[lam_spec, lam_spec, lam_spec, gip_spec, gip_spec, gip_spec, gip_spec],
        out_specs=[out_spec(S * I, I), pl.BlockSpec((1, Gb, I, I), lambda l, g: (l, g, 0, 0)),
                   out_spec(S * I, Pn), out_spec(S * I, Pn),
                   out_spec((S + 1) * I, Pn), out_spec((S + 1) * I, Pn), lam_spec, lam_spec],
        out_shape=[jax.ShapeDtypeStruct((L, 2, G, S * I, I), F32),
                   jax.ShapeDtypeStruct((L, G, I, I), F32),
                   jax.ShapeDtypeStruct((L, 2, G, S * I, Pn), F32),
                   jax.ShapeDtypeStruct((L, 2, G, S * I, Pn), F32),
                   jax.ShapeDtypeStruct((L, 2, G, (S + 1) * I, Pn), F32),
                   jax.ShapeDtypeStruct((L, 2, G, (S + 1) * I, Pn), F32),
                   jax.ShapeDtypeStruct((L, 2, G, Pn), F32),
                   jax.ShapeDtypeStruct((L, 2, G, Pn), F32)],
        compiler_params=_cparams("parallel", "parallel"),
        name="s5_prep",
    )(lam_re, lam_im, ls, btr, bti, c_re, c_im)

    kt = kt.reshape(L, 2, G, S, I, I)
    table = jnp.concatenate([kt[:, 0, :, 1:], kt[:, 1, :, 1:], k0[:, :, None]], axis=2)
    s_idx = jnp.arange(S)[:, None]
    t_idx = jnp.arange(S)[None, :]
    pick = jnp.where(t_idx > s_idx, t_idx - s_idx - 1,
                     jnp.where(t_idx < s_idx, (S - 1) + s_idx - t_idx - 1, 2 * S - 2))
    m = table[:, :, pick]
    m = m.transpose(0, 1, 3, 4, 2, 5).reshape(L, G, S * I, S * I)

    def by_pos(w, reverse):
        w = w.reshape(L, G, S, I, Pn)
        if reverse:
            w = w[:, :, ::-1]
        return w.reshape(L, G, S * I, Pn).swapaxes(-1, -2)

    wall = jnp.concatenate([by_pos(wre[:, 0], True), by_pos(wre[:, 1], False),
                            by_pos(wim[:, 0], True), by_pos(wim[:, 1], False)], axis=2)

    def readout(cp):
        cp = cp.reshape(L, 2, G, S + 1, I, Pn)
        f = cp[:, 0, :, 1:]
        b = cp[:, 1, :, 1:][:, :, ::-1]
        return jnp.concatenate([f.reshape(L, G, S * I, Pn), b.reshape(L, G, S * I, Pn)], axis=-1)

    vre = readout(cpre)
    vim = readout(cpimn)
    are = jnp.concatenate([pre[:, 0], pre[:, 1]], axis=-1)[:, :, None, :]
    aim = jnp.concatenate([pim[:, 0], pim[:, 1]], axis=-1)[:, :, None, :]
    return m, wall, vre, vim, are, aim


def _s5projt_kernel(x_ref, mod_ref, wt_ref, o_ref, *, D, R):
    Bm = mod_ref.shape[0]
    span = x_ref.shape[0] // Bm
    parts = []
    for b in range(Bm):
        shift = mod_ref[b, :, 0:D]
        scale = mod_ref[b, :, D:2 * D]
        parts.append((x_ref[b * span:(b + 1) * span, :] * (1.0 + scale) + shift).astype(BF16))
    h = parts[0] if Bm == 1 else jnp.concatenate(parts, axis=0)
    ut = lax.dot_general(wt_ref[...], h, (((1,), (1,)), ((), ())), preferred_element_type=F32)
    o_ref[...] = ut.reshape(o_ref.shape).astype(o_ref.dtype)


def _s5_proj_t(x2, mod, w_ut, B, T):
    N, D = x2.shape
    S = S5_SUB
    R = T // S
    G = D // S5_GROUP
    Bm = mod.shape[0]
    xv = x2.reshape(B * R, S * D)
    rows = B * R
    while rows > S5_PROJ_ROWS and rows % 2 == 0 and (rows // 2) % R == 0 and (rows // 2) % LANES == 0:
        rows //= 2
    nrb = B * R // rows
    mods = max(1, Bm * rows // (B * R))
    mod_spec = pl.BlockSpec((mods, 1, 6 * D), lambda i, s: (i if Bm > 1 else 0, 0, 0))
    return pl.pallas_call(
        functools.partial(_s5projt_kernel, D=D, R=R),
        grid=(nrb, S),
        in_specs=[pl.BlockSpec((rows, D), lambda i, s: (i, s)),
                  mod_spec,
                  pl.BlockSpec((D, D), lambda i, s: (0, 0))],
        out_specs=pl.BlockSpec((G, S5_GROUP, rows), lambda i, s: (0, s, i)),
        out_shape=jax.ShapeDtypeStruct((G, S * S5_GROUP, B * R), F32),
        compiler_params=_cparams("parallel", "parallel"),
        name="s5_proj_t",
    )(xv, mod, w_ut)


def _s5t_kernel(xt_ref, dcol_ref, mt_ref, wt_ref, vret_ref, vimt_ref, are_ref, aim_ref,
                s0re_ref, s0im_ref, yt_ref, fre_ref, fim_ref,
                zt_ref, zre_ref, zim_ref, hfre_ref, hfim_ref, hbre_ref, hbim_ref, *, R, B, chunk_major):
    Pn = S5_STATE
    xt32 = xt_ref[0]
    xt = xt32.astype(BF16)
    yt_ref[0] = _dot(mt_ref[0], xt) + dcol_ref[0] * xt32
    zt_ref[...] = jnp.dot(wt_ref[0].astype(BF16), xt, preferred_element_type=F32)
    if chunk_major:
        for b in range(B):
            zb = zt_ref[:, b * R:(b + 1) * R].T
            zre_ref[pl.ds(b, R, stride=B), :] = zb[:, :2 * Pn]
            zim_ref[pl.ds(b, R, stride=B), :] = zb[:, 2 * Pn:]
        step_rows = lambda r: pl.ds(pl.multiple_of(r * B, B), B)
    else:
        z = zt_ref[...].T
        zre_ref[...] = z[:, :2 * Pn]
        zim_ref[...] = z[:, 2 * Pn:]
        step_rows = lambda r: pl.ds(r, B, stride=R)
    a_re = are_ref[0]
    a_im = aim_ref[0]
    h0 = (s0re_ref[0], s0im_ref[0])

    def advance(h, rows):
        hr, hi = h
        return (hr * a_re - hi * a_im + zre_ref[rows, :], hr * a_im + hi * a_re + zim_ref[rows, :])

    def step(r, carry):
        hf, hb = carry
        rows_f = step_rows(r)
        rows_b = step_rows(R - 1 - r)
        hfre_ref[rows_f, :] = hf[0]
        hfim_ref[rows_f, :] = hf[1]
        hbre_ref[rows_b, :] = hb[0]
        hbim_ref[rows_b, :] = hb[1]
        return advance(hf, rows_f), advance(hb, rows_b)

    hf, hb = lax.fori_loop(0, R, step, (h0, h0), unroll=SCAN_UNROLL)

    def own_half(f, b):
        return jnp.where(lax.broadcasted_iota(jnp.int32, f.shape, 1) < Pn, f, b)

    fre_ref[0] = own_half(hf[0], hb[0])
    fim_ref[0] = own_half(hf[1], hb[1])
    vret = vret_ref[0]
    vimt = vimt_ref[0]
    if chunk_major:
        for b in range(B):
            sel = pl.ds(b, R, stride=B)
            hre_b = own_half(hfre_ref[sel, :], hbre_ref[sel, :])
            him_b = own_half(hfim_ref[sel, :], hbim_ref[sel, :])
            cols = slice(b * R, (b + 1) * R)
            yt_ref[0, :, cols] = yt_ref[0, :, cols] + _dot_nt(vret, hre_b) + _dot_nt(vimt, him_b)
    else:
        hre = own_half(hfre_ref[...], hbre_ref[...])
        him = own_half(hfim_ref[...], hbim_ref[...])
        yt_ref[0] = yt_ref[0] + _dot_nt(vret, hre) + _dot_nt(vimt, him)


def _s5_core_t(xt, dcol, ops, s0re, s0im, R, B):
    G, K, BR = xt.shape
    mt, wt, vret, vimt, are, aim = ops
    P2 = 2 * S5_STATE
    chunk_major = B == SUBLANES and R % LANES == 0
    gspec = lambda shape: pl.BlockSpec((1,) + shape, lambda g: (g, 0, 0))
    return pl.pallas_call(
        functools.partial(_s5t_kernel, R=R, B=B, chunk_major=chunk_major),
        grid=(G,),
        in_specs=[gspec((K, BR)), gspec((K, 1)), gspec((K, K)), gspec((K, K)),
                  gspec((K, P2)), gspec((K, P2)), gspec((1, P2)), gspec((1, P2)),
                  gspec((B, P2)), gspec((B, P2))],
        out_specs=[gspec((K, BR)), gspec((B, P2)), gspec((B, P2))],
        out_shape=[jax.ShapeDtypeStruct((G, K, BR), F32),
                   jax.ShapeDtypeStruct((G, B, P2), F32),
                   jax.ShapeDtypeStruct((G, B, P2), F32)],
        scratch_shapes=[pltpu.VMEM((K, BR), F32)] + [pltpu.VMEM((BR, P2), F32) for _ in range(6)],
        compiler_params=_cparams("parallel"),
        name="s5_core_t",
    )(xt, dcol, mt, wt, vret, vimt, are, aim, s0re, s0im)


def _s5rows_kernel(yt_ref, o_ref):
    I = S5_GROUP
    nc = yt_ref.shape[-1]
    for t in range(S5_SUB):
        blk = yt_ref[:, t * I:(t + 1) * I, :]
        o_ref[pl.ds(t, nc, stride=S5_SUB), :] = blk.reshape(LANES, nc).T


def _s5_rows(yt, B, T, D):
    G = D // S5_GROUP
    BR = B * T // S5_SUB
    gpt = LANES // S5_GROUP
    nc = next(n for n in (4 * LANES, 2 * LANES, LANES) if BR % n == 0)
    return pl.pallas_call(
        _s5rows_kernel,
        grid=(BR // nc, G // gpt),
        in_specs=[pl.BlockSpec((gpt, S5_SUB * S5_GROUP, nc), lambda i, j: (j, 0, i))],
        out_specs=pl.BlockSpec((nc * S5_SUB, LANES), lambda i, j: (i, j)),
        out_shape=jax.ShapeDtypeStruct((B * T, D), F32),
        compiler_params=_cparams("parallel", "parallel"),
        name="s5_rows",
    )(yt)


def _lru_kernel(*refs, Tt, CB, reverse, has_init):
    refs = list(refs)
    lx_ref, prev_ref, next_ref, cw_ref, cb_ref, wg_ref, bgate_ref, lam_ref = refs[:8]
    pos = 8
    h0_ref = None
    if has_init:
        h0_ref = refs[pos]
        pos += 1
    if reverse:
        hf_ref, ly_ref = refs[pos:pos + 2]
        pos += 2
    out_ref, fin_ref, x_scr, a_scr, b_scr, h_scr, carry_ref = refs[pos:pos + 7]
    NBk = CB // LRU_BLOCK
    k = pl.program_id(2)
    nk = pl.num_programs(2)
    first_tile = k == 0

    @pl.when(first_tile)
    def _():
        if has_init:
            carry_ref[...] = h0_ref[...]
        else:
            carry_ref[...] = jnp.zeros((SUBLANES, CB), F32)

    tpos = (nk - 1 - k) if reverse else k
    has_prev = tpos > 0
    has_next = tpos < nk - 1
    cw = cw_ref[0]
    cbias = cb_ref[0]
    bgate = bgate_ref[0]
    lam = lam_ref[0]
    half_cfac = (-0.5 * LRU_C) * (jnp.maximum(-lam, 0.0) + jnp.log1p(jnp.exp(-jnp.abs(lam))))

    left = LRU_CONV // 2
    for b in range(SUBLANES):
        cur = lx_ref[b]
        pv = jnp.where(has_prev, prev_ref[b], 0.0)
        nx = jnp.where(has_next, next_ref[b], 0.0)
        for c in range(NBk):
            sl = slice(c * LRU_BLOCK, (c + 1) * LRU_BLOCK)
            x_scr[c, pl.ds(left * SUBLANES + b, Tt, stride=SUBLANES), :] = cur[:, sl]
            for q in range(left):
                x_scr[c, pl.ds(q * SUBLANES + b, 1), :] = pv[SUBLANES - left + q:SUBLANES - left + q + 1, sl]
            for q in range(LRU_CONV - 1 - left):
                x_scr[c, pl.ds((Tt + left + q) * SUBLANES + b, 1), :] = nx[q:q + 1, sl]

    RC = min(Tt * SUBLANES, 512)
    for c in range(NBk):
        sl = slice(c * LRU_BLOCK, (c + 1) * LRU_BLOCK)
        bias_c = jnp.concatenate([bgate[:, sl], bgate[:, CB + c * LRU_BLOCK: CB + (c + 1) * LRU_BLOCK]], axis=1)
        for r0 in range(0, Tt * SUBLANES, RC):
            xc = cbias[:, sl] + sum(cw[j:j + 1, sl] * x_scr[c, pl.ds(r0 + j * SUBLANES, RC), :]
                                    for j in range(LRU_CONV))
            th = jnp.tanh(_dot(xc, wg_ref[0, c]) + bias_c)
            log_a = half_cfac[:, sl] * th[:, :LRU_BLOCK] + half_cfac[:, sl]
            a = jnp.exp(log_a)
            om = -jnp.tanh(log_a) * (1.0 + a * a)
            root = om * lax.rsqrt(jnp.maximum(om, F32_TINY))
            bt = root * ((0.5 * th[:, LRU_BLOCK:] + 0.5) * xc)
            a_scr[c, pl.ds(r0, RC), :] = a
            b_scr[c, pl.ds(r0, RC), :] = bt

    def step(s, h):
        t = (Tt - 1 - s) if reverse else s
        rows = pl.ds(pl.multiple_of(t * SUBLANES, SUBLANES), SUBLANES)
        new = []
        for c in range(NBk):
            hc = a_scr[c, rows, :] * h[c] + b_scr[c, rows, :]
            h_scr[c, rows, :] = hc
            new.append(hc)
        return tuple(new)

    h_init = tuple(carry_ref[:, c * LRU_BLOCK:(c + 1) * LRU_BLOCK] for c in range(NBk))
    h_last = lax.fori_loop(0, Tt, step, h_init, unroll=SCAN_UNROLL)
    for c in range(NBk):
        carry_ref[:, c * LRU_BLOCK:(c + 1) * LRU_BLOCK] = h_last[c]
        fin_ref[:, c * LRU_BLOCK:(c + 1) * LRU_BLOCK] = h_last[c]

    for b in range(SUBLANES):
        hb = jnp.concatenate([h_scr[c, pl.ds(b, Tt, stride=SUBLANES), :] for c in range(NBk)], axis=1)
        if reverse:
            out_ref[b] = (hf_ref[b] + hb) * jax.nn.gelu(ly_ref[b])
        else:
            out_ref[b] = hb


def _lru_sweep(P3, lw, h0, hf, reverse, B, T, D):
    W = 3 * D // 2
    CB = 2 * LRU_BLOCK
    Tt = min(T, 256)
    nk = T // Tt
    nb8 = Tt // SUBLANES
    lxcol = (D // 2 + D // 2 + D + D) // CB
    lycol = lxcol + W // CB
    d = 1 if reverse else 0
    has_init = h0 is not None

    def tmap(k):
        return (nk - 1 - k) if reverse else k

    in_specs = [
        pl.BlockSpec((SUBLANES, Tt, CB), lambda g, j, k: (g, tmap(k), lxcol + j)),
        pl.BlockSpec((SUBLANES, SUBLANES, CB),
                     lambda g, j, k: (g, jnp.maximum(tmap(k) * nb8 - 1, 0), lxcol + j)),
        pl.BlockSpec((SUBLANES, SUBLANES, CB),
                     lambda g, j, k: (g, jnp.minimum((tmap(k) + 1) * nb8, T // SUBLANES - 1), lxcol + j)),
        pl.BlockSpec((1, LRU_CONV, CB), lambda g, j, k: (j, 0, 0)),
        pl.BlockSpec((1, 1, CB), lambda g, j, k: (j, 0, 0)),
        pl.BlockSpec((1, CB // LRU_BLOCK, LRU_BLOCK, 2 * LRU_BLOCK), lambda g, j, k: (j, 0, 0, 0)),
        pl.BlockSpec((1, 1, 2 * CB), lambda g, j, k: (j, 0, 0)),
        pl.BlockSpec((1, 1, CB), lambda g, j, k: (j, 0, 0)),
    ]
    args = [P3, P3, P3, lw["conv_w"], lw["conv_b"], lw["wg"][d], lw["bg"][d], lw["lam"][d]]
    if has_init:
        in_specs.append(pl.BlockSpec((SUBLANES, CB), lambda g, j, k: (g, j)))
        args.append(h0[:, d])
    if reverse:
        in_specs.append(pl.BlockSpec((SUBLANES, Tt, CB), lambda g, j, k: (g, tmap(k), j)))
        in_specs.append(pl.BlockSpec((SUBLANES, Tt, CB), lambda g, j, k: (g, tmap(k), lycol + j)))
        args += [hf, P3]
    out, fin = pl.pallas_call(
        functools.partial(_lru_kernel, Tt=Tt, CB=CB, reverse=reverse, has_init=has_init),
        grid=(B // SUBLANES, W // CB, nk),
        in_specs=in_specs,
        out_specs=[pl.BlockSpec((SUBLANES, Tt, CB), lambda g, j, k: (g, tmap(k), j)),
                   pl.BlockSpec((SUBLANES, CB), lambda g, j, k: (g, j))],
        out_shape=[jax.ShapeDtypeStruct((B, T, W), F32), jax.ShapeDtypeStruct((B, W), F32)],
        scratch_shapes=[pltpu.VMEM((CB // LRU_BLOCK, (Tt + LRU_CONV) * SUBLANES, LRU_BLOCK), F32)]
        + [pltpu.VMEM((CB // LRU_BLOCK, Tt * SUBLANES, LRU_BLOCK), F32) for _ in range(3)]
        + [pltpu.VMEM((SUBLANES, CB), F32)],
        compiler_params=_cparams("parallel", "parallel", "arbitrary"),
        name="lru_bwd" if reverse else "lru_fwd",
    )(*args)
    return out, fin


def _merge_kernel(x_ref, mod_ref, o_ref, r_ref, ys_ref, yl_ref, gg_ref, gs_ref, gl_ref,
                  gn_ref, wglu_ref, wbg_ref, wbs_ref, wbl_ref, wo_ref, lg_ref, lb_ref,
                  out_ref, *, D, alpha):
    DV = D // GLA_HEADS
    gn = gn_ref[...]
    parts = []
    for h in range(GLA_HEADS):
        sl = slice(h * DV, (h + 1) * DV)
        o = o_ref[:, sl]
        mu = jnp.mean(o, axis=-1, keepdims=True)
        oc = o - mu
        var = jnp.mean(oc * oc, axis=-1, keepdims=True)
        parts.append(oc * lax.rsqrt(var + LN_EPS) * gn[:, sl] * _silu(r_ref[:, sl]))
    y_gla = jnp.concatenate(parts, axis=1)
    ys = jax.nn.gelu(ys_ref[...])
    y_s5 = ys * _sigmoid(_dot(ys, wglu_ref[...]))
    merged = (_sigmoid(gg_ref[...]) * _dot(y_gla, wbg_ref[...])
              + _sigmoid(gs_ref[...]) * _dot(y_s5, wbs_ref[...])
              + _sigmoid(gl_ref[...]) * _dot(yl_ref[...], wbl_ref[...]))
    mix = _dot(merged, wo_ref[...])
    gate1 = mod_ref[0, :, 2 * D:3 * D]
    out_ref[...] = _layer_norm(alpha * x_ref[...] + gate1 * mix, lg_ref[...], lb_ref[...])


def _merge(x2, mod, o_gla, P, ys, ylru, lw, T, alpha):
    N, D = x2.shape
    W = 3 * D // 2
    Bm = mod.shape[0]
    rcol, gcol = 2, 6
    tm = _row_tile(N, T, Bm > 1, 256)
    per_b = T // tm
    mod_map = (lambda i: (i // per_b, 0, 0)) if Bm > 1 else (lambda i: (0, 0, 0))
    row = lambda width, cb=0: pl.BlockSpec((tm, width), lambda i: (i, cb))
    full = lambda a: pl.BlockSpec(a.shape, lambda i: (0,) * a.ndim)
    weights = [lw["gnorm"], lw["w_glu"], lw["w_br_gla"], lw["w_br_s5"], lw["w_br_lru"],
               lw["w_out"], lw["ln1_g"], lw["ln1_b"]]
    return pl.pallas_call(
        functools.partial(_merge_kernel, D=D, alpha=alpha),
        grid=(N // tm,),
        in_specs=[row(D), pl.BlockSpec((1, 1, 6 * D), mod_map), row(D), row(D, rcol), row(D),
                  row(W), row(D, gcol), row(D, gcol + 1), row(D, gcol + 2)]
        + [full(w) for w in weights],
        out_specs=row(D),
        out_shape=jax.ShapeDtypeStruct((N, D), F32),
        compiler_params=_cparams("parallel"),
        name="merge",
    )(x2, mod, o_gla, P, ys, ylru, P, P, P, *weights)


def _mlp_kernel(x_ref, mod_ref, w1_ref, w2_ref, lg_ref, lb_ref, out_ref, h_ref, acc_ref, *, D, alpha):
    j = pl.program_id(1)

    @pl.when(j == 0)
    def _():
        shift = mod_ref[0, :, 3 * D:4 * D]
        scale = mod_ref[0, :, 4 * D:5 * D]
        h_ref[...] = (x_ref[...] * (1.0 + scale) + shift).astype(BF16)
        acc_ref[...] = jnp.zeros_like(acc_ref)

    hid = jnp.dot(h_ref[...], w1_ref[...], preferred_element_type=F32)
    hid = jnp.square(jnp.maximum(hid, 0.0))
    acc_ref[...] += _dot(hid, w2_ref[...])

    @pl.when(j == pl.num_programs(1) - 1)
    def _():
        gate2 = mod_ref[0, :, 5 * D:6 * D]
        out_ref[...] = _layer_norm(alpha * x_ref[...] + gate2 * acc_ref[...], lg_ref[...], lb_ref[...])


def _mlp(x2, mod, lw, T, alpha):
    N, D = x2.shape
    HID = lw["w_mlp_in"].shape[1]
    Bm = mod.shape[0]
    tm = _row_tile(N, T, Bm > 1, 1024)
    th = min(HID, 1024)
    per_b = T // tm
    mod_map = (lambda i, j: (i // per_b, 0, 0)) if Bm > 1 else (lambda i, j: (0, 0, 0))
    return pl.pallas_call(
        functools.partial(_mlp_kernel, D=D, alpha=alpha),
        grid=(N // tm, HID // th),
        in_specs=[pl.BlockSpec((tm, D), lambda i, j: (i, 0)),
                  pl.BlockSpec((1, 1, 6 * D), mod_map),
                  pl.BlockSpec((D, th), lambda i, j: (0, j)),
                  pl.BlockSpec((th, D), lambda i, j: (j, 0)),
                  pl.BlockSpec((1, D), lambda i, j: (0, 0)),
                  pl.BlockSpec((1, D), lambda i, j: (0, 0))],
        out_specs=pl.BlockSpec((tm, D), lambda i, j: (i, 0)),
        out_shape=jax.ShapeDtypeStruct((N, D), F32),
        scratch_shapes=[pltpu.VMEM((tm, D), BF16), pltpu.VMEM((tm, D), F32)],
        compiler_params=_cparams("parallel", "arbitrary"),
        name="mlp",
    )(x2, mod, lw["w_mlp_in"], lw["w_mlp_out"], lw["ln2_g"], lw["ln2_b"])


def _pack_layer_weights(l, D, w_in, gla_w_gate, gla_b_gate, gla_norm_g, s5_d, s5_w_glu,
                        lru_conv_w, lru_conv_b, lru_w_a, lru_b_a, lru_w_i, lru_b_i, lru_lam,
                        w_br_gla, w_br_s5, w_br_lru, w_out, ln1_g, ln1_b, ln2_g, ln2_b,
                        w_mlp_in, w_mlp_out):
    H = GLA_HEADS
    KEY = D // 2
    DK = KEY // H
    W = 3 * D // 2
    CB = 2 * LRU_BLOCK
    NB = W // LRU_BLOCK
    widths = (KEY, KEY, D, D, 2 * GATE_RANK, D, W, W, 3 * D)
    offs = [0]
    for wd in widths:
        offs.append(offs[-1] + wd)
    wl = w_in[l]
    piece = lambda i: wl[:, offs[i]:offs[i + 1]]
    w_pack = jnp.concatenate([piece(0), piece(1), piece(2), piece(3), piece(6), piece(7),
                              piece(8)], axis=1).astype(BF16)
    w_glr = jnp.concatenate([piece(4), jnp.zeros((D, LANES - 2 * GATE_RANK), wl.dtype)], axis=1).astype(BF16)
    w_ut = piece(5).T.astype(BF16)
    wgate = gla_w_gate[l]
    wg = jnp.zeros((H, LANES, 2 * DK), F32)
    wg = wg.at[:, 0:GATE_RANK, 0:DK].set(wgate[0].reshape(GATE_RANK, H, DK).transpose(1, 0, 2))
    wg = wg.at[:, GATE_RANK:2 * GATE_RANK, DK:].set(wgate[1].reshape(GATE_RANK, H, DK).transpose(1, 0, 2))
    bgate = gla_b_gate[l].reshape(2, H, DK).transpose(1, 0, 2).reshape(H, 1, 2 * DK)
    lru_wg = 0.5 * jnp.concatenate([lru_w_a[l], lru_w_i[l]], axis=-1)
    lru_wg = lru_wg.reshape(2, W // CB, CB // LRU_BLOCK, LRU_BLOCK, 2 * LRU_BLOCK).astype(BF16)
    lru_bg = 0.5 * jnp.concatenate([lru_b_a[l].reshape(2, W // CB, 1, CB), lru_b_i[l].reshape(2, W // CB, 1, CB)],
                                   axis=-1)
    return {
        "w_pack": w_pack, "w_glr": w_glr, "w_ut": w_ut, "gla_wg": wg.astype(BF16), "gla_bg": bgate,
        "lru": {"conv_w": lru_conv_w[l].reshape(LRU_CONV, W // CB, CB).transpose(1, 0, 2),
                "conv_b": lru_conv_b[l].reshape(W // CB, 1, CB),
                "wg": lru_wg, "bg": lru_bg, "lam": lru_lam[l].reshape(2, W // CB, 1, CB)},
        "gnorm": gla_norm_g[l].reshape(1, D), "s5_dcol": jnp.tile(s5_d[l].reshape(D // S5_GROUP, 1, S5_GROUP), (1, S5_SUB, 1)).reshape(
            D // S5_GROUP, S5_SUB * S5_GROUP, 1),
        "w_glu": s5_w_glu[l].astype(BF16), "w_br_gla": w_br_gla[l].astype(BF16),
        "w_br_s5": w_br_s5[l].astype(BF16), "w_br_lru": w_br_lru[l].astype(BF16),
        "w_out": w_out[l].astype(BF16),
        "ln1_g": ln1_g[l].reshape(1, D), "ln1_b": ln1_b[l].reshape(1, D),
        "ln2_g": ln2_g[l].reshape(1, D), "ln2_b": ln2_b[l].reshape(1, D),
        "w_mlp_in": w_mlp_in[l].astype(BF16), "w_mlp_out": w_mlp_out[l].astype(BF16),
    }


def _trunk_layer(x2, mod, lw, s5ops, init, B, T, D, alpha):
    N = B * T
    G = D // S5_GROUP
    S = S5_SUB
    R = T // S
    Pn = S5_STATE
    P, glr = _in_proj(x2, mod, lw["w_pack"], lw["w_glr"], T)
    gla0 = s5re0 = s5im0 = lru0 = None
    if init is not None:
        gla0, s5re0, s5im0, lru0 = init
        s5re0 = s5re0.transpose(2, 0, 1, 3).reshape(G, B, 2 * Pn)
        s5im0 = s5im0.transpose(2, 0, 1, 3).reshape(G, B, 2 * Pn)
    o_gla, gla_fin = _gla_core(P, glr, lw["gla_wg"], lw["gla_bg"], gla0, B, T, D)
    if s5re0 is None:
        s5re0 = jnp.zeros((G, B, 2 * Pn), F32)
        s5im0 = jnp.zeros((G, B, 2 * Pn), F32)
    xt = _s5_proj_t(x2, mod, lw["w_ut"], B, T)
    yt, fre, fim = _s5_core_t(xt, lw["s5_dcol"], s5ops, s5re0, s5im0, R, B)
    ys = _s5_rows(yt, B, T, D)
    s5re_fin = fre.reshape(G, B, 2, Pn).transpose(1, 2, 0, 3)
    s5im_fin = fim.reshape(G, B, 2, Pn).transpose(1, 2, 0, 3)
    P3 = P.reshape(B, T, P.shape[1])
    hf, lf = _lru_sweep(P3, lw["lru"], lru0, None, False, B, T, D)
    ylru, lb = _lru_sweep(P3, lw["lru"], lru0, hf, True, B, T, D)
    lru_fin = jnp.stack([lf, lb], axis=1)
    x1 = _merge(x2, mod, o_gla, P, ys, ylru.reshape(N, -1), lw, T, alpha)
    x3 = _mlp(x1, mod, lw, T, alpha)
    return x3, (gla_fin, s5re_fin, s5im_fin, lru_fin)


def kernel(x_prompt, x_sample, state_gla, state_s5_re, state_s5_im, state_lru, c, c_ctx, w_ada, b_ada, w_in, gla_w_gate, gla_b_gate, gla_norm_g, s5_lam_re, s5_lam_im, s5_log_step, s5_b_re, s5_b_im, s5_c_re, s5_c_im, s5_d, s5_w_glu, lru_conv_w, lru_conv_b, lru_w_a, lru_b_a, lru_w_i, lru_b_i, lru_lam, w_br_gla, w_br_s5, w_br_lru, w_out, ln1_g, ln1_b, ln2_g, ln2_b, w_mlp_in, w_mlp_out):
    Bp, Tp, D = x_prompt.shape
    Bs, Ts, _ = x_sample.shape
    L = w_in.shape[0]
    alpha = (2.0 * L) ** 0.25

    n_rows = -(-(Bs + 1) // SUBLANES) * SUBLANES
    cc = jnp.concatenate([c, c_ctx[None], jnp.zeros((n_rows - Bs - 1, D), F32)], axis=0)
    mod = _ada_mod(cc, w_ada, b_ada)
    s5ops_all = _s5_operators(s5_lam_re, s5_lam_im, s5_log_step, s5_b_re, s5_b_im, s5_c_re, s5_c_im)

    xp = x_prompt.reshape(Bp * Tp, D)
    xs = _add_pos(x_sample, _grid_pos_table(Ts, D)).reshape(Bs * Ts, D)
    fins = []
    for l in range(L):
        lw = _pack_layer_weights(l, D, w_in, gla_w_gate, gla_b_gate, gla_norm_g, s5_d, s5_w_glu,
                                 lru_conv_w, lru_conv_b, lru_w_a, lru_b_a, lru_w_i, lru_b_i, lru_lam,
                                 w_br_gla, w_br_s5, w_br_lru, w_out, ln1_g, ln1_b, ln2_g, ln2_b,
                                 w_mlp_in, w_mlp_out)
        s5ops = tuple(a[l] for a in s5ops_all)
        mod_ctx = mod[l, Bs:Bs + 1].reshape(1, 1, 6 * D)
        mod_lat = mod[l, :Bs].reshape(Bs, 1, 6 * D)
        xp, fin = _trunk_layer(xp, mod_ctx, lw, s5ops, None, Bp, Tp, D, alpha)
        fins.append(fin)
        cache = (state_gla[:, l], state_s5_re[:, l], state_s5_im[:, l], state_lru[:, l])
        xs, _ = _trunk_layer(xs, mod_lat, lw, s5ops, cache, Bs, Ts, D, alpha)
    sdt = x_prompt.dtype
    new_states = tuple(jnp.stack([f[i] for f in fins], axis=1).astype(sdt) for i in range(4))
    return (xp.reshape(Bp, Tp, D), xs.reshape(Bs, Ts, D)) + new_states
```

```python
import functools
import math

import jax
import jax.numpy as jnp
from jax import lax
from jax.experimental import pallas as pl
from jax.experimental.pallas import tpu as pltpu

F32 = jnp.float32
BF16 = jnp.bfloat16
HIGHEST = lax.Precision.HIGHEST

LANES = 128
SUBLANES = 8
VMEM_LIMIT = 56 * 1024 * 1024

GRID_W = 64
GLA_HEADS = 4
GATE_RANK = 16
GLA_TAU = 16.0
GLA_CHUNK = 64
GLA_BLOCK = 256
S5_GROUP = 16
S5_STATE = 64
S5_SUB = 16
S5_PROJ_ROWS = 1024
LRU_BLOCK = 128
LRU_CONV = 4
LRU_C = 8.0
LN_EPS = 1e-5
F32_TINY = 1.1754944e-38
SCAN_UNROLL = 8


def _cparams(*sem):
    return pltpu.CompilerParams(dimension_semantics=sem, vmem_limit_bytes=VMEM_LIMIT)


def _dot(a, b):
    return jnp.dot(a.astype(BF16), b.astype(BF16), preferred_element_type=F32)


def _dot_nt(a, b):
    return lax.dot_general(a.astype(BF16), b.astype(BF16), (((1,), (1,)), ((), ())),
                           preferred_element_type=F32)


def _dot_tn(a, b):
    return lax.dot_general(a.astype(BF16), b.astype(BF16), (((0,), (0,)), ((), ())),
                           preferred_element_type=F32)


def _dot_f32(a, b):
    return jnp.dot(a, b, precision=HIGHEST, preferred_element_type=F32)


def _layer_norm(z, g, b):
    mu = jnp.mean(z, axis=-1, keepdims=True)
    zc = z - mu
    var = jnp.mean(zc * zc, axis=-1, keepdims=True)
    return zc * lax.rsqrt(var + LN_EPS) * g + b


def _sigmoid(x):
    return 0.5 * jnp.tanh(0.5 * x) + 0.5


def _silu(x):
    return x * _sigmoid(x)


def _ada_kernel(cc_ref, w_ref, b_ref, o_ref):
    s = _silu(cc_ref[...])
    o_ref[0] = _dot_f32(s, w_ref[0]) + b_ref[0]


def _ada_mod(cc, w_ada, b_ada):
    L, D, D6 = w_ada.shape
    R = cc.shape[0]
    tn = D6 // 4
    return pl.pallas_call(
        _ada_kernel,
        grid=(L, D6 // tn),
        in_specs=[pl.BlockSpec((R, D), lambda l, j: (0, 0)),
                  pl.BlockSpec((1, D, tn), lambda l, j: (l, 0, j)),
                  pl.BlockSpec((1, 1, tn), lambda l, j: (l, 0, j))],
        out_specs=pl.BlockSpec((1, R, tn), lambda l, j: (l, 0, j)),
        out_shape=jax.ShapeDtypeStruct((L, R, D6), F32),
        compiler_params=_cparams("parallel", "parallel"),
        name="ada_mod",
    )(cc, w_ada, b_ada.reshape(L, 1, D6))


def _addpos_kernel(x_ref, p_ref, o_ref):
    o_ref[0] = x_ref[0] + p_ref[...]


def _add_pos(x, pos):
    B, T, D = x.shape
    tt = min(T, 512)
    return pl.pallas_call(
        _addpos_kernel,
        grid=(T // tt, B),
        in_specs=[pl.BlockSpec((1, tt, D), lambda t, b: (b, t, 0)),
                  pl.BlockSpec((tt, D), lambda t, b: (t, 0))],
        out_specs=pl.BlockSpec((1, tt, D), lambda t, b: (b, t, 0)),
        out_shape=jax.ShapeDtypeStruct((B, T, D), F32),
        compiler_params=_cparams("parallel", "parallel"),
        name="add_pos",
    )(x, pos)


def _grid_pos_table(n_tokens, dim):
    rows = n_tokens // GRID_W
    quarter = dim // 4
    omega = 1.0 / (10000.0 ** (jnp.arange(quarter, dtype=F32) / quarter))
    r = jnp.arange(rows, dtype=F32)[:, None, None] * omega
    cl = jnp.arange(GRID_W, dtype=F32)[None, :, None] * omega
    shape = (rows, GRID_W, quarter)
    emb = jnp.concatenate([jnp.broadcast_to(jnp.sin(r), shape), jnp.broadcast_to(jnp.cos(r), shape),
                           jnp.broadcast_to(jnp.sin(cl), shape), jnp.broadcast_to(jnp.cos(cl), shape)], axis=-1)
    return emb.reshape(rows * GRID_W, dim)


def _inproj_kernel(x_ref, mod_ref, w_ref, wglr_ref, o_ref, glr_ref, h_ref, *, D):
    @pl.when(pl.program_id(1) == 0)
    def _():
        shift = mod_ref[0, :, 0:D]
        scale = mod_ref[0, :, D:2 * D]
        h = (x_ref[...] * (1.0 + scale) + shift).astype(BF16)
        h_ref[...] = h
        glr_ref[...] = jnp.dot(h, wglr_ref[...], preferred_element_type=F32)

    o_ref[...] = jnp.dot(h_ref[...], w_ref[...], preferred_element_type=F32)


def _row_tile(N, T, per_batch_mod, want):
    tm = min(want, T if per_batch_mod else N)
    while N % tm or (per_batch_mod and T % tm):
        tm //= 2
    return tm


def _in_proj(x2, mod, w_pack, w_glr, T):
    N, D = x2.shape
    NC = w_pack.shape[1]
    Bm = mod.shape[0]
    tm = _row_tile(N, T, Bm > 1, 1024)
    tn = D
    per_b = T // tm
    mod_map = (lambda i, j: (i // per_b, 0, 0)) if Bm > 1 else (lambda i, j: (0, 0, 0))
    return pl.pallas_call(
        functools.partial(_inproj_kernel, D=D),
        grid=(N // tm, NC // tn),
        in_specs=[pl.BlockSpec((tm, D), lambda i, j: (i, 0)),
                  pl.BlockSpec((1, 1, 6 * D), mod_map),
                  pl.BlockSpec((D, tn), lambda i, j: (0, j)),
                  pl.BlockSpec((D, LANES), lambda i, j: (0, 0))],
        out_specs=[pl.BlockSpec((tm, tn), lambda i, j: (i, j)),
                   pl.BlockSpec((tm, LANES), lambda i, j: (i, 0))],
        out_shape=[jax.ShapeDtypeStruct((N, NC), F32), jax.ShapeDtypeStruct((N, LANES), F32)],
        scratch_shapes=[pltpu.VMEM((tm, D), BF16)],
        compiler_params=_cparams("parallel", "arbitrary"),
        name="in_proj",
    )(x2, mod, w_pack, w_glr)


def _gla_kernel(*refs, T, DK, DV, has_init):
    if has_init:
        (q_ref, k_ref, v_ref, glr_ref, wg_ref, bg_ref, s0_ref,
         o_ref, sfin_ref, qb_ref, keb_ref, db_ref, sf_ref, sb_ref) = refs
    else:
        (q_ref, k_ref, v_ref, glr_ref, wg_ref, bg_ref,
         o_ref, sfin_ref, qb_ref, keb_ref, db_ref, sf_ref, sb_ref) = refs
    C = GLA_CHUNK
    BLK = GLA_BLOCK
    NS = BLK // C
    nblk = T // BLK
    qscale = DK ** -0.5
    shift = C.bit_length() - 1
    row = lax.broadcasted_iota(jnp.int32, (BLK, BLK), 0)
    col = lax.broadcasted_iota(jnp.int32, (BLK, BLK), 1)
    same = lax.shift_right_logical(row, shift) == lax.shift_right_logical(col, shift)
    lower = jnp.logical_and(same, row >= col)
    upper = jnp.logical_and(same, row <= col)
    tri = jnp.where(lower, 1.0, 0.0).astype(BF16)
    chunk_of_row = lax.shift_right_logical(lax.broadcasted_iota(jnp.int32, (BLK, DK), 0), shift)
    wg = wg_ref[0]
    bg = bg_ref[0]

    def chunk_cols(ke):
        return jnp.concatenate([jnp.where(chunk_of_row == c, ke, 0.0) for c in range(NS)], axis=1).astype(BF16)

    def per_chunk_last(x, r):
        return jnp.concatenate([jnp.broadcast_to(x[c * C + r:c * C + r + 1, :], (C, x.shape[1]))
                                for c in range(NS)], axis=0)

    if has_init:
        sf_ref[...] = s0_ref[0, 0, 0].T
        sb_ref[...] = s0_ref[0, 1, 0].T
    else:
        sf_ref[...] = jnp.zeros((DV, DK), F32)
        sb_ref[...] = jnp.zeros((DV, DK), F32)

    def fwd_body(i, carry):
        r0 = pl.multiple_of(i * BLK, BLK)
        rows = pl.ds(r0, BLK)
        qc = q_ref[rows, :] * qscale
        kc = k_ref[rows, :]
        vc = v_ref[rows, :]
        logits = _dot(glr_ref[rows, :], wg) + bg
        la = (jnp.minimum(logits, 0.0) - jnp.log(1.0 + jnp.exp(-jnp.abs(logits)))) * (1.0 / GLA_TAU)
        p1 = la.astype(BF16)
        p2 = (la - p1.astype(F32)).astype(BF16)
        pre = jnp.dot(tri, p1, preferred_element_type=F32) + jnp.dot(tri, p2, preferred_element_type=F32)
        tot = per_chunk_last(pre, C - 1)
        cum_f = pre[:, :DK]
        last_f = tot[:, :DK]
        cum_b = tot[:, DK:] - pre[:, DK:] + la[:, DK:]
        last_b = tot[:, DK:]
        q_f = qc * jnp.exp(cum_f)
        k_f = kc * jnp.exp(-cum_f)
        ke_f = kc * jnp.exp(last_f - cum_f)
        q_b = qc * jnp.exp(cum_b)
        k_b = kc * jnp.exp(-cum_b)
        ke_b = kc * jnp.exp(last_b - cum_b)
        sc = jnp.where(lower, _dot_nt(q_f, k_f), 0.0) + jnp.where(upper, _dot_nt(q_b, k_b), 0.0)
        o_blk = _dot(sc, vc)
        u_f = jnp.dot(vc.T.astype(BF16), chunk_cols(ke_f), preferred_element_type=F32)
        d_f = jnp.exp(last_f)
        s = sf_ref[...]
        for c in range(NS):
            rc = slice(c * C, (c + 1) * C)
            o_ref[pl.ds(r0 + c * C, C), :] = o_blk[rc] + _dot_nt(q_f[rc], s)
            s = s * d_f[c * C:c * C + 1, :] + u_f[:, c * DK:(c + 1) * DK]
        sf_ref[...] = s
        qb_ref[rows, :] = q_b
        keb_ref[rows, :] = ke_b
        d_b = jnp.exp(last_b)
        db_ref[pl.ds(i, 1), :] = jnp.concatenate([d_b[c * C:c * C + 1, :] for c in range(NS)], axis=1)
        return carry

    lax.fori_loop(0, nblk, fwd_body, 0, unroll=min(8, nblk))

    def bwd_body(j, carry):
        i = nblk - 1 - j
        r0 = pl.multiple_of(i * BLK, BLK)
        rows = pl.ds(r0, BLK)
        q_b = qb_ref[rows, :]
        u_b = jnp.dot(v_ref[rows, :].T.astype(BF16), chunk_cols(keb_ref[rows, :]), preferred_element_type=F32)
        d_b = db_ref[pl.ds(i, 1), :]
        s = sb_ref[...]
        for c in reversed(range(NS)):
            rc = pl.ds(r0 + c * C, C)
            o_ref[rc, :] = o_ref[rc, :] + _dot_nt(q_b[c * C:(c + 1) * C], s)
            s = s * d_b[:, c * DK:(c + 1) * DK] + u_b[:, c * DK:(c + 1) * DK]
        sb_ref[...] = s
        return carry

    lax.fori_loop(0, nblk, bwd_body, 0, unroll=min(8, nblk))

    sfin_ref[0, 0, 0] = sf_ref[...].T
    sfin_ref[0, 1, 0] = sb_ref[...].T


def _gla_core(P, glr, wg, bg, s0, B, T, D):
    H = GLA_HEADS
    DK = D // 2 // H
    DV = D // H
    N = B * T
    has_init = s0 is not None
    kcol = (D // 2) // DK
    vcol = D // DV
    in_specs = [pl.BlockSpec((T, DK), lambda b, h: (b, h)),
                pl.BlockSpec((T, DK), lambda b, h: (b, kcol + h)),
                pl.BlockSpec((T, DV), lambda b, h: (b, vcol + h)),
                pl.BlockSpec((T, LANES), lambda b, h: (b, 0)),
                pl.BlockSpec((1, LANES, 2 * DK), lambda b, h: (h, 0, 0)),
                pl.BlockSpec((1, 1, 2 * DK), lambda b, h: (h, 0, 0))]
    args = [P, P, P, glr, wg, bg]
    if has_init:
        in_specs.append(pl.BlockSpec((1, 2, 1, DK, DV), lambda b, h: (b, 0, h, 0, 0)))
        args.append(s0)
    o, sfin = pl.pallas_call(
        functools.partial(_gla_kernel, T=T, DK=DK, DV=DV, has_init=has_init),
        grid=(B, H),
        in_specs=in_specs,
        out_specs=[pl.BlockSpec((T, DV), lambda b, h: (b, h)),
                   pl.BlockSpec((1, 2, 1, DK, DV), lambda b, h: (b, 0, h, 0, 0))],
        out_shape=[jax.ShapeDtypeStruct((N, D), F32),
                   jax.ShapeDtypeStruct((B, 2, H, DK, DV), F32)],
        scratch_shapes=[pltpu.VMEM((T, DK), F32), pltpu.VMEM((T, DK), F32),
                        pltpu.VMEM((T // GLA_BLOCK, (GLA_BLOCK // GLA_CHUNK) * DK), F32),
                        pltpu.VMEM((DV, DK), F32), pltpu.VMEM((DV, DK), F32)],
        compiler_params=_cparams("parallel", "parallel"),
        name="gla_core",
    )(*args)
    return o, sfin


def _s5prep_kernel(lr_ref, li_ref, ls_ref, btr_ref, bti_ref, cr_ref, ci_ref,
                   kt_ref, k0_ref, wre_ref, wim_ref, cpre_ref, cpimn_ref, pre_ref, pim_ref):
    I = S5_GROUP
    btr = btr_ref[0]
    bti = bti_ref[0]
    c_re = cr_ref[0]
    c_im = ci_ref[0]
    batched_nt = (((2,), (2,)), ((0,), (0,)))
    for d in range(2):
        lr = lr_ref[0, d]
        li = li_ref[0, d]
        step = jnp.exp(ls_ref[0, d])
        mag = jnp.exp(lr * step)
        ang = li * step
        a_re = mag * jnp.cos(ang)
        a_im = mag * jnp.sin(ang)
        den = lr * lr + li * li
        nr = a_re - 1.0
        f_re = (nr * lr + a_im * li) / den
        f_im = (a_im * lr - nr * li) / den
        bb_re = f_re[:, None, :] * btr - f_im[:, None, :] * bti
        bb_im = f_re[:, None, :] * bti + f_im[:, None, :] * btr
        p_re = jnp.ones_like(lr)
        p_im = jnp.zeros_like(lr)
        for e in range(S5_SUB + 1):
            pr = p_re[:, None, :]
            pi = p_im[:, None, :]
            cp_re = c_re * pr - c_im * pi
            cp_im = c_re * pi + c_im * pr
            cpre_ref[0, d, :, e * I:(e + 1) * I, :] = cp_re
            cpimn_ref[0, d, :, e * I:(e + 1) * I, :] = -cp_im
            if e < S5_SUB:
                wre_ref[0, d, :, e * I:(e + 1) * I, :] = pr * bb_re - pi * bb_im
                wim_ref[0, d, :, e * I:(e + 1) * I, :] = pr * bb_im + pi * bb_re
                lag = (lax.dot_general(cp_re, bb_re, batched_nt, precision=HIGHEST, preferred_element_type=F32)
                       - lax.dot_general(cp_im, bb_im, batched_nt, precision=HIGHEST, preferred_element_type=F32))
                kt_ref[0, d, :, e * I:(e + 1) * I, :] = lag
                if e == 0:
                    k0_ref[0] = lag if d == 0 else k0_ref[0] + lag
                p_re, p_im = p_re * a_re - p_im * a_im, p_re * a_im + p_im * a_re
        pre_ref[0, d] = p_re
        pim_ref[0, d] = p_im


def _s5_operators(lam_re, lam_im, log_step, b_re, b_im, c_re, c_im):
    L, _, G, Pn = lam_re.shape
    I = S5_GROUP
    S = S5_SUB
    ls = jnp.broadcast_to(log_step[..., None], lam_re.shape)
    btr = jnp.swapaxes(b_re, -1, -2)
    bti = jnp.swapaxes(b_im, -1, -2)
    Gb = SUBLANES
    lam_spec = pl.BlockSpec((1, 2, Gb, Pn), lambda l, g: (l, 0, g, 0))
    gip_spec = pl.BlockSpec((1, Gb, I, Pn), lambda l, g: (l, g, 0, 0))

    def out_spec(rows, last):
        return pl.BlockSpec((1, 2, Gb, rows, last), lambda l, g: (l, 0, g, 0, 0))

    kt, k0, wre, wim, cpre, cpimn, pre, pim = pl.pallas_call(
        _s5prep_kernel,
        grid=(L, G // Gb),
        in_specs=[lam_spec, lam_spec, lam_spec, gip_spec, gip_spec, gip_spec, gip_spec],
        out_specs=[out_spec(S * I, I), pl.BlockSpec((1, Gb, I, I), lambda l, g: (l, g, 0, 0)),
                   out_spec(S * I, Pn), out_spec(S * I, Pn),
                   out_spec((S + 1) * I, Pn), out_spec((S + 1) * I, Pn), lam_spec, lam_spec],
        out_shape=[jax.ShapeDtypeStruct((L, 2, G, S * I, I), F32),
                   jax.ShapeDtypeStruct((L, G, I, I), F32),
                   jax.ShapeDtypeStruct((L, 2, G, S * I, Pn), F32),
                   jax.ShapeDtypeStruct((L, 2, G, S * I, Pn), F32),
                   jax.ShapeDtypeStruct((L, 2, G, (S + 1) * I, Pn), F32),
                   jax.ShapeDtypeStruct((L, 2, G, (S + 1) * I, Pn), F32),
                   jax.ShapeDtypeStruct((L, 2, G, Pn), F32),
                   jax.ShapeDtypeStruct((L, 2, G, Pn), F32)],
        compiler_params=_cparams("parallel", "parallel"),
        name="s5_prep",
    )(lam_re, lam_im, ls, btr, bti, c_re, c_im)

    kt = kt.reshape(L, 2, G, S, I, I)
    table = jnp.concatenate([kt[:, 0, :, 1:], kt[:, 1, :, 1:], k0[:, :, None]], axis=2)
    s_idx = jnp.arange(S)[:, None]
    t_idx = jnp.arange(S)[None, :]
    pick = jnp.where(t_idx > s_idx, t_idx - s_idx - 1,
                     jnp.where(t_idx < s_idx, (S - 1) + s_idx - t_idx - 1, 2 * S - 2))
    m = table[:, :, pick]
    m = m.transpose(0, 1, 3, 4, 2, 5).reshape(L, G, S * I, S * I)

    def by_pos(w, reverse):
        w = w.reshape(L, G, S, I, Pn)
        if reverse:
            w = w[:, :, ::-1]
        return w.reshape(L, G, S * I, Pn).swapaxes(-1, -2)

    wall = jnp.concatenate([by_pos(wre[:, 0], True), by_pos(wre[:, 1], False),
                            by_pos(wim[:, 0], True), by_pos(wim[:, 1], False)], axis=2)

    def readout(cp):
        cp = cp.reshape(L, 2, G, S + 1, I, Pn)
        f = cp[:, 0, :, 1:]
        b = cp[:, 1, :, 1:][:, :, ::-1]
        return jnp.concatenate([f.reshape(L, G, S * I, Pn), b.reshape(L, G, S * I, Pn)], axis=-1)

    vre = readout(cpre)
    vim = readout(cpimn)
    are = jnp.concatenate([pre[:, 0], pre[:, 1]], axis=-1)[:, :, None, :]
    aim = jnp.concatenate([pim[:, 0], pim[:, 1]], axis=-1)[:, :, None, :]
    return m, wall, vre, vim, are, aim


def _s5projt_kernel(x_ref, mod_ref, wt_ref, o_ref, *, D, R):
    Bm = mod_ref.shape[0]
    span = x_ref.shape[0] // Bm
    parts = []
    for b in range(Bm):
        shift = mod_ref[b, :, 0:D]
        scale = mod_ref[b, :, D:2 * D]
        parts.append((x_ref[b * span:(b + 1) * span, :] * (1.0 + scale) + shift).astype(BF16))
    h = parts[0] if Bm == 1 else jnp.concatenate(parts, axis=0)
    ut = lax.dot_general(wt_ref[...], h, (((1,), (1,)), ((), ())), preferred_element_type=F32)
    o_ref[...] = ut.reshape(o_ref.shape).astype(o_ref.dtype)


def _s5_proj_t(x2, mod, w_ut, B, T):
    N, D = x2.shape
    S = S5_SUB
    R = T // S
    G = D // S5_GROUP
    Bm = mod.shape[0]
    xv = x2.reshape(B * R, S * D)
    rows = B * R
    while rows > S5_PROJ_ROWS and rows % 2 == 0 and (rows // 2) % R == 0 and (rows // 2) % LANES == 0:
        rows //= 2
    nrb = B * R // rows
    mods = max(1, Bm * rows // (B * R))
    mod_spec = pl.BlockSpec((mods, 1, 6 * D), lambda i, s: (i if Bm > 1 else 0, 0, 0))
    return pl.pallas_call(
        functools.partial(_s5projt_kernel, D=D, R=R),
        grid=(nrb, S),
        in_specs=[pl.BlockSpec((rows, D), lambda i, s: (i, s)),
                  mod_spec,
                  pl.BlockSpec((D, D), lambda i, s: (0, 0))],
        out_specs=pl.BlockSpec((G, S5_GROUP, rows), lambda i, s: (0, s, i)),
        out_shape=jax.ShapeDtypeStruct((G, S * S5_GROUP, B * R), F32),
        compiler_params=_cparams("parallel", "parallel"),
        name="s5_proj_t",
    )(xv, mod, w_ut)


def _s5t_kernel(xt_ref, dcol_ref, mt_ref, wt_ref, vret_ref, vimt_ref, are_ref, aim_ref,
                s0re_ref, s0im_ref, yt_ref, fre_ref, fim_ref,
                zt_ref, zre_ref, zim_ref, hfre_ref, hfim_ref, hbre_ref, hbim_ref, *, R, B, chunk_major):
    Pn = S5_STATE
    xt32 = xt_ref[0]
    xt = xt32.astype(BF16)
    yt_ref[0] = _dot(mt_ref[0], xt) + dcol_ref[0] * xt32
    zt_ref[...] = jnp.dot(wt_ref[0].astype(BF16), xt, preferred_element_type=F32)
    if chunk_major:
        for b in range(B):
            zb = zt_ref[:, b * R:(b + 1) * R].T
            zre_ref[pl.ds(b, R, stride=B), :] = zb[:, :2 * Pn]
            zim_ref[pl.ds(b, R, stride=B), :] = zb[:, 2 * Pn:]
        step_rows = lambda r: pl.ds(pl.multiple_of(r * B, B), B)
    else:
        z = zt_ref[...].T
        zre_ref[...] = z[:, :2 * Pn]
        zim_ref[...] = z[:, 2 * Pn:]
        step_rows = lambda r: pl.ds(r, B, stride=R)
    a_re = are_ref[0]
    a_im = aim_ref[0]
    h0 = (s0re_ref[0], s0im_ref[0])

    def advance(h, rows):
        hr, hi = h
        return (hr * a_re - hi * a_im + zre_ref[rows, :], hr * a_im + hi * a_re + zim_ref[rows, :])

    def step(r, carry):
        hf, hb = carry
        rows_f = step_rows(r)
        rows_b = step_rows(R - 1 - r)
        hfre_ref[rows_f, :] = hf[0]
        hfim_ref[rows_f, :] = hf[1]
        hbre_ref[rows_b, :] = hb[0]
        hbim_ref[rows_b, :] = hb[1]
        return advance(hf, rows_f), advance(hb, rows_b)

    hf, hb = lax.fori_loop(0, R, step, (h0, h0), unroll=SCAN_UNROLL)

    def own_half(f, b):
        return jnp.where(lax.broadcasted_iota(jnp.int32, f.shape, 1) < Pn, f, b)

    fre_ref[0] = own_half(hf[0], hb[0])
    fim_ref[0] = own_half(hf[1], hb[1])
    vret = vret_ref[0]
    vimt = vimt_ref[0]
    if chunk_major:
        for b in range(B):
            sel = pl.ds(b, R, stride=B)
            hre_b = own_half(hfre_ref[sel, :], hbre_ref[sel, :])
            him_b = own_half(hfim_ref[sel, :], hbim_ref[sel, :])
            cols = slice(b * R, (b + 1) * R)
            yt_ref[0, :, cols] = yt_ref[0, :, cols] + _dot_nt(vret, hre_b) + _dot_nt(vimt, him_b)
    else:
        hre = own_half(hfre_ref[...], hbre_ref[...])
        him = own_half(hfim_ref[...], hbim_ref[...])
        yt_ref[0] = yt_ref[0] + _dot_nt(vret, hre) + _dot_nt(vimt, him)


def _s5_core_t(xt, dcol, ops, s0re, s0im, R, B):
    G, K, BR = xt.shape
    mt, wt, vret, vimt, are, aim = ops
    P2 = 2 * S5_STATE
    chunk_major = B == SUBLANES and R % LANES == 0
    gspec = lambda shape: pl.BlockSpec((1,) + shape, lambda g: (g, 0, 0))
    return pl.pallas_call(
        functools.partial(_s5t_kernel, R=R, B=B, chunk_major=chunk_major),
        grid=(G,),
        in_specs=[gspec((K, BR)), gspec((K, 1)), gspec((K, K)), gspec((K, K)),
                  gspec((K, P2)), gspec((K, P2)), gspec((1, P2)), gspec((1, P2)),
                  gspec((B, P2)), gspec((B, P2))],
        out_specs=[gspec((K, BR)), gspec((B, P2)), gspec((B, P2))],
        out_shape=[jax.ShapeDtypeStruct((G, K, BR), F32),
                   jax.ShapeDtypeStruct((G, B, P2), F32),
                   jax.ShapeDtypeStruct((G, B, P2), F32)],
        scratch_shapes=[pltpu.VMEM((K, BR), F32)] + [pltpu.VMEM((BR, P2), F32) for _ in range(6)],
        compiler_params=_cparams("parallel"),
        name="s5_core_t",
    )(xt, dcol, mt, wt, vret, vimt, are, aim, s0re, s0im)


def _s5rows_kernel(yt_ref, o_ref):
    I = S5_GROUP
    nc = yt_ref.shape[-1]
    for t in range(S5_SUB):
        blk = yt_ref[:, t * I:(t + 1) * I, :]
        o_ref[pl.ds(t, nc, stride=S5_SUB), :] = blk.reshape(LANES, nc).T


def _s5_rows(yt, B, T, D):
    G = D // S5_GROUP
    BR = B * T // S5_SUB
    gpt = LANES // S5_GROUP
    nc = next(n for n in (4 * LANES, 2 * LANES, LANES) if BR % n == 0)
    return pl.pallas_call(
        _s5rows_kernel,
        grid=(BR // nc, G // gpt),
        in_specs=[pl.BlockSpec((gpt, S5_SUB * S5_GROUP, nc), lambda i, j: (j, 0, i))],
        out_specs=pl.BlockSpec((nc * S5_SUB, LANES), lambda i, j: (i, j)),
        out_shape=jax.ShapeDtypeStruct((B * T, D), F32),
        compiler_params=_cparams("parallel", "parallel"),
        name="s5_rows",
    )(yt)


def _lru_kernel(*refs, Tt, CB, reverse, has_init):
    refs = list(refs)
    lx_ref, prev_ref, next_ref, cw_ref, cb_ref, wg_ref, bgate_ref, lam_ref = refs[:8]
    pos = 8
    h0_ref = None
    if has_init:
        h0_ref = refs[pos]
        pos += 1
    if reverse:
        hf_ref, ly_ref = refs[pos:pos + 2]
        pos += 2
    out_ref, fin_ref, x_scr, a_scr, b_scr, h_scr, carry_ref = refs[pos:pos + 7]
    NBk = CB // LRU_BLOCK
    k = pl.program_id(2)
    nk = pl.num_programs(2)
    first_tile = k == 0

    @pl.when(first_tile)
    def _():
        if has_init:
            carry_ref[...] = h0_ref[...]
        else:
            carry_ref[...] = jnp.zeros((SUBLANES, CB), F32)

    tpos = (nk - 1 - k) if reverse else k
    has_prev = tpos > 0
    has_next = tpos < nk - 1
    cw = cw_ref[0]
    cbias = cb_ref[0]
    bgate = bgate_ref[0]
    lam = lam_ref[0]
    half_cfac = (-0.5 * LRU_C) * (jnp.maximum(-lam, 0.0) + jnp.log1p(jnp.exp(-jnp.abs(lam))))

    left = LRU_CONV // 2
    for b in range(SUBLANES):
        cur = lx_ref[b]
        pv = jnp.where(has_prev, prev_ref[b], 0.0)
        nx = jnp.where(has_next, next_ref[b], 0.0)
        for c in range(NBk):
            sl = slice(c * LRU_BLOCK, (c + 1) * LRU_BLOCK)
            x_scr[c, pl.ds(left * SUBLANES + b, Tt, stride=SUBLANES), :] = cur[:, sl]
            for q in range(left):
                x_scr[c, pl.ds(q * SUBLANES + b, 1), :] = pv[SUBLANES - left + q:SUBLANES - left + q + 1, sl]
            for q in range(LRU_CONV - 1 - left):
                x_scr[c, pl.ds((Tt + left + q) * SUBLANES + b, 1), :] = nx[q:q + 1, sl]

    RC = min(Tt * SUBLANES, 512)
    for c in range(NBk):
        sl = slice(c * LRU_BLOCK, (c + 1) * LRU_BLOCK)
        bias_c = jnp.concatenate([bgate[:, sl], bgate[:, CB + c * LRU_BLOCK: CB + (c + 1) * LRU_BLOCK]], axis=1)
        for r0 in range(0, Tt * SUBLANES, RC):
            xc = cbias[:, sl] + sum(cw[j:j + 1, sl] * x_scr[c, pl.ds(r0 + j * SUBLANES, RC), :]
                                    for j in range(LRU_CONV))
            th = jnp.tanh(_dot(xc, wg_ref[0, c]) + bias_c)
            log_a = half_cfac[:, sl] * th[:, :LRU_BLOCK] + half_cfac[:, sl]
            a = jnp.exp(log_a)
            om = -jnp.tanh(log_a) * (1.0 + a * a)
            root = om * lax.rsqrt(jnp.maximum(om, F32_TINY))
            bt = root * ((0.5 * th[:, LRU_BLOCK:] + 0.5) * xc)
            a_scr[c, pl.ds(r0, RC), :] = a
            b_scr[c, pl.ds(r0, RC), :] = bt

    def step(s, h):
        t = (Tt - 1 - s) if reverse else s
        rows = pl.ds(pl.multiple_of(t * SUBLANES, SUBLANES), SUBLANES)
        new = []
        for c in range(NBk):
            hc = a_scr[c, rows, :] * h[c] + b_scr[c, rows, :]
            h_scr[c, rows, :] = hc
            new.append(hc)
        return tuple(new)

    h_init = tuple(carry_ref[:, c * LRU_BLOCK:(c + 1) * LRU_BLOCK] for c in range(NBk))
    h_last = lax.fori_loop(0, Tt, step, h_init, unroll=2 * SCAN_UNROLL)
    for c in range(NBk):
        carry_ref[:, c * LRU_BLOCK:(c + 1) * LRU_BLOCK] = h_last[c]
        fin_ref[:, c * LRU_BLOCK:(c + 1) * LRU_BLOCK] = h_last[c]

    for b in range(SUBLANES):
        hb = jnp.concatenate([h_scr[c, pl.ds(b, Tt, stride=SUBLANES), :] for c in range(NBk)], axis=1)
        if reverse:
            out_ref[b] = (hf_ref[b] + hb) * jax.nn.gelu(ly_ref[b])
        else:
            out_ref[b] = hb


def _lru_sweep(P3, lw, h0, hf, reverse, B, T, D):
    W = 3 * D // 2
    CB = 2 * LRU_BLOCK
    Tt = min(T, 256)
    nk = T // Tt
    nb8 = Tt // SUBLANES
    lxcol = (D // 2 + D // 2 + D + D) // CB
    lycol = lxcol + W // CB
    d = 1 if reverse else 0
    has_init = h0 is not None

    def tmap(k):
        return (nk - 1 - k) if reverse else k

    in_specs = [
        pl.BlockSpec((SUBLANES, Tt, CB), lambda g, j, k: (g, tmap(k), lxcol + j)),
        pl.BlockSpec((SUBLANES, SUBLANES, CB),
                     lambda g, j, k: (g, jnp.maximum(tmap(k) * nb8 - 1, 0), lxcol + j)),
        pl.BlockSpec((SUBLANES, SUBLANES, CB),
                     lambda g, j, k: (g, jnp.minimum((tmap(k) + 1) * nb8, T // SUBLANES - 1), lxcol + j)),
        pl.BlockSpec((1, LRU_CONV, CB), lambda g, j, k: (j, 0, 0)),
        pl.BlockSpec((1, 1, CB), lambda g, j, k: (j, 0, 0)),
        pl.BlockSpec((1, CB // LRU_BLOCK, LRU_BLOCK, 2 * LRU_BLOCK), lambda g, j, k: (j, 0, 0, 0)),
        pl.BlockSpec((1, 1, 2 * CB), lambda g, j, k: (j, 0, 0)),
        pl.BlockSpec((1, 1, CB), lambda g, j, k: (j, 0, 0)),
    ]
    args = [P3, P3, P3, lw["conv_w"], lw["conv_b"], lw["wg"][d], lw["bg"][d], lw["lam"][d]]
    if has_init:
        in_specs.append(pl.BlockSpec((SUBLANES, CB), lambda g, j, k: (g, j)))
        args.append(h0[:, d])
    if reverse:
        in_specs.append(pl.BlockSpec((SUBLANES, Tt, CB), lambda g, j, k: (g, tmap(k), j)))
        in_specs.append(pl.BlockSpec((SUBLANES, Tt, CB), lambda g, j, k: (g, tmap(k), lycol + j)))
        args += [hf, P3]
    out, fin = pl.pallas_call(
        functools.partial(_lru_kernel, Tt=Tt, CB=CB, reverse=reverse, has_init=has_init),
        grid=(B // SUBLANES, W // CB, nk),
        in_specs=in_specs,
        out_specs=[pl.BlockSpec((SUBLANES, Tt, CB), lambda g, j, k: (g, tmap(k), j)),
                   pl.BlockSpec((SUBLANES, CB), lambda g, j, k: (g, j))],
        out_shape=[jax.ShapeDtypeStruct((B, T, W), F32), jax.ShapeDtypeStruct((B, W), F32)],
        scratch_shapes=[pltpu.VMEM((CB // LRU_BLOCK, (Tt + LRU_CONV) * SUBLANES, LRU_BLOCK), F32)]
        + [pltpu.VMEM((CB // LRU_BLOCK, Tt * SUBLANES, LRU_BLOCK), F32) for _ in range(3)]
        + [pltpu.VMEM((SUBLANES, CB), F32)],
        compiler_params=_cparams("parallel", "parallel", "arbitrary"),
        name="lru_bwd" if reverse else "lru_fwd",
    )(*args)
    return out, fin


def _merge_kernel(x_ref, mod_ref, o_ref, r_ref, ys_ref, yl_ref, gg_ref, gs_ref, gl_ref,
                  gn_ref, wglu_ref, wbg_ref, wbs_ref, wbl_ref, wo_ref, lg_ref, lb_ref,
                  out_ref, *, D, alpha):
    DV = D // GLA_HEADS
    gn = gn_ref[...]
    parts = []
    for h in range(GLA_HEADS):
        sl = slice(h * DV, (h + 1) * DV)
        o = o_ref[:, sl]
        mu = jnp.mean(o, axis=-1, keepdims=True)
        oc = o - mu
        var = jnp.mean(oc * oc, axis=-1, keepdims=True)
        parts.append(oc * lax.rsqrt(var + LN_EPS) * gn[:, sl] * _silu(r_ref[:, sl]))
    y_gla = jnp.concatenate(parts, axis=1)
    ys = jax.nn.gelu(ys_ref[...])
    y_s5 = ys * _sigmoid(_dot(ys, wglu_ref[...]))
    merged = (_sigmoid(gg_ref[...]) * _dot(y_gla, wbg_ref[...])
              + _sigmoid(gs_ref[...]) * _dot(y_s5, wbs_ref[...])
              + _sigmoid(gl_ref[...]) * _dot(yl_ref[...], wbl_ref[...]))
    mix = _dot(merged, wo_ref[...])
    gate1 = mod_ref[0, :, 2 * D:3 * D]
    out_ref[...] = _layer_norm(alpha * x_ref[...] + gate1 * mix, lg_ref[...], lb_ref[...])


def _merge(x2, mod, o_gla, P, ys, ylru, lw, T, alpha):
    N, D = x2.shape
    W = 3 * D // 2
    Bm = mod.shape[0]
    rcol, gcol = 2, 6
    tm = _row_tile(N, T, Bm > 1, 256)
    per_b = T // tm
    mod_map = (lambda i: (i // per_b, 0, 0)) if Bm > 1 else (lambda i: (0, 0, 0))
    row = lambda width, cb=0: pl.BlockSpec((tm, width), lambda i: (i, cb))
    full = lambda a: pl.BlockSpec(a.shape, lambda i: (0,) * a.ndim)
    weights = [lw["gnorm"], lw["w_glu"], lw["w_br_gla"], lw["w_br_s5"], lw["w_br_lru"],
               lw["w_out"], lw["ln1_g"], lw["ln1_b"]]
    return pl.pallas_call(
        functools.partial(_merge_kernel, D=D, alpha=alpha),
        grid=(N // tm,),
        in_specs=[row(D), pl.BlockSpec((1, 1, 6 * D), mod_map), row(D), row(D, rcol), row(D),
                  row(W), row(D, gcol), row(D, gcol + 1), row(D, gcol + 2)]
        + [full(w) for w in weights],
        out_specs=row(D),
        out_shape=jax.ShapeDtypeStruct((N, D), F32),
        compiler_params=_cparams("parallel"),
        name="merge",
    )(x2, mod, o_gla, P, ys, ylru, P, P, P, *weights)


def _mlp_kernel(x_ref, mod_ref, w1_ref, w2_ref, lg_ref, lb_ref, out_ref, h_ref, acc_ref, *, D, alpha):
    j = pl.program_id(1)

    @pl.when(j == 0)
    def _():
        shift = mod_ref[0, :, 3 * D:4 * D]
        scale = mod_ref[0, :, 4 * D:5 * D]
        h_ref[...] = (x_ref[...] * (1.0 + scale) + shift).astype(BF16)
        acc_ref[...] = jnp.zeros_like(acc_ref)

    hid = jnp.dot(h_ref[...], w1_ref[...], preferred_element_type=F32)
    hid = jnp.square(jnp.maximum(hid, 0.0))
    acc_ref[...] += _dot(hid, w2_ref[...])

    @pl.when(j == pl.num_programs(1) - 1)
    def _():
        gate2 = mod_ref[0, :, 5 * D:6 * D]
        out_ref[...] = _layer_norm(alpha * x_ref[...] + gate2 * acc_ref[...], lg_ref[...], lb_ref[...])


def _mlp(x2, mod, lw, T, alpha):
    N, D = x2.shape
    HID = lw["w_mlp_in"].shape[1]
    Bm = mod.shape[0]
    tm = _row_tile(N, T, Bm > 1, 1024)
    th = min(HID, 1024)
    per_b = T // tm
    mod_map = (lambda i, j: (i // per_b, 0, 0)) if Bm > 1 else (lambda i, j: (0, 0, 0))
    return pl.pallas_call(
        functools.partial(_mlp_kernel, D=D, alpha=alpha),
        grid=(N // tm, HID // th),
        in_specs=[pl.BlockSpec((tm, D), lambda i, j: (i, 0)),
                  pl.BlockSpec((1, 1, 6 * D), mod_map),
                  pl.BlockSpec((D, th), lambda i, j: (0, j)),
                  pl.BlockSpec((th, D), lambda i, j: (j, 0)),
                  pl.BlockSpec((1, D), lambda i, j: (0, 0)),
                  pl.BlockSpec((1, D), lambda i, j: (0, 0))],
        out_specs=pl.BlockSpec((tm, D), lambda i, j: (i, 0)),
        out_shape=jax.ShapeDtypeStruct((N, D), F32),
        scratch_shapes=[pltpu.VMEM((tm, D), BF16), pltpu.VMEM((tm, D), F32)],
        compiler_params=_cparams("parallel", "arbitrary"),
        name="mlp",
    )(x2, mod, lw["w_mlp_in"], lw["w_mlp_out"], lw["ln2_g"], lw["ln2_b"])


def _pack_layer_weights(l, D, w_in, gla_w_gate, gla_b_gate, gla_norm_g, s5_d, s5_w_glu,
                        lru_conv_w, lru_conv_b, lru_w_a, lru_b_a, lru_w_i, lru_b_i, lru_lam,
                        w_br_gla, w_br_s5, w_br_lru, w_out, ln1_g, ln1_b, ln2_g, ln2_b,
                        w_mlp_in, w_mlp_out):
    H = GLA_HEADS
    KEY = D // 2
    DK = KEY // H
    W = 3 * D // 2
    CB = 2 * LRU_BLOCK
    NB = W // LRU_BLOCK
    widths = (KEY, KEY, D, D, 2 * GATE_RANK, D, W, W, 3 * D)
    offs = [0]
    for wd in widths:
        offs.append(offs[-1] + wd)
    wl = w_in[l]
    piece = lambda i: wl[:, offs[i]:offs[i + 1]]
    w_pack = jnp.concatenate([piece(0), piece(1), piece(2), piece(3), piece(6), piece(7),
                              piece(8)], axis=1).astype(BF16)
    w_glr = jnp.concatenate([piece(4), jnp.zeros((D, LANES - 2 * GATE_RANK), wl.dtype)], axis=1).astype(BF16)
    w_ut = piece(5).T.astype(BF16)
    wgate = gla_w_gate[l]
    wg = jnp.zeros((H, LANES, 2 * DK), F32)
    wg = wg.at[:, 0:GATE_RANK, 0:DK].set(wgate[0].reshape(GATE_RANK, H, DK).transpose(1, 0, 2))
    wg = wg.at[:, GATE_RANK:2 * GATE_RANK, DK:].set(wgate[1].reshape(GATE_RANK, H, DK).transpose(1, 0, 2))
    bgate = gla_b_gate[l].reshape(2, H, DK).transpose(1, 0, 2).reshape(H, 1, 2 * DK)
    lru_wg = 0.5 * jnp.concatenate([lru_w_a[l], lru_w_i[l]], axis=-1)
    lru_wg = lru_wg.reshape(2, W // CB, CB // LRU_BLOCK, LRU_BLOCK, 2 * LRU_BLOCK).astype(BF16)
    lru_bg = 0.5 * jnp.concatenate([lru_b_a[l].reshape(2, W // CB, 1, CB), lru_b_i[l].reshape(2, W // CB, 1, CB)],
                                   axis=-1)
    return {
        "w_pack": w_pack, "w_glr": w_glr, "w_ut": w_ut, "gla_wg": wg.astype(BF16), "gla_bg": bgate,
        "lru": {"conv_w": lru_conv_w[l].reshape(LRU_CONV, W // CB, CB).transpose(1, 0, 2),
                "conv_b": lru_conv_b[l].reshape(W // CB, 1, CB),
                "wg": lru_wg, "bg": lru_bg, "lam": lru_lam[l].reshape(2, W // CB, 1, CB)},
        "gnorm": gla_norm_g[l].reshape(1, D), "s5_dcol": jnp.tile(s5_d[l].reshape(D // S5_GROUP, 1, S5_GROUP), (1, S5_SUB, 1)).reshape(
            D // S5_GROUP, S5_SUB * S5_GROUP, 1),
        "w_glu": s5_w_glu[l].astype(BF16), "w_br_gla": w_br_gla[l].astype(BF16),
        "w_br_s5": w_br_s5[l].astype(BF16), "w_br_lru": w_br_lru[l].astype(BF16),
        "w_out": w_out[l].astype(BF16),
        "ln1_g": ln1_g[l].reshape(1, D), "ln1_b": ln1_b[l].reshape(1, D),
        "ln2_g": ln2_g[l].reshape(1, D), "ln2_b": ln2_b[l].reshape(1, D),
        "w_mlp_in": w_mlp_in[l].astype(BF16), "w_mlp_out": w_mlp_out[l].astype(BF16),
    }


def _trunk_layer(x2, mod, lw, s5ops, init, B, T, D, alpha):
    N = B * T
    G = D // S5_GROUP
    S = S5_SUB
    R = T // S
    Pn = S5_STATE
    P, glr = _in_proj(x2, mod, lw["w_pack"], lw["w_glr"], T)
    gla0 = s5re0 = s5im0 = lru0 = None
    if init is not None:
        gla0, s5re0, s5im0, lru0 = init
        s5re0 = s5re0.transpose(2, 0, 1, 3).reshape(G, B, 2 * Pn)
        s5im0 = s5im0.transpose(2, 0, 1, 3).reshape(G, B, 2 * Pn)
    o_gla, gla_fin = _gla_core(P, glr, lw["gla_wg"], lw["gla_bg"], gla0, B, T, D)
    if s5re0 is None:
        s5re0 = jnp.zeros((G, B, 2 * Pn), F32)
        s5im0 = jnp.zeros((G, B, 2 * Pn), F32)
    xt = _s5_proj_t(x2, mod, lw["w_ut"], B, T)
    yt, fre, fim = _s5_core_t(xt, lw["s5_dcol"], s5ops, s5re0, s5im0, R, B)
    ys = _s5_rows(yt, B, T, D)
    s5re_fin = fre.reshape(G, B, 2, Pn).transpose(1, 2, 0, 3)
    s5im_fin = fim.reshape(G, B, 2, Pn).transpose(1, 2, 0, 3)
    P3 = P.reshape(B, T, P.shape[1])
    hf, lf = _lru_sweep(P3, lw["lru"], lru0, None, False, B, T, D)
    ylru, lb = _lru_sweep(P3, lw["lru"], lru0, hf, True, B, T, D)
    lru_fin = jnp.stack([lf, lb], axis=1)
    x1 = _merge(x2, mod, o_gla, P, ys, ylru.reshape(N, -1), lw, T, alpha)
    x3 = _mlp(x1, mod, lw, T, alpha)
    return x3, (gla_fin, s5re_fin, s5im_fin, lru_fin)


def kernel(x_prompt, x_sample, state_gla, state_s5_re, state_s5_im, state_lru, c, c_ctx, w_ada, b_ada, w_in, gla_w_gate, gla_b_gate, gla_norm_g, s5_lam_re, s5_lam_im, s5_log_step, s5_b_re, s5_b_im, s5_c_re, s5_c_im, s5_d, s5_w_glu, lru_conv_w, lru_conv_b, lru_w_a, lru_b_a, lru_w_i, lru_b_i, lru_lam, w_br_gla, w_br_s5, w_br_lru, w_out, ln1_g, ln1_b, ln2_g, ln2_b, w_mlp_in, w_mlp_out):
    Bp, Tp, D = x_prompt.shape
    Bs, Ts, _ = x_sample.shape
    L = w_in.shape[0]
    alpha = (2.0 * L) ** 0.25

    n_rows = -(-(Bs + 1) // SUBLANES) * SUBLANES
    cc = jnp.concatenate([c, c_ctx[None], jnp.zeros((n_rows - Bs - 1, D), F32)], axis=0)
    mod = _ada_mod(cc, w_ada, b_ada)
    s5ops_all = _s5_operators(s5_lam_re, s5_lam_im, s5_log_step, s5_b_re, s5_b_im, s5_c_re, s5_c_im)

    xp = x_prompt.reshape(Bp * Tp, D)
    xs = _add_pos(x_sample, _grid_pos_table(Ts, D)).reshape(Bs * Ts, D)
    fins = []
    for l in range(L):
        lw = _pack_layer_weights(l, D, w_in, gla_w_gate, gla_b_gate, gla_norm_g, s5_d, s5_w_glu,
                                 lru_conv_w, lru_conv_b, lru_w_a, lru_b_a, lru_w_i, lru_b_i, lru_lam,
                                 w_br_gla, w_br_s5, w_br_lru, w_out, ln1_g, ln1_b, ln2_g, ln2_b,
                                 w_mlp_in, w_mlp_out)
        s5ops = tuple(a[l] for a in s5ops_all)
        mod_ctx = mod[l, Bs:Bs + 1].reshape(1, 1, 6 * D)
        mod_lat = mod[l, :Bs].reshape(Bs, 1, 6 * D)
        xp, fin = _trunk_layer(xp, mod_ctx, lw, s5ops, None, Bp, Tp, D, alpha)
        fins.append(fin)
        cache = (state_gla[:, l], state_s5_re[:, l], state_s5_im[:, l], state_lru[:, l])
        xs, _ = _trunk_layer(xs, mod_lat, lw, s5ops, cache, Bs, Ts, D, alpha)
    sdt = x_prompt.dtype
    new_states = tuple(jnp.stack([f[i] for f in fins], axis=1).astype(sdt) for i in range(4))
    return (xp.reshape(Bp, Tp, D), xs.reshape(Bs, Ts, D)) + new_states
```
